```python
import math
import jax
import jax.numpy as jnp
from jax import lax
import numpy as np

D_MODEL = 1024
BATCH = 1
SEQ = 16384
DEPTH = 2

GRID_W = 64
CTX_LEN = 256
HEAD_DIM = 64
Q_BLOCK = 128
EPS = 1e-6
ROPE_THETA = 10000.0
D_FF = 2816
ADA_STD = 0.02
N_MOD = 9

A_HEADS = 8
A_KV_HEADS = 2
B_HEADS = 8
B_NOPE = 64
B_ROPE = 32
B_VDIM = 64
B_KV_RANK = 256
B_QK = B_NOPE + B_ROPE
C_HEADS = 8
C_DK = 64
C_DV = 64
C_CHUNK = 64
D_HEADS = 8
D_WIN_H = 8
D_WIN_W = 16

N_EVEN = (DEPTH + 1) // 2
N_ODD = DEPTH // 2

A_QW = A_HEADS * HEAD_DIM
A_KVW = A_KV_HEADS * HEAD_DIM
B_QW = B_HEADS * B_QK
AB_IN = A_QW + 2 * A_KVW + B_QW + B_KV_RANK + B_ROPE
AB_SPLITS = (A_QW, A_QW + A_KVW, A_QW + 2 * A_KVW, A_QW + 2 * A_KVW + B_QW, A_QW + 2 * A_KVW + B_QW + B_KV_RANK)
AB_OUT = A_QW + B_HEADS * B_VDIM

C_W = C_HEADS * C_DK
C_VW = C_HEADS * C_DV
D_W = D_HEADS * HEAD_DIM
CD_IN = 3 * C_W + 2 * C_VW + 3 * D_W
CD_SPLITS = (C_W, 2 * C_W, 3 * C_W, 3 * C_W + C_VW, 3 * C_W + 2 * C_VW, 3 * C_W + 2 * C_VW + D_W, 3 * C_W + 2 * C_VW + 2 * D_W)
CD_OUT = C_VW + D_W

kernel_name = 'hybrid_dit_gqa_mla_hgrn2_natten'


def rms_norm(x, g):
    xf = x.astype(jnp.float32)
    y = xf * lax.rsqrt(jnp.mean(xf * xf, axis=-1, keepdims=True) + EPS)
    return (y * g.astype(jnp.float32)).astype(x.dtype)


def modulate(h, shift, scale):
    return h * (1 + scale[:, None, :]) + shift[:, None, :]


def swiglu(h, w1, w3, w2):
    return (jax.nn.silu(h @ w1) * (h @ w3)) @ w2


def half_ffn(h, g, shift, scale, gate, w1, w3, w2):
    return h + 0.5 * gate[:, None, :] * swiglu(modulate(rms_norm(h, g), shift, scale), w1, w3, w2)


def rope_1d(pos, dim):
    inv = ROPE_THETA ** (-jnp.arange(0, dim, 2, dtype=jnp.float32) / dim)
    ang = pos.astype(jnp.float32)[:, None] * inv[None, :]
    ang = jnp.concatenate([ang, ang], axis=-1)
    return jnp.cos(ang), jnp.sin(ang)


def axial_rope_tables(n_tok, dim):
    t = jnp.arange(n_tok, dtype=jnp.int32)
    cos_r, sin_r = rope_1d(t // GRID_W, dim // 2)
    cos_c, sin_c = rope_1d(t % GRID_W, dim // 2)
    return jnp.concatenate([cos_r, cos_c], axis=-1), jnp.concatenate([sin_r, sin_c], axis=-1)


def _rotate_half(u):
    u1, u2 = jnp.split(u, 2, axis=-1)
    return jnp.concatenate([-u2, u1], axis=-1)


def apply_axial_rope(x, cos, sin):
    h = x.shape[-1] // 2
    rot = jnp.concatenate([_rotate_half(x[..., :h]), _rotate_half(x[..., h:])], axis=-1)
    return x * cos[None, :, None, :].astype(x.dtype) + rot * sin[None, :, None, :].astype(x.dtype)


def blocked_attention(q, k, v, scale):
    b, s, hkv, g, d = q.shape
    nb = s // Q_BLOCK
    qb = jnp.moveaxis(q.reshape(b, nb, Q_BLOCK, hkv, g, d), 1, 0)

    def one_block(qi):
        sc = jnp.einsum('bqhgd,bkhd->bhgqk', qi, k, preferred_element_type=jnp.float32) * scale
        p = jax.nn.softmax(sc, axis=-1).astype(v.dtype)
        return jnp.einsum('bhgqk,bkhd->bqhgd', p, v)

    out = lax.map(one_block, qb)
    return jnp.moveaxis(out, 0, 1).reshape(b, s, hkv * g * v.shape[-1])


def mixer_ab(hx, hc, w_in, a_qn, a_kn, b_qn, b_kn, b_kvn, b_wukv, w_out, with_ctx):
    b, s, _ = hx.shape
    grp = A_HEADS // A_KV_HEADS
    cos_a, sin_a = axial_rope_tables(s, HEAD_DIM)
    cos_b, sin_b = axial_rope_tables(s, B_ROPE)

    def project(h, rotary):
        n = h.shape[1]
        aq, ak, av, bq, bkv, bkr = jnp.split(h @ w_in, AB_SPLITS, axis=-1)
        aq = rms_norm(aq.reshape(b, n, A_HEADS, HEAD_DIM), a_qn)
        ak = rms_norm(ak.reshape(b, n, A_KV_HEADS, HEAD_DIM), a_kn)
        av = av.reshape(b, n, A_KV_HEADS, HEAD_DIM)
        kv = (rms_norm(bkv, b_kvn) @ b_wukv).reshape(b, n, B_HEADS, B_NOPE + B_VDIM)
        bk_rope = jnp.broadcast_to(bkr[:, :, None, :], (b, n, B_HEADS, B_ROPE))
        bq = rms_norm(bq.reshape(b, n, B_HEADS, B_QK), b_qn)
        bk = rms_norm(jnp.concatenate([kv[..., :B_NOPE], bk_rope], axis=-1), b_kn)
        bv = kv[..., B_NOPE:]
        if rotary:
            aq = apply_axial_rope(aq, cos_a, sin_a)
            ak = apply_axial_rope(ak, cos_a, sin_a)
            bq = jnp.concatenate([bq[..., :B_NOPE], apply_axial_rope(bq[..., B_NOPE:], cos_b, sin_b)], axis=-1)
            bk = jnp.concatenate([bk[..., :B_NOPE], apply_axial_rope(bk[..., B_NOPE:], cos_b, sin_b)], axis=-1)
        return aq.reshape(b, n, A_KV_HEADS, grp, HEAD_DIM), ak, av, bq[:, :, :, None, :], bk, bv

    qa, ka, va, qb, kb, vb = project(hx, True)
    qa_c, ka_c, va_c, qb_c, kb_c, vb_c = project(hc, False)
    y_x = jnp.concatenate([
        blocked_attention(qa, jnp.concatenate([ka, ka_c], axis=1), jnp.concatenate([va, va_c], axis=1), HEAD_DIM ** -0.5),
        blocked_attention(qb, jnp.concatenate([kb, kb_c], axis=1), jnp.concatenate([vb, vb_c], axis=1), B_QK ** -0.5),
    ], axis=-1) @ w_out
    if not with_ctx:
        return y_x, None
    y_c = jnp.concatenate([
        blocked_attention(qa_c, ka_c, va_c, HEAD_DIM ** -0.5),
        blocked_attention(qb_c, kb_c, vb_c, B_QK ** -0.5),
    ], axis=-1) @ w_out
    return y_x, y_c


def hgrn_chunk_scan(q, k, v, g, s0, with_output):
    b, h, n, dk = q.shape
    nc = n // C_CHUNK

    def chunks(t):
        return jnp.moveaxis(t.reshape(b, h, nc, C_CHUNK, t.shape[-1]), 2, 0)

    tri = jnp.tril(jnp.ones((C_CHUNK, C_CHUNK), dtype=bool))[:, :, None]

    def step(state, inp):
        qi, ki, vi, gi = inp
        gcum = jnp.cumsum(gi, axis=2)
        glast = gcum[:, :, -1, :]
        new_state = jnp.exp(glast)[..., None] * state + jnp.einsum('bhck,bhcv->bhkv', ki * jnp.exp(glast[:, :, None, :] - gcum), vi)
        if not with_output:
            return new_state, None
        o_inter = jnp.einsum('bhtk,bhkv->bhtv', qi * jnp.exp(gcum), state)
        decay = jnp.exp(jnp.where(tri, gcum[:, :, :, None, :] - gcum[:, :, None, :, :], -jnp.inf))
        attn = jnp.einsum('bhtk,bhsk,bhtsk->bhts', qi, ki, decay)
        return new_state, o_inter + jnp.einsum('bhts,bhsv->bhtv', attn, vi)

    s_fin, o = lax.scan(step, s0, (chunks(q), chunks(k), chunks(v), chunks(g)))
    if with_output:
        o = jnp.moveaxis(o, 0, 2).reshape(b, h, n, v.shape[-1])
    return s_fin, o


def neighbourhood_attention(q, k, v, k_ctx, v_ctx, rpb):
    b, s, h, d = q.shape
    rows = s // GRID_W
    kh = min(D_WIN_H, rows)
    kw = D_WIN_W
    n_win = kh * kw
    cols = np.arange(GRID_W)
    col_idx = np.clip(cols - kw // 2, 0, GRID_W - kw)[:, None] + np.arange(kw)[None, :]
    col_rel = col_idx - cols[:, None] + (D_WIN_W - 1)
    rpb_cols = rpb[:, :, col_rel]
    kg = k.reshape(b, rows, GRID_W, h, d)
    vg = v.reshape(b, rows, GRID_W, h, d)
    q_rows = jnp.moveaxis(q.reshape(b, rows, GRID_W, h, d), 1, 0)
    scale = d ** -0.5

    def one_row(args):
        r, q_row = args
        rs = jnp.clip(r - kh // 2, 0, rows - kh)
        k_win = lax.dynamic_slice_in_dim(kg, rs, kh, axis=1)[:, :, col_idx]
        v_win = lax.dynamic_slice_in_dim(vg, rs, kh, axis=1)[:, :, col_idx]
        row_rel = rs + jnp.arange(kh) - r + (D_WIN_H - 1)
        bias = jnp.transpose(jnp.take(rpb_cols, row_rel, axis=1), (0, 2, 1, 3))
        s_win = jnp.einsum('bchd,brcwhd->bhcrw', q_row, k_win, preferred_element_type=jnp.float32) * scale + bias[None].astype(jnp.float32)
        s_ctx = jnp.einsum('bchd,bnhd->bhcn', q_row, k_ctx, preferred_element_type=jnp.float32) * scale
        p = jax.nn.softmax(jnp.concatenate([s_win.reshape(b, h, GRID_W, n_win), s_ctx], axis=-1), axis=-1).astype(v.dtype)
        return (jnp.einsum('bhcrw,brcwhd->bchd', p[..., :n_win].reshape(b, h, GRID_W, kh, kw), v_win)
                + jnp.einsum('bhcn,bnhd->bchd', p[..., n_win:], v_ctx))

    out = lax.map(one_row, (jnp.arange(rows, dtype=jnp.int32), q_rows))
    return jnp.moveaxis(out, 0, 1).reshape(b, s, h * d)


def mixer_cd(hx, hc, w_in, lb, c_gn, d_qn, d_kn, d_rpb, w_out, with_ctx):
    b, s, _ = hx.shape
    f32 = jnp.float32
    px = jnp.split(hx @ w_in, CD_SPLITS, axis=-1)
    pc = jnp.split(hc @ w_in, CD_SPLITS, axis=-1)

    def heads_first(t, dh):
        return jnp.moveaxis(t.reshape(b, t.shape[1], -1, dh), 2, 1)

    def hgrn_inputs(p, direction):
        z = p[1 + direction].astype(f32)
        lbd = lb[direction]
        logf = jnp.logaddexp(jnp.log(lbd), jnp.log1p(-lbd) + jax.nn.log_sigmoid(z))
        key_in = -jnp.expm1(logf)
        t = (heads_first(jax.nn.silu(p[0].astype(f32)), C_DK), heads_first(key_in, C_DK),
             heads_first(p[3].astype(f32), C_DV), heads_first(logf, C_DK))
        if direction == 1:
            t = tuple(jnp.flip(u, axis=2) for u in t)
        return t

    def hgrn_out(o, p):
        o = jnp.moveaxis(o, 1, 2)
        n = o.shape[1]
        gate = p[4].reshape(b, n, C_HEADS, C_DV)
        return (rms_norm(o, c_gn).astype(gate.dtype) * jax.nn.silu(gate)).reshape(b, n, C_VW)

    s0 = jnp.zeros((b, C_HEADS, C_DK, C_DV), f32)
    qf, kf, vf, gf = hgrn_inputs(pc, 0)
    s_cf, o_cf = hgrn_chunk_scan(qf, kf, vf, gf, s0, with_ctx)
    qf, kf, vf, gf = hgrn_inputs(px, 0)
    _, o_xf = hgrn_chunk_scan(qf, kf, vf, gf, s_cf, True)
    qb_, kb_, vb_, gb_ = hgrn_inputs(pc, 1)
    s_cb, o_cb = hgrn_chunk_scan(qb_, kb_, vb_, gb_, s0, with_ctx)
    qb_, kb_, vb_, gb_ = hgrn_inputs(px, 1)
    _, o_xb = hgrn_chunk_scan(qb_, kb_, vb_, gb_, s_cb, True)
    o_lat = o_xf + jnp.flip(o_xb, axis=2)

    def d_qkv(p):
        n = p[5].shape[1]
        return (rms_norm(p[5].reshape(b, n, D_HEADS, HEAD_DIM), d_qn),
                rms_norm(p[6].reshape(b, n, D_HEADS, HEAD_DIM), d_kn),
                p[7].reshape(b, n, D_HEADS, HEAD_DIM))

    dq, dk, dv = d_qkv(px)
    dq_c, dk_c, dv_c = d_qkv(pc)
    y_x = jnp.concatenate([hgrn_out(o_lat, px), neighbourhood_attention(dq, dk, dv, dk_c, dv_c, d_rpb)], axis=-1) @ w_out
    if not with_ctx:
        return y_x, None
    o_ctx = o_cf + jnp.flip(o_cb, axis=2)
    y_c = jnp.concatenate([hgrn_out(o_ctx, pc), blocked_attention(dq_c[:, :, :, None, :], dk_c, dv_c, HEAD_DIM ** -0.5)], axis=-1) @ w_out
    return y_x, y_c


def setup_inputs(seed: int = 0) -> dict:
    key = jax.random.key(seed)
    ks = jax.random.split(key, 26)

    def nrm(k, shape, scale):
        return jax.random.normal(k, shape, jnp.float32) * scale

    def gain(k, shape):
        return 1.0 + 0.05 * jax.random.normal(k, shape, jnp.float32)

    return {
        'x': nrm(ks[0], (BATCH, SEQ, D_MODEL), 1.0),
        'c': nrm(ks[1], (BATCH, D_MODEL), 1.0),
        'ctx': nrm(ks[2], (BATCH, CTX_LEN, D_MODEL), 1.0),
        'c_ctx': nrm(ks[3], (D_MODEL,), 1.0),
        'norm_g': gain(ks[4], (DEPTH, 3, D_MODEL)),
        'ada_w': nrm(ks[5], (DEPTH, D_MODEL, N_MOD * D_MODEL), ADA_STD),
        'ada_b': nrm(ks[6], (DEPTH, N_MOD * D_MODEL), 0.02),
        'ffn_w1': nrm(ks[7], (DEPTH, 2, D_MODEL, D_FF), D_MODEL ** -0.5),
        'ffn_w3': nrm(ks[8], (DEPTH, 2, D_MODEL, D_FF), D_MODEL ** -0.5),
        'ffn_w2': nrm(ks[9], (DEPTH, 2, D_FF, D_MODEL), D_FF ** -0.5),
        'ab_w_in': nrm(ks[10], (N_EVEN, D_MODEL, AB_IN), D_MODEL ** -0.5),
        'ab_a_qn': gain(ks[11], (N_EVEN, HEAD_DIM)),
        'ab_a_kn': gain(ks[12], (N_EVEN, HEAD_DIM)),
        'ab_b_qn': gain(ks[13], (N_EVEN, B_QK)),
        'ab_b_kn': gain(ks[14], (N_EVEN, B_QK)),
        'ab_b_kvn': gain(ks[15], (N_EVEN, B_KV_RANK)),
        'ab_b_wukv': nrm(ks[16], (N_EVEN, B_KV_RANK, B_HEADS * (B_NOPE + B_VDIM)), B_KV_RANK ** -0.5),
        'ab_w_out': nrm(ks[17], (N_EVEN, AB_OUT, D_MODEL), AB_OUT ** -0.5),
        'cd_w_in': nrm(ks[18], (N_ODD, D_MODEL, CD_IN), D_MODEL ** -0.5),
        'hgrn_lb': nrm(ks[19], (DEPTH, 2, C_W), 0.5),
        'cd_c_gn': gain(ks[20], (N_ODD, C_DV)),
        'cd_d_qn': gain(ks[21], (N_ODD, HEAD_DIM)),
        'cd_d_kn': gain(ks[22], (N_ODD, HEAD_DIM)),
        'cd_d_rpb': nrm(ks[23], (N_ODD, D_HEADS, 2 * D_WIN_H - 1, 2 * D_WIN_W - 1), 0.5),
        'cd_w_out': nrm(ks[24], (N_ODD, CD_OUT, D_MODEL), CD_OUT ** -0.5),
    }


def reference(x, c, ctx, c_ctx, norm_g, ada_w, ada_b, ffn_w1, ffn_w3, ffn_w2, ab_w_in, ab_a_qn, ab_a_kn, ab_b_qn, ab_b_kn, ab_b_kvn, ab_b_wukv, ab_w_out, cd_w_in, hgrn_lb, cd_c_gn, cd_d_qn, cd_d_kn, cd_d_rpb, cd_w_out):
    lb_all = jnp.cumsum(jax.nn.softmax(hgrn_lb.astype(jnp.float32), axis=0), axis=0)
    lb_all = lb_all - lb_all[:1]
    for l in range(DEPTH):
        last = l == DEPTH - 1
        i = l // 2
        mod_x = jnp.split(jax.nn.silu(c) @ ada_w[l] + ada_b[l], N_MOD, axis=-1)
        mod_c = jnp.split(jax.nn.silu(c_ctx)[None, :] @ ada_w[l] + ada_b[l], N_MOD, axis=-1)
        x = half_ffn(x, norm_g[l, 0], mod_x[0], mod_x[1], mod_x[2], ffn_w1[l, 0], ffn_w3[l, 0], ffn_w2[l, 0])
        ctx = half_ffn(ctx, norm_g[l, 0], mod_c[0], mod_c[1], mod_c[2], ffn_w1[l, 0], ffn_w3[l, 0], ffn_w2[l, 0])
        hx = modulate(rms_norm(x, norm_g[l, 1]), mod_x[3], mod_x[4])
        hc = modulate(rms_norm(ctx, norm_g[l, 1]), mod_c[3], mod_c[4])
        if l % 2 == 0:
            y_x, y_c = mixer_ab(hx, hc, ab_w_in[i], ab_a_qn[i], ab_a_kn[i], ab_b_qn[i], ab_b_kn[i], ab_b_kvn[i], ab_b_wukv[i], ab_w_out[i], not last)
        else:
            y_x, y_c = mixer_cd(hx, hc, cd_w_in[i], lb_all[l], cd_c_gn[i], cd_d_qn[i], cd_d_kn[i], cd_d_rpb[i], cd_w_out[i], not last)
        x = x + mod_x[5][:, None, :] * y_x
        x = half_ffn(x, norm_g[l, 2], mod_x[6], mod_x[7], mod_x[8], ffn_w1[l, 1], ffn_w3[l, 1], ffn_w2[l, 1])
        if not last:
            ctx = ctx + mod_c[5][:, None, :] * y_c
            ctx = half_ffn(ctx, norm_g[l, 2], mod_c[6], mod_c[7], mod_c[8], ffn_w1[l, 1], ffn_w3[l, 1], ffn_w2[l, 1])
    return x
```

```python
import functools
import math

import numpy as np
import jax
import jax.numpy as jnp
from jax import lax
from jax.experimental import pallas as pl
from jax.experimental.pallas import tpu as pltpu

F32 = jnp.float32
BF16 = jnp.bfloat16

GRID_W = 64
HEAD_DIM = 64
EPS = 1e-6
ROPE_THETA = 10000.0
N_MOD = 9
A_HEADS, A_KV_HEADS = 8, 2
B_HEADS, B_NOPE, B_ROPE, B_VDIM, B_KV_RANK = 8, 64, 32, 64, 256
B_QK = B_NOPE + B_ROPE
C_HEADS, C_DK, C_DV, C_CHUNK = 8, 64, 64, 64
D_HEADS, D_WIN_H, D_WIN_W = 8, 8, 16

LANES = 128
TM = 256
VMEM_LIMIT = 56 * 1024 * 1024
NEG = -1e30


def _cparams(sem):
    return pltpu.CompilerParams(dimension_semantics=sem, vmem_limit_bytes=VMEM_LIMIT)


def _const_spec(shape):
    nd = len(shape)
    return pl.BlockSpec(shape, lambda *_: (0,) * nd, pipeline_mode=pl.Buffered(1))


def _lane(shape):
    return lax.broadcasted_iota(jnp.int32, shape, len(shape) - 1)


def _rms_mod(x, g, shift, scale):
    y = x * lax.rsqrt(jnp.mean(x * x, axis=-1, keepdims=True) + EPS) * g
    return y * (1.0 + scale) + shift


def _half_norm(x, gain, n):
    left = _lane(x.shape) < n
    x2 = x * x
    sl = jnp.sum(jnp.where(left, x2, 0.0), axis=-1, keepdims=True)
    sr = jnp.sum(jnp.where(left, 0.0, x2), axis=-1, keepdims=True)
    r = jnp.where(left, lax.rsqrt(sl / n + EPS), lax.rsqrt(sr / n + EPS))
    return x * r * gain


def _slab_norm(x, gain, n):
    ss = jnp.sum(x * x, axis=-1, keepdims=True)
    return x * lax.rsqrt(ss / n + EPS) * gain


def _rope(x, cos, sin_signed, half):
    first = (_lane(x.shape) % (2 * half)) < half
    w = x.shape[-1]
    partner = jnp.where(first, pltpu.roll(x, w - half, 1), pltpu.roll(x, half, 1))
    return x * cos + partner * sin_signed


def _dot(a, b):
    return jnp.dot(a, b, preferred_element_type=F32)


def _dot_nt(a, b):
    return lax.dot_general(a, b, (((1,), (1,)), ((), ())), preferred_element_type=F32)


def _dot_tn(a, b):
    return lax.dot_general(a, b, (((0,), (0,)), ((), ())), preferred_element_type=F32)


def _ada_kernel(c_ref, w_ref, b_ref, o_ref):
    c = c_ref[...]
    a = c * jax.nn.sigmoid(c)
    o_ref[0] = jnp.dot(a, w_ref[0], preferred_element_type=F32, precision=lax.Precision.HIGHEST) + b_ref[0]


def _ada(cc, ada_w, ada_b):
    depth, d, n = ada_w.shape
    bn = n // 8
    return pl.pallas_call(
        _ada_kernel,
        grid=(depth, n // bn),
        in_specs=[pl.BlockSpec((8, d), lambda l, j: (0, 0)),
                  pl.BlockSpec((1, d, bn), lambda l, j: (l, 0, j)),
                  pl.BlockSpec((1, 1, bn), lambda l, j: (l, 0, j))],
        out_specs=pl.BlockSpec((1, 8, bn), lambda l, j: (l, 0, j)),
        out_shape=jax.ShapeDtypeStruct((depth, 8, n), F32),
        name="ada_mod",
        compiler_params=_cparams(("arbitrary", "arbitrary")),
    )(cc, ada_w, ada_b.reshape(depth, 1, n))


def _ffn_body(x, m, g, w1_ref, w3_ref, w2_ref, base):
    h = _rms_mod(x, g, m[base:base + 1], m[base + 1:base + 2]).astype(BF16)
    a = _dot(h, w1_ref[...])
    b = _dot(h, w3_ref[...])
    u = (a * jax.nn.sigmoid(a) * b).astype(BF16)
    return x + 0.5 * m[base + 2:base + 3] * _dot(u, w2_ref[...])


def _ffn_kernel(t_ref, m_ref, g_ref, w1_ref, w3_ref, w2_ref, o_ref, *, base):
    o_ref[...] = _ffn_body(t_ref[...], m_ref[0], g_ref[...], w1_ref, w3_ref, w2_ref, base)


def _ffn(t, mod, g, w1, w3, w2, base, latent_only):
    n, d = t.shape
    f = w1.shape[1]
    mod_idx = (lambda i: (1, 0, 0)) if latent_only else (lambda i: (jnp.minimum(i, 1), 0, 0))
    return pl.pallas_call(
        functools.partial(_ffn_kernel, base=base),
        grid=(n // TM,),
        in_specs=[pl.BlockSpec((TM, d), lambda i: (i, 0)),
                  pl.BlockSpec((1, N_MOD, d), mod_idx),
                  _const_spec((1, d)), _const_spec((d, f)), _const_spec((d, f)), _const_spec((f, d))],
        out_specs=pl.BlockSpec((TM, d), lambda i: (i, 0)),
        out_shape=jax.ShapeDtypeStruct((n, d), F32),
        name="half_ffn",
        compiler_params=_cparams(("arbitrary",)),
    )(t, mod, g, w1, w3, w2)


def _proj_ab_kernel(t_ref, m_ref, g_ref, wp_ref, wukv_ref, gq_a_ref, gk_a_ref, gq_b_ref, gk_b_ref, gkv_ref,
                    cos_a_ref, sin_a_ref, cos_b_ref, sin_b_ref,
                    qt_ref, k_ref, vta_ref, vtb_ref):
    m = m_ref[0]
    h = _rms_mod(t_ref[...], g_ref[...], m[3:4], m[4:5]).astype(BF16)
    p = _dot(h, wp_ref[...])
    cos_a, sin_a = cos_a_ref[...], sin_a_ref[...]
    cos_b, sin_b = cos_b_ref[...], sin_b_ref[...]
    o_ak, o_av, o_bq, o_kv, o_kr = 8 * LANES, 9 * LANES, 10 * LANES, 18 * LANES, 20 * LANES

    for hd in range(A_HEADS):
        x = _slab_norm(p[:, hd * LANES:(hd + 1) * LANES], gq_a_ref[:, hd * LANES:(hd + 1) * LANES], HEAD_DIM)
        x = _rope(x, cos_a, sin_a, HEAD_DIM // 4) * (HEAD_DIM ** -0.5)
        qt_ref[0, hd * LANES:(hd + 1) * LANES, :] = x.T.astype(BF16)
    x = _half_norm(p[:, o_ak:o_ak + LANES], gk_a_ref[...], HEAD_DIM)
    k_ref[:, B_HEADS * LANES:(B_HEADS + 1) * LANES] = _rope(x, cos_a, sin_a, HEAD_DIM // 4).astype(BF16)
    vta_ref[0] = p[:, o_av:o_av + LANES].T.astype(BF16)
    for hd in range(B_HEADS):
        x = _slab_norm(p[:, o_bq + hd * LANES:o_bq + (hd + 1) * LANES], gq_b_ref[...], B_QK)
        x = _rope(x, cos_b, sin_b, B_ROPE // 4) * (B_QK ** -0.5)
        qt_ref[0, (A_HEADS + hd) * LANES:(A_HEADS + hd + 1) * LANES, :] = x.T.astype(BF16)
    c = p[:, o_kv:o_kv + B_KV_RANK]
    cn = (c * lax.rsqrt(jnp.mean(c * c, axis=-1, keepdims=True) + EPS) * gkv_ref[...]).astype(BF16)
    kv = _dot(cn, wukv_ref[...])
    kr = p[:, o_kr:o_kr + LANES]
    for hd in range(B_HEADS):
        x = _slab_norm(kv[:, hd * LANES:(hd + 1) * LANES] + kr, gk_b_ref[...], B_QK)
        k_ref[:, hd * LANES:(hd + 1) * LANES] = _rope(x, cos_b, sin_b, B_ROPE // 4).astype(BF16)
    for pr in range(B_HEADS // 2):
        o = (B_HEADS + pr) * LANES
        vtb_ref[0, pr * LANES:(pr + 1) * LANES, :] = kv[:, o:o + LANES].T.astype(BF16)


def _proj_ab(t, mod, g, wp, wukv, gq_a, gk_a, gq_b, gk_b, gkv, cos_a, sin_a, cos_b, sin_b):
    n, d = t.shape
    nt = n // TM
    nq = (A_HEADS + B_HEADS) * LANES
    tile = lambda w: pl.BlockSpec((TM, w), lambda i: (i, 0))
    return pl.pallas_call(
        _proj_ab_kernel,
        grid=(nt,),
        in_specs=[tile(d),
                  pl.BlockSpec((1, N_MOD, d), lambda i: (jnp.minimum(i, 1), 0, 0)),
                  _const_spec((1, d)), _const_spec(wp.shape), _const_spec(wukv.shape),
                  _const_spec(gq_a.shape), _const_spec(gk_a.shape), _const_spec(gq_b.shape),
                  _const_spec(gk_b.shape), _const_spec(gkv.shape),
                  tile(LANES), tile(LANES), tile(LANES), tile(LANES)],
        out_specs=[pl.BlockSpec((1, nq, TM), lambda i: (i, 0, 0)),
                   pl.BlockSpec((TM, (B_HEADS + 1) * LANES), lambda i: (i, 0)),
                   pl.BlockSpec((1, LANES, TM), lambda i: (i, 0, 0)),
                   pl.BlockSpec((1, B_HEADS * B_VDIM, TM), lambda i: (i, 0, 0))],
        out_shape=[jax.ShapeDtypeStruct((nt, nq, TM), BF16),
                   jax.ShapeDtypeStruct((n, (B_HEADS + 1) * LANES), BF16),
                   jax.ShapeDtypeStruct((nt, LANES, TM), BF16),
                   jax.ShapeDtypeStruct((nt, B_HEADS * B_VDIM, TM), BF16)],
        name="proj_ab",
        compiler_params=_cparams(("arbitrary",)),
    )(t, mod, g, wp, wukv, gq_a, gk_a, gq_b, gk_b, gkv, cos_a, sin_a, cos_b, sin_b)


def _attn_kernel(qt_ref, k_ref, vt_ref, o_ref, *, n_heads, k_slab, v_rows, n_tiles):
    nk = jnp.where(pl.program_id(1) == 0, 1, n_tiles)
    for pr in range(n_heads // 2):
        outs = []
        for e in range(2):
            hl = 2 * pr + e
            qt = qt_ref[0, hl * LANES:(hl + 1) * LANES, :]
            ks, vr = k_slab[hl], v_rows[hl]

            def body(kt, carry, qt=qt, ks=ks, vr=vr):
                m, l, acc = carry
                off = pl.multiple_of(kt * TM, TM)
                k = k_ref[pl.ds(off, TM), ks * LANES:(ks + 1) * LANES]
                vt = vt_ref[kt, vr * HEAD_DIM:(vr + 1) * HEAD_DIM, :]
                s = _dot(k, qt)
                m_new = jnp.maximum(m, jnp.max(s, axis=0, keepdims=True))
                alpha = jnp.exp(m - m_new)
                pm = jnp.exp(s - m_new)
                l = alpha * l + jnp.sum(pm, axis=0, keepdims=True)
                acc = alpha * acc + _dot(vt, pm.astype(BF16))
                return m_new, l, acc

            init = (jnp.full((1, TM), -jnp.inf, F32), jnp.zeros((1, TM), F32), jnp.zeros((HEAD_DIM, TM), F32))
            m, l, acc = lax.fori_loop(0, nk, body, init)
            outs.append(acc / l)
        o_ref[:, pr * LANES:(pr + 1) * LANES] = jnp.concatenate(outs, axis=0).T.astype(BF16)


def _attn_a(qt, k, vta):
    nt = qt.shape[0]
    n = nt * TM
    gh = A_HEADS // A_KV_HEADS
    kern = functools.partial(_attn_kernel, n_heads=gh, k_slab=(0,) * gh, v_rows=(0,) * gh, n_tiles=nt)
    return pl.pallas_call(
        kern,
        grid=(A_KV_HEADS, nt),
        in_specs=[pl.BlockSpec((1, gh * LANES, TM), lambda g, i: (i, g, 0)),
                  pl.BlockSpec((n, LANES), lambda g, i: (0, B_HEADS)),
                  pl.BlockSpec((nt, HEAD_DIM, TM), lambda g, i: (0, g, 0))],
        out_specs=pl.BlockSpec((TM, gh * HEAD_DIM), lambda g, i: (i, g)),
        out_shape=jax.ShapeDtypeStruct((n, A_HEADS * HEAD_DIM), BF16),
        name="attn_gqa",
        compiler_params=_cparams(("arbitrary", "arbitrary")),
    )(qt, k, vta)


def _attn_b(qt, k, vtb):
    nt = qt.shape[0]
    n = nt * TM
    kern = functools.partial(_attn_kernel, n_heads=2, k_slab=(0, 1), v_rows=(0, 1), n_tiles=nt)
    return pl.pallas_call(
        kern,
        grid=(B_HEADS // 2, nt),
        in_specs=[pl.BlockSpec((1, 2 * LANES, TM), lambda j, i: (i, A_HEADS // 2 + j, 0)),
                  pl.BlockSpec((n, 2 * LANES), lambda j, i: (0, j)),
                  pl.BlockSpec((nt, 2 * B_VDIM, TM), lambda j, i: (0, j, 0))],
        out_specs=pl.BlockSpec((TM, 2 * B_VDIM), lambda j, i: (i, j)),
        out_shape=jax.ShapeDtypeStruct((n, B_HEADS * B_VDIM), BF16),
        name="attn_mla",
        compiler_params=_cparams(("arbitrary", "arbitrary")),
    )(qt, k, vtb)


def _out_ab_kernel(t_ref, m_ref, oa_ref, ob_ref, w_ref, o_ref):
    half = oa_ref.shape[1]
    y = _dot(oa_ref[...], w_ref[:half, :]) + _dot(ob_ref[...], w_ref[half:, :])
    o_ref[...] = t_ref[...] + m_ref[0][5:6] * y


def _out_ab(t, mod, oa, ob, w):
    n, d = t.shape
    tile = lambda wd: pl.BlockSpec((TM, wd), lambda i: (i, 0))
    return pl.pallas_call(
        _out_ab_kernel,
        grid=(n // TM,),
        in_specs=[tile(d), pl.BlockSpec((1, N_MOD, d), lambda i: (jnp.minimum(i, 1), 0, 0)),
                  tile(oa.shape[1]), tile(ob.shape[1]), _const_spec(w.shape)],
        out_specs=tile(d),
        out_shape=jax.ShapeDtypeStruct((n, d), F32),
        name="out_ab",
        compiler_params=_cparams(("arbitrary",)),
    )(t, mod, oa, ob, w)


def _proj_cd_kernel(t_ref, m_ref, g_ref, w_ref, lbp_ref, gq_ref, gk_ref,
                    hq_ref, gf_ref, gb_ref, hv_ref, hg_ref, dq_ref, dk_ref, dv_ref, *, layer):
    m = m_ref[0]
    h = _rms_mod(t_ref[...], g_ref[...], m[3:4], m[4:5]).astype(BF16)
    p = _dot(h, w_ref[...])
    w = C_HEADS * C_DK
    x = p[:, 0:w]
    hq_ref[...] = x * jax.nn.sigmoid(x)
    lbp = lbp_ref[...]
    e = jnp.exp(lbp - jnp.max(lbp, axis=0, keepdims=True))
    sm = e / jnp.sum(e, axis=0, keepdims=True)
    lb = jnp.sum(sm[1:layer + 1], axis=0)
    for dr, ref in ((0, gf_ref), (1, gb_ref)):
        z = p[:, (1 + dr) * w:(2 + dr) * w]
        lbd = lb[dr:dr + 1]
        ref[...] = jnp.log(lbd + (1.0 - lbd) * jax.nn.sigmoid(z))
    hv_ref[...] = p[:, 3 * w:4 * w].astype(BF16)
    hg_ref[...] = p[:, 4 * w:5 * w]
    for s in range(w // LANES):
        sl = slice(s * LANES, (s + 1) * LANES)
        q = _half_norm(p[:, 5 * w + s * LANES:5 * w + (s + 1) * LANES], gq_ref[...], HEAD_DIM)
        dq_ref[:, sl] = (q * (HEAD_DIM ** -0.5)).astype(BF16)
        dk_ref[:, sl] = _half_norm(p[:, 6 * w + s * LANES:6 * w + (s + 1) * LANES], gk_ref[...], HEAD_DIM).astype(BF16)
    dv_ref[...] = p[:, 7 * w:8 * w].astype(BF16)


def _proj_cd(t, mod, g, w_in, hgrn_lb, gq, gk, layer):
    n, d = t.shape
    w = C_HEADS * C_DK
    tile = lambda wd: pl.BlockSpec((TM, wd), lambda i: (i, 0))
    shp = lambda dt: jax.ShapeDtypeStruct((n, w), dt)
    return pl.pallas_call(
        functools.partial(_proj_cd_kernel, layer=layer),
        grid=(n // TM,),
        in_specs=[tile(d), pl.BlockSpec((1, N_MOD, d), lambda i: (jnp.minimum(i, 1), 0, 0)),
                  _const_spec((1, d)), _const_spec(w_in.shape), _const_spec(hgrn_lb.shape),
                  _const_spec(gq.shape), _const_spec(gk.shape)],
        out_specs=[tile(w)] * 8,
        out_shape=[shp(F32), shp(F32), shp(F32), shp(BF16), shp(F32), shp(BF16), shp(BF16), shp(BF16)],
        name="proj_cd",
        compiler_params=_cparams(("arbitrary",)),
    )(t, mod, g, w_in, hgrn_lb, gq, gk)


N_LEVELS = int(math.log2(C_CHUNK))


def _hgrn_tables():
    c = C_CHUNK
    t = np.arange(c)[:, None]
    r = np.arange(c)[None, :]
    blocks = [(r <= t), (r > t)]
    for lv in range(N_LEVELS):
        h = c >> (lv + 1)
        b = (t // (2 * h)) * (2 * h) + h - 1
        right = (t % (2 * h)) >= h
        blocks.append(np.where(right, (r > b) & (r <= t), (r > t) & (r <= b)))
    return np.concatenate(blocks, axis=0).astype(np.float32)


def _hgrn_chunk(q, g, v, st_ref, w_ref, d, reverse):
    c = C_CHUNK
    w = w_ref[...]
    ghi = g.astype(BF16)
    glo = (g - ghi.astype(F32)).astype(BF16)
    xall = _dot(w, ghi) + _dot(w, glo)
    kk = 1.0 - jnp.exp(g)
    row = lax.broadcasted_iota(jnp.int32, (c, q.shape[1]), 0)
    if reverse:
        row = (c - 1) - row
    gi = xall[0:c]
    glast = xall[(c - 1):c] if not reverse else xall[0:1]
    qs = (q * jnp.exp(gi)).astype(BF16)
    ks = (kk * jnp.exp(xall[c:2 * c])).astype(BF16)
    dec = jnp.exp(glast)
    eqs, eks = [], []
    for lv in range(N_LEVELS):
        h = c >> (lv + 1)
        e = jnp.exp(xall[(2 + lv) * c:(3 + lv) * c])
        right = (row & h) != 0
        eqs.append((jnp.where(right, e, 0.0) * q).astype(BF16))
        eks.append((jnp.where(right, 0.0, e) * kk).astype(BF16))
    eqs.append(q.astype(BF16))
    eks.append(kk.astype(BF16))
    ti = lax.broadcasted_iota(jnp.int32, (2 * c, c), 0) % c
    si = lax.broadcasted_iota(jnp.int32, (2 * c, c), 1)
    lane = _lane((c, LANES))
    left = lane < C_DK
    blockdiag = (lax.broadcasted_iota(jnp.int32, (LANES, LANES), 0) < C_DV) == (_lane((LANES, LANES)) < C_DK)
    outs = []
    for pr in range(C_HEADS // 2):
        sl = slice(pr * LANES, (pr + 1) * LANES)
        a2 = jnp.zeros((2 * c, c), F32)
        for lv in range(N_LEVELS + 1):
            eq = eqs[lv][:, sl]
            eq2 = jnp.concatenate([jnp.where(left, eq, 0), jnp.where(left, 0, eq)], axis=0)
            a = _dot_nt(eq2, eks[lv][:, sl])
            if lv < N_LEVELS:
                sh = N_LEVELS - lv
                mask = (ti >> sh) == (si >> sh)
            else:
                mask = ti == si
            a2 = a2 + jnp.where(mask, a, 0.0)
        r2 = _dot(a2.astype(BF16), v[:, sl])
        st = st_ref[d, pr]
        o = jnp.where(left, r2[:c], r2[c:]) + _dot_nt(qs[:, sl], st.astype(BF16))
        outs.append(o)
        upd = _dot_tn(v[:, sl], ks[:, sl])
        st_ref[d, pr] = st * dec[:, sl] + jnp.where(blockdiag, upd, 0.0)
    return jnp.concatenate(outs, axis=1)


def _hgrn_kernel(qf_ref, gf_ref, vf_ref, qb_ref, gb_ref, vb_ref, wf_ref, wb_ref, of_ref, ob_ref, st_ref):
    @pl.when(pl.program_id(0) == 0)
    def _():
        st_ref[...] = jnp.zeros(st_ref.shape, F32)

    c = C_CHUNK
    for j in range(TM // c):
        sl = slice(j * c, (j + 1) * c)
        of_ref[sl, :] = _hgrn_chunk(qf_ref[sl, :], gf_ref[sl, :], vf_ref[sl, :], st_ref, wf_ref, 0, False)
        rl = slice(TM - (j + 1) * c, TM - j * c)
        ob_ref[rl, :] = _hgrn_chunk(qb_ref[rl, :], gb_ref[rl, :], vb_ref[rl, :], st_ref, wb_ref, 1, True)


def _hgrn(hq, gf, gb, hv):
    n, w = hq.shape
    nt = n // TM
    wf = _hgrn_tables()
    wb = wf.reshape(-1, C_CHUNK, C_CHUNK)[:, ::-1, ::-1].reshape(wf.shape)
    fwd = pl.BlockSpec((TM, w), lambda i: (i, 0))
    bwd = pl.BlockSpec((TM, w), lambda i: (jnp.where(i == 0, 0, nt - i), 0))
    return pl.pallas_call(
        _hgrn_kernel,
        grid=(nt,),
        in_specs=[fwd, fwd, fwd, bwd, bwd, bwd, _const_spec(wf.shape), _const_spec(wb.shape)],
        out_specs=[fwd, bwd],
        out_shape=[jax.ShapeDtypeStruct((n, w), F32)] * 2,
        scratch_shapes=[pltpu.VMEM((2, C_HEADS // 2, LANES, LANES), F32)],
        name="hgrn2",
        compiler_params=_cparams(("arbitrary",)),
    )(hq, gf, hv, hq, gb, hv, jnp.asarray(wf, BF16), jnp.asarray(wb, BF16))


NAT_ROWS = TM // GRID_W


def _natten_kernel(q_ref, k_ref, v_ref, bias_ref, o_ref, *, rows):
    i = pl.program_id(1)
    kc = k_ref[0:TM, :]
    vc = v_ref[0:TM, :]
    left_q = _lane((GRID_W, LANES)) < HEAD_DIM
    for j in range(NAT_ROWS):
        r = i * NAT_ROWS + j
        rs = jnp.clip(r - D_WIN_H // 2, 0, rows - D_WIN_H)
        a0 = rs - r + (D_WIN_H - 1)
        off = pl.multiple_of(TM + rs * GRID_W, GRID_W)
        q = q_ref[j * GRID_W:(j + 1) * GRID_W, :]
        q2 = jnp.concatenate([jnp.where(left_q, q, 0), jnp.where(left_q, 0, q)], axis=0)
        kw = k_ref[pl.ds(off, D_WIN_H * GRID_W), :]
        vw = v_ref[pl.ds(off, D_WIN_H * GRID_W), :]
        sw = _dot_nt(q2, kw) + bias_ref[0, a0]
        sc = _dot_nt(q2, kc)
        m = jnp.maximum(jnp.max(sw, axis=-1, keepdims=True), jnp.max(sc, axis=-1, keepdims=True))
        pw = jnp.exp(sw - m)
        pc = jnp.exp(sc - m)
        l = jnp.sum(pw, axis=-1, keepdims=True) + jnp.sum(pc, axis=-1, keepdims=True)
        o2 = (_dot(pw.astype(BF16), vw) + _dot(pc.astype(BF16), vc)) / l
        o_ref[j * GRID_W:(j + 1) * GRID_W, :] = jnp.where(left_q, o2[:GRID_W], o2[GRID_W:]).astype(BF16)


def _natten(dq, dk, dv, bias):
    n, w = dq.shape
    nt = n // TM - 1
    rows = nt * NAT_ROWS
    return pl.pallas_call(
        functools.partial(_natten_kernel, rows=rows),
        grid=(w // LANES, nt),
        in_specs=[pl.BlockSpec((TM, LANES), lambda p, i: (i + 1, p)),
                  pl.BlockSpec((n, LANES), lambda p, i: (0, p)),
                  pl.BlockSpec((n, LANES), lambda p, i: (0, p)),
                  pl.BlockSpec((1, D_WIN_H) + bias.shape[2:], lambda p, i: (p, 0, 0, 0))],
        out_specs=pl.BlockSpec((TM, LANES), lambda p, i: (i, p)),
        out_shape=jax.ShapeDtypeStruct((nt * TM, w), BF16),
        name="natten",
        compiler_params=_cparams(("arbitrary", "arbitrary")),
    )(dq, dk, dv, bias)


def _natten_bias(rpb):
    cols = np.arange(GRID_W)
    start = np.clip(cols - D_WIN_W // 2, 0, GRID_W - D_WIN_W)
    kc = cols[None, :]
    inside = (kc >= start[:, None]) & (kc < start[:, None] + D_WIN_W)
    rel = np.clip(kc - cols[:, None] + (D_WIN_W - 1), 0, 2 * D_WIN_W - 2)
    tab = jnp.where(inside[None, None], rpb[:, :, rel], NEG)
    a = np.arange(D_WIN_H)[:, None] + np.arange(D_WIN_H)[None, :]
    t = tab[:, a]
    t = jnp.transpose(t, (0, 1, 3, 2, 4)).reshape(D_HEADS // 2, 2, D_WIN_H, GRID_W, D_WIN_H * GRID_W)
    return jnp.transpose(t, (0, 2, 1, 3, 4)).reshape(D_HEADS // 2, D_WIN_H, 2 * GRID_W, D_WIN_H * GRID_W)


def _out_cd_kernel(t_ref, m_ref, of_ref, ob_ref, hg_ref, nat_ref, gn_ref, w_ref, o_ref):
    half = nat_ref.shape[1]
    o = of_ref[...] + ob_ref[...]
    gate = hg_ref[...]
    gate = gate * jax.nn.sigmoid(gate)
    parts = []
    for s in range(half // LANES):
        sl = slice(s * LANES, (s + 1) * LANES)
        parts.append((_half_norm(o[:, sl], gn_ref[...], C_DV) * gate[:, sl]).astype(BF16))
    y = _dot(jnp.concatenate(parts, axis=1), w_ref[:half, :]) + _dot(nat_ref[...], w_ref[half:, :])
    o_ref[...] = t_ref[...] + m_ref[0][5:6] * y


def _out_cd(t, mod, o_f, o_b, hg, nat, gn, w):
    n, d = t.shape
    nt = n // TM - 1
    wd = nat.shape[1]
    lat = lambda x: pl.BlockSpec((TM, x), lambda i: (i + 1, 0))
    return pl.pallas_call(
        _out_cd_kernel,
        grid=(nt,),
        in_specs=[lat(d), pl.BlockSpec((1, N_MOD, d), lambda i: (1, 0, 0)),
                  lat(wd), lat(wd), lat(wd), pl.BlockSpec((TM, wd), lambda i: (i, 0)),
                  _const_spec(gn.shape), _const_spec(w.shape)],
        out_specs=pl.BlockSpec((TM, d), lambda i: (i, 0)),
        out_shape=jax.ShapeDtypeStruct((nt * TM, d), F32),
        name="out_cd",
        compiler_params=_cparams(("arbitrary",)),
    )(t, mod, o_f, o_b, hg, nat, gn, w)


def _rope_tables(n_lat, dim, lane0):
    t = jnp.arange(n_lat, dtype=jnp.int32)

    def one(pos, dh):
        inv = ROPE_THETA ** (-jnp.arange(0, dh, 2, dtype=F32) / dh)
        ang = pos.astype(F32)[:, None] * inv[None, :]
        return jnp.concatenate([ang, ang], axis=-1)

    ang = jnp.concatenate([one(t // GRID_W, dim // 2), one(t % GRID_W, dim // 2)], axis=-1)
    cos, sin = jnp.cos(ang), jnp.sin(ang)
    q = dim // 4
    sign = jnp.where((jnp.arange(dim) % (2 * q)) < q, -1.0, 1.0)
    sin = sin * sign
    if dim == HEAD_DIM:
        cos, sin = jnp.concatenate([cos, cos], -1), jnp.concatenate([sin, sin], -1)
    else:
        pad = ((0, 0), (lane0, LANES - lane0 - dim))
        cos = jnp.pad(cos, pad, constant_values=1.0)
        sin = jnp.pad(sin, pad)
    cos = jnp.concatenate([jnp.ones((TM, LANES), F32), cos], axis=0)
    sin = jnp.concatenate([jnp.zeros((TM, LANES), F32), sin], axis=0)
    return cos, sin


def _pad_lanes(x, lo, width=LANES):
    return jnp.pad(x, [(0, 0)] * (x.ndim - 1) + [(lo, width - lo - x.shape[-1])])


def _ab_weights(w_in, a_qn, a_kn, b_qn, b_kn, b_kvn, wukv):
    d = w_in.shape[0]
    grp = A_HEADS // A_KV_HEADS
    o = 0
    aq = w_in[:, o:o + A_HEADS * HEAD_DIM].reshape(d, A_HEADS, HEAD_DIM); o += A_HEADS * HEAD_DIM
    ak = w_in[:, o:o + A_KV_HEADS * HEAD_DIM]; o += A_KV_HEADS * HEAD_DIM
    av = w_in[:, o:o + A_KV_HEADS * HEAD_DIM]; o += A_KV_HEADS * HEAD_DIM
    bq = w_in[:, o:o + B_HEADS * B_QK].reshape(d, B_HEADS, B_QK); o += B_HEADS * B_QK
    bkv = w_in[:, o:o + B_KV_RANK]; o += B_KV_RANK
    bkr = w_in[:, o:o + B_ROPE]
    aq_slabs = jnp.concatenate([_pad_lanes(aq[:, h], (h // grp) * HEAD_DIM) for h in range(A_HEADS)], axis=1)
    bq_slabs = _pad_lanes(bq, 0).reshape(d, B_HEADS * LANES)
    wp = jnp.concatenate([aq_slabs, ak, av, bq_slabs, bkv, _pad_lanes(bkr, B_NOPE)], axis=1).astype(BF16)
    gq_a = jnp.concatenate([_pad_lanes(a_qn[None], (h // grp) * HEAD_DIM) for h in range(A_HEADS)], axis=1)
    gk_a = jnp.concatenate([a_kn, a_kn])[None]
    gq_b = _pad_lanes(b_qn[None], 0)
    gk_b = _pad_lanes(b_kn[None], 0)
    kvw = wukv.reshape(B_KV_RANK, B_HEADS, B_NOPE + B_VDIM)
    wk = _pad_lanes(kvw[:, :, :B_NOPE], 0).reshape(B_KV_RANK, B_HEADS * LANES)
    wv = kvw[:, :, B_NOPE:].reshape(B_KV_RANK, B_HEADS * B_VDIM)
    wukv_p = jnp.concatenate([wk, wv], axis=1).astype(BF16)
    return wp, wukv_p, gq_a, gk_a, gq_b, gk_b, b_kvn[None]


def kernel(x, c, ctx, c_ctx, norm_g, ada_w, ada_b, ffn_w1, ffn_w3, ffn_w2, ab_w_in, ab_a_qn, ab_a_kn, ab_b_qn,
           ab_b_kn, ab_b_kvn, ab_b_wukv, ab_w_out, cd_w_in, hgrn_lb, cd_c_gn, cd_d_qn, cd_d_kn, cd_d_rpb, cd_w_out):
    batch, seq, d = x.shape
    depth = norm_g.shape[0]
    assert batch == 1 and ctx.shape[1] == TM and seq % TM == 0 and depth == 2

    t = jnp.concatenate([ctx[0], x[0]], axis=0)
    cc = jnp.zeros((8, d), F32).at[0].set(c_ctx).at[1].set(c[0])
    mods = _ada(cc, ada_w, ada_b)[:, :2].reshape(depth, 2, N_MOD, d)
    bf = lambda a: a.astype(BF16)

    def ffn(tt, l, j, latent_only=False):
        return _ffn(tt, mods[l], norm_g[l, 2 * j:2 * j + 1], bf(ffn_w1[l, j]), bf(ffn_w3[l, j]),
                    bf(ffn_w2[l, j]), 6 * j, latent_only)

    t = ffn(t, 0, 0)
    wp, wukv_p, gq_a, gk_a, gq_b, gk_b, gkv = _ab_weights(ab_w_in[0], ab_a_qn[0], ab_a_kn[0], ab_b_qn[0],
                                                         ab_b_kn[0], ab_b_kvn[0], ab_b_wukv[0])
    cos_a, sin_a = _rope_tables(seq, HEAD_DIM, 0)
    cos_b, sin_b = _rope_tables(seq, B_ROPE, B_NOPE)
    qt, k, vta, vtb = _proj_ab(t, mods[0], norm_g[0, 1:2], wp, wukv_p, gq_a, gk_a, gq_b, gk_b, gkv,
                               cos_a, sin_a, cos_b, sin_b)
    oa = _attn_a(qt, k, vta)
    ob = _attn_b(qt, k, vtb)
    t = _out_ab(t, mods[0], oa, ob, bf(ab_w_out[0]))
    t = ffn(t, 0, 1)

    t = ffn(t, 1, 0)
    gq = jnp.concatenate([cd_d_qn[0], cd_d_qn[0]])[None]
    gk = jnp.concatenate([cd_d_kn[0], cd_d_kn[0]])[None]
    gn = jnp.concatenate([cd_c_gn[0], cd_c_gn[0]])[None]
    hq, gf, gb, hv, hg, dq, dk, dv = _proj_cd(t, mods[1], norm_g[1, 1:2], bf(cd_w_in[0]), hgrn_lb, gq, gk, 1)
    o_f, o_b = _hgrn(hq, gf, gb, hv)
    nat = _natten(dq, dk, dv, _natten_bias(cd_d_rpb[0]))
    xl = _out_cd(t, mods[1], o_f, o_b, hg, nat, gn, bf(cd_w_out[0]))
    xl = ffn(xl, 1, 1, latent_only=True)
    return xl[None]
```

```python
import functools
import math

import numpy as np
import jax
import jax.numpy as jnp
from jax import lax
from jax.experimental import pallas as pl
from jax.experimental.pallas import tpu as pltpu

F32 = jnp.float32
BF16 = jnp.bfloat16

GRID_W = 64
HEAD_DIM = 64
EPS = 1e-6
ROPE_THETA = 10000.0
N_MOD = 9
A_HEADS, A_KV_HEADS = 8, 2
B_HEADS, B_NOPE, B_ROPE, B_VDIM, B_KV_RANK = 8, 64, 32, 64, 256
B_QK = B_NOPE + B_ROPE
C_HEADS, C_DK, C_DV, C_CHUNK = 8, 64, 64, 64
D_HEADS, D_WIN_H, D_WIN_W = 8, 8, 16

LANES = 128
TM = 256
VMEM_LIMIT = 56 * 1024 * 1024
NEG = -1e30
LOG2E = math.log2(math.e)


def _cparams(sem):
    return pltpu.CompilerParams(dimension_semantics=sem, vmem_limit_bytes=VMEM_LIMIT)


def _const_spec(shape):
    nd = len(shape)
    return pl.BlockSpec(shape, lambda *_: (0,) * nd, pipeline_mode=pl.Buffered(1))


def _lane(shape):
    return lax.broadcasted_iota(jnp.int32, shape, len(shape) - 1)


def _rms_mod(x, g, shift, scale):
    y = x * lax.rsqrt(jnp.mean(x * x, axis=-1, keepdims=True) + EPS) * g
    return y * (1.0 + scale) + shift


def _half_norm(x, gain, n):
    left = _lane(x.shape) < n
    x2 = x * x
    sl = jnp.sum(jnp.where(left, x2, 0.0), axis=-1, keepdims=True)
    sr = jnp.sum(jnp.where(left, 0.0, x2), axis=-1, keepdims=True)
    r = jnp.where(left, lax.rsqrt(sl / n + EPS), lax.rsqrt(sr / n + EPS))
    return x * r * gain


def _slab_norm(x, gain, n):
    ss = jnp.sum(x * x, axis=-1, keepdims=True)
    return x * lax.rsqrt(ss / n + EPS) * gain


def _rope(x, cos, sin_signed, half):
    first = (_lane(x.shape) % (2 * half)) < half
    w = x.shape[-1]
    partner = jnp.where(first, pltpu.roll(x, w - half, 1), pltpu.roll(x, half, 1))
    return x * cos + partner * sin_signed


def _dot(a, b):
    return jnp.dot(a, b, preferred_element_type=F32)


def _dot_nt(a, b):
    return lax.dot_general(a, b, (((1,), (1,)), ((), ())), preferred_element_type=F32)


def _dot_tn(a, b):
    return lax.dot_general(a, b, (((0,), (0,)), ((), ())), preferred_element_type=F32)


def _ada_kernel(c_ref, w_ref, b_ref, o_ref):
    c = c_ref[...]
    a = c * jax.nn.sigmoid(c)
    o_ref[0] = jnp.dot(a, w_ref[0], preferred_element_type=F32, precision=lax.Precision.HIGHEST) + b_ref[0]


def _ada(cc, ada_w, ada_b):
    depth, d, n = ada_w.shape
    bn = n // 8
    return pl.pallas_call(
        _ada_kernel,
        grid=(depth, n // bn),
        in_specs=[pl.BlockSpec((8, d), lambda l, j: (0, 0)),
                  pl.BlockSpec((1, d, bn), lambda l, j: (l, 0, j)),
                  pl.BlockSpec((1, 1, bn), lambda l, j: (l, 0, j))],
        out_specs=pl.BlockSpec((1, 8, bn), lambda l, j: (l, 0, j)),
        out_shape=jax.ShapeDtypeStruct((depth, 8, n), F32),
        name="ada_mod",
        compiler_params=_cparams(("arbitrary", "arbitrary")),
    )(cc, ada_w, ada_b.reshape(depth, 1, n))


def _ffn_body(x, m, g, w1_ref, w3_ref, w2_ref, base):
    h = _rms_mod(x, g, m[base:base + 1], m[base + 1:base + 2]).astype(BF16)
    a = _dot(h, w1_ref[...])
    b = _dot(h, w3_ref[...])
    u = (a * jax.nn.sigmoid(a) * b).astype(BF16)
    return x + 0.5 * m[base + 2:base + 3] * _dot(u, w2_ref[...])


def _ffn_kernel(t_ref, m_ref, g_ref, w1_ref, w3_ref, w2_ref, o_ref, *, base):
    o_ref[...] = _ffn_body(t_ref[...], m_ref[0], g_ref[...], w1_ref, w3_ref, w2_ref, base)


def _ffn(t, mod, g, w1, w3, w2, base, latent_only):
    n, d = t.shape
    f = w1.shape[1]
    mod_idx = (lambda i: (1, 0, 0)) if latent_only else (lambda i: (jnp.minimum(i, 1), 0, 0))
    return pl.pallas_call(
        functools.partial(_ffn_kernel, base=base),
        grid=(n // TM,),
        in_specs=[pl.BlockSpec((TM, d), lambda i: (i, 0)),
                  pl.BlockSpec((1, N_MOD, d), mod_idx),
                  _const_spec((1, d)), _const_spec((d, f)), _const_spec((d, f)), _const_spec((f, d))],
        out_specs=pl.BlockSpec((TM, d), lambda i: (i, 0)),
        out_shape=jax.ShapeDtypeStruct((n, d), F32),
        name="half_ffn",
        compiler_params=_cparams(("arbitrary",)),
    )(t, mod, g, w1, w3, w2)


VROWS = HEAD_DIM + 16


def _store_values_t(ref, head0, x):
    xt = x.T.astype(BF16)
    for e in range(2):
        r0 = (head0 + e) * VROWS
        ref[0, r0:r0 + HEAD_DIM, :] = xt[e * HEAD_DIM:(e + 1) * HEAD_DIM]
        ref[0, r0 + HEAD_DIM:r0 + VROWS, :] = jnp.ones((VROWS - HEAD_DIM, x.shape[0]), BF16)

def _proj_ab_kernel(t_ref, m_ref, g_ref, wp_ref, wukv_ref, gq_a_ref, gk_a_ref, gq_b_ref, gk_b_ref, gkv_ref,
                    cos_a_ref, sin_a_ref, cos_b_ref, sin_b_ref,
                    qt_ref, k_ref, vta_ref, vtb_ref):
    m = m_ref[0]
    h = _rms_mod(t_ref[...], g_ref[...], m[3:4], m[4:5]).astype(BF16)
    p = _dot(h, wp_ref[...])
    cos_a, sin_a = cos_a_ref[...], sin_a_ref[...]
    cos_b, sin_b = cos_b_ref[...], sin_b_ref[...]
    o_ak, o_av, o_bq, o_kv, o_kr = 8 * LANES, 9 * LANES, 10 * LANES, 18 * LANES, 20 * LANES

    for hd in range(A_HEADS):
        x = _slab_norm(p[:, hd * LANES:(hd + 1) * LANES], gq_a_ref[:, hd * LANES:(hd + 1) * LANES], HEAD_DIM)
        x = _rope(x, cos_a, sin_a, HEAD_DIM // 4) * (HEAD_DIM ** -0.5 * LOG2E)
        qt_ref[0, hd * LANES:(hd + 1) * LANES, :] = x.T.astype(BF16)
    x = _half_norm(p[:, o_ak:o_ak + LANES], gk_a_ref[...], HEAD_DIM)
    k_ref[:, B_HEADS * LANES:(B_HEADS + 1) * LANES] = _rope(x, cos_a, sin_a, HEAD_DIM // 4).astype(BF16)
    _store_values_t(vta_ref, 0, p[:, o_av:o_av + LANES])
    for hd in range(B_HEADS):
        x = _slab_norm(p[:, o_bq + hd * LANES:o_bq + (hd + 1) * LANES], gq_b_ref[...], B_QK)
        x = _rope(x, cos_b, sin_b, B_ROPE // 4) * (B_QK ** -0.5 * LOG2E)
        qt_ref[0, (A_HEADS + hd) * LANES:(A_HEADS + hd + 1) * LANES, :] = x.T.astype(BF16)
    c = p[:, o_kv:o_kv + B_KV_RANK]
    cn = (c * lax.rsqrt(jnp.mean(c * c, axis=-1, keepdims=True) + EPS) * gkv_ref[...]).astype(BF16)
    kv = _dot(cn, wukv_ref[...])
    kr = p[:, o_kr:o_kr + LANES]
    for hd in range(B_HEADS):
        x = _slab_norm(kv[:, hd * LANES:(hd + 1) * LANES] + kr, gk_b_ref[...], B_QK)
        k_ref[:, hd * LANES:(hd + 1) * LANES] = _rope(x, cos_b, sin_b, B_ROPE // 4).astype(BF16)
    for pr in range(B_HEADS // 2):
        o = (B_HEADS + pr) * LANES
        _store_values_t(vtb_ref, 2 * pr, kv[:, o:o + LANES])


def _proj_ab(t, mod, g, wp, wukv, gq_a, gk_a, gq_b, gk_b, gkv, cos_a, sin_a, cos_b, sin_b):
    n, d = t.shape
    nt = n // TM
    nq = (A_HEADS + B_HEADS) * LANES
    tile = lambda w: pl.BlockSpec((TM, w), lambda i: (i, 0))
    return pl.pallas_call(
        _proj_ab_kernel,
        grid=(nt,),
        in_specs=[tile(d),
                  pl.BlockSpec((1, N_MOD, d), lambda i: (jnp.minimum(i, 1), 0, 0)),
                  _const_spec((1, d)), _const_spec(wp.shape), _const_spec(wukv.shape),
                  _const_spec(gq_a.shape), _const_spec(gk_a.shape), _const_spec(gq_b.shape),
                  _const_spec(gk_b.shape), _const_spec(gkv.shape),
                  tile(LANES), tile(LANES), tile(LANES), tile(LANES)],
        out_specs=[pl.BlockSpec((1, nq, TM), lambda i: (i, 0, 0)),
                   pl.BlockSpec((TM, (B_HEADS + 1) * LANES), lambda i: (i, 0)),
                   pl.BlockSpec((1, A_KV_HEADS * VROWS, TM), lambda i: (i, 0, 0)),
                   pl.BlockSpec((1, B_HEADS * VROWS, TM), lambda i: (i, 0, 0))],
        out_shape=[jax.ShapeDtypeStruct((nt, nq, TM), BF16),
                   jax.ShapeDtypeStruct((n, (B_HEADS + 1) * LANES), BF16),
                   jax.ShapeDtypeStruct((nt, A_KV_HEADS * VROWS, TM), BF16),
                   jax.ShapeDtypeStruct((nt, B_HEADS * VROWS, TM), BF16)],
        name="proj_ab",
        compiler_params=_cparams(("arbitrary",)),
    )(t, mod, g, wp, wukv, gq_a, gk_a, gq_b, gk_b, gkv, cos_a, sin_a, cos_b, sin_b)


def _attn_kernel(qt_ref, k_ref, vt_ref, o_ref, s_ref, mx_ref, *, n_heads, k_slab, v_rows, n_tiles):
    heads = range(n_heads)
    is_ctx = pl.program_id(1) == 0
    n_pairs = jnp.where(is_ctx, 0, (n_tiles - 1) // 2)
    last = jnp.where(is_ctx, 0, n_tiles - 1)

    def scores(kt, slot):
        off = pl.multiple_of(kt * TM, TM)
        for hl in heads:
            r = _dot(k_ref[pl.ds(off, TM), k_slab[hl] * LANES:(k_slab[hl] + 1) * LANES],
                     qt_ref[0, hl * LANES:(hl + 1) * LANES, :])
            s_ref[slot, hl] = r
            mx_ref[slot, hl] = jnp.max(r, axis=0, keepdims=True)

    def consume(kt, slot, carry):
        m, acc = carry
        m_new = tuple(jnp.maximum(m[hl], mx_ref[slot, hl]) for hl in heads)
        alpha = tuple(jnp.exp2(m[hl] - m_new[hl]) for hl in heads)
        pm = tuple(jnp.exp2(s_ref[slot, hl] - m_new[hl]).astype(BF16) for hl in heads)
        acc = tuple(alpha[hl] * acc[hl] + _dot(vt_ref[kt, v_rows[hl] * VROWS:(v_rows[hl] + 1) * VROWS, :], pm[hl])
                    for hl in heads)
        return m_new, acc

    pairs_per_trip = max(u for u in (4, 2, 1) if ((n_tiles - 1) // 2) % u == 0)

    def body(j, carry):
        for u in range(pairs_per_trip):
            t0 = 2 * (j * pairs_per_trip + u)
            scores(t0 + 1, 1)
            carry = consume(t0, 0, carry)
            scores(t0 + 2, 0)
            carry = consume(t0 + 1, 1, carry)
        return carry

    rep = lambda v: (v,) * n_heads
    init = (rep(jnp.full((1, TM), -jnp.inf, F32)), rep(jnp.zeros((VROWS, TM), F32)))
    scores(0, 0)
    _, acc = consume(last, 0, lax.fori_loop(0, n_pairs // pairs_per_trip, body, init))
    for pr in range(n_heads // 2):
        outs = [acc[2 * pr + e][:HEAD_DIM] / acc[2 * pr + e][HEAD_DIM:HEAD_DIM + 1] for e in range(2)]
        o_ref[:, pr * LANES:(pr + 1) * LANES] = jnp.concatenate(outs, axis=0).T.astype(BF16)


def _attn_scratch(n_heads):
    return [pltpu.VMEM((2, n_heads, TM, TM), F32), pltpu.VMEM((2, n_heads, 1, TM), F32)]


def _attn_a(qt, k, vta):
    nt = qt.shape[0]
    assert nt % 2 == 1
    n = nt * TM
    gh = A_HEADS // A_KV_HEADS
    kern = functools.partial(_attn_kernel, n_heads=gh, k_slab=(0,) * gh, v_rows=(0,) * gh, n_tiles=nt)
    return pl.pallas_call(
        kern,
        grid=(A_KV_HEADS, nt),
        in_specs=[pl.BlockSpec((1, gh * LANES, TM), lambda g, i: (i, g, 0)),
                  pl.BlockSpec((n, LANES), lambda g, i: (0, B_HEADS)),
                  pl.BlockSpec((nt, VROWS, TM), lambda g, i: (0, g, 0))],
        out_specs=pl.BlockSpec((TM, gh * HEAD_DIM), lambda g, i: (i, g)),
        out_shape=jax.ShapeDtypeStruct((n, A_HEADS * HEAD_DIM), BF16),
        scratch_shapes=_attn_scratch(gh),
        name="attn_gqa",
        compiler_params=_cparams(("arbitrary", "arbitrary")),
    )(qt, k, vta)


def _attn_b(qt, k, vtb):
    nt = qt.shape[0]
    n = nt * TM
    gh = 4
    hs = tuple(range(gh))
    kern = functools.partial(_attn_kernel, n_heads=gh, k_slab=hs, v_rows=hs, n_tiles=nt)
    return pl.pallas_call(
        kern,
        grid=(B_HEADS // gh, nt),
        in_specs=[pl.BlockSpec((1, gh * LANES, TM), lambda j, i: (i, A_HEADS // gh + j, 0)),
                  pl.BlockSpec((n, gh * LANES), lambda j, i: (0, j), pipeline_mode=pl.Buffered(1)),
                  pl.BlockSpec((nt, gh * VROWS, TM), lambda j, i: (0, j, 0), pipeline_mode=pl.Buffered(1))],
        out_specs=pl.BlockSpec((TM, gh * B_VDIM), lambda j, i: (i, j)),
        out_shape=jax.ShapeDtypeStruct((n, B_HEADS * B_VDIM), BF16),
        scratch_shapes=_attn_scratch(gh),
        name="attn_mla",
        compiler_params=_cparams(("arbitrary", "arbitrary")),
    )(qt, k, vtb)


def _out_ab_kernel(t_ref, m_ref, oa_ref, ob_ref, w_ref, o_ref):
    half = oa_ref.shape[1]
    y = _dot(oa_ref[...], w_ref[:half, :]) + _dot(ob_ref[...], w_ref[half:, :])
    o_ref[...] = t_ref[...] + m_ref[0][5:6] * y


def _out_ab(t, mod, oa, ob, w):
    n, d = t.shape
    tile = lambda wd: pl.BlockSpec((TM, wd), lambda i: (i, 0))
    return pl.pallas_call(
        _out_ab_kernel,
        grid=(n // TM,),
        in_specs=[tile(d), pl.BlockSpec((1, N_MOD, d), lambda i: (jnp.minimum(i, 1), 0, 0)),
                  tile(oa.shape[1]), tile(ob.shape[1]), _const_spec(w.shape)],
        out_specs=tile(d),
        out_shape=jax.ShapeDtypeStruct((n, d), F32),
        name="out_ab",
        compiler_params=_cparams(("arbitrary",)),
    )(t, mod, oa, ob, w)


def _proj_cd_kernel(t_ref, m_ref, g_ref, w_ref, lbp_ref, gq_ref, gk_ref,
                    hq_ref, gf_ref, gb_ref, hv_ref, hg_ref, dq_ref, dk_ref, dv_ref, *, layer):
    m = m_ref[0]
    h = _rms_mod(t_ref[...], g_ref[...], m[3:4], m[4:5]).astype(BF16)
    p = _dot(h, w_ref[...])
    w = C_HEADS * C_DK
    x = p[:, 0:w]
    hq_ref[...] = x * jax.nn.sigmoid(x)
    lbp = lbp_ref[...]
    e = jnp.exp(lbp - jnp.max(lbp, axis=0, keepdims=True))
    sm = e / jnp.sum(e, axis=0, keepdims=True)
    lb = jnp.sum(sm[1:layer + 1], axis=0)
    for dr, ref in ((0, gf_ref), (1, gb_ref)):
        z = p[:, (1 + dr) * w:(2 + dr) * w]
        lbd = lb[dr:dr + 1]
        ref[...] = jnp.log(lbd + (1.0 - lbd) * jax.nn.sigmoid(z))
    hv_ref[...] = p[:, 3 * w:4 * w].astype(BF16)
    hg_ref[...] = p[:, 4 * w:5 * w]
    for s in range(w // LANES):
        sl = slice(s * LANES, (s + 1) * LANES)
        q = _half_norm(p[:, 5 * w + s * LANES:5 * w + (s + 1) * LANES], gq_ref[...], HEAD_DIM)
        dq_ref[:, sl] = (q * (HEAD_DIM ** -0.5)).astype(BF16)
        dk_ref[:, sl] = _half_norm(p[:, 6 * w + s * LANES:6 * w + (s + 1) * LANES], gk_ref[...], HEAD_DIM).astype(BF16)
    dv_ref[...] = p[:, 7 * w:8 * w].astype(BF16)


def _proj_cd(t, mod, g, w_in, hgrn_lb, gq, gk, layer):
    n, d = t.shape
    w = C_HEADS * C_DK
    tile = lambda wd: pl.BlockSpec((TM, wd), lambda i: (i, 0))
    shp = lambda dt: jax.ShapeDtypeStruct((n, w), dt)
    return pl.pallas_call(
        functools.partial(_proj_cd_kernel, layer=layer),
        grid=(n // TM,),
        in_specs=[tile(d), pl.BlockSpec((1, N_MOD, d), lambda i: (jnp.minimum(i, 1), 0, 0)),
                  _const_spec((1, d)), _const_spec(w_in.shape), _const_spec(hgrn_lb.shape),
                  _const_spec(gq.shape), _const_spec(gk.shape)],
        out_specs=[tile(w)] * 8,
        out_shape=[shp(F32), shp(F32), shp(F32), shp(BF16), shp(F32), shp(BF16), shp(BF16), shp(BF16)],
        name="proj_cd",
        compiler_params=_cparams(("arbitrary",)),
    )(t, mod, g, w_in, hgrn_lb, gq, gk)


N_LEVELS = int(math.log2(C_CHUNK))


def _hgrn_tables():
    c = C_CHUNK
    t = np.arange(c)[:, None]
    r = np.arange(c)[None, :]
    blocks = [(r <= t), (r > t)]
    for lv in range(N_LEVELS):
        h = c >> (lv + 1)
        b = (t // (2 * h)) * (2 * h) + h - 1
        right = (t % (2 * h)) >= h
        blocks.append(np.where(right, (r > b) & (r <= t), (r > t) & (r <= b)))
    return np.concatenate(blocks, axis=0).astype(np.float32)


def _hgrn_chunk(q, g, v, st_ref, w_ref, d, reverse):
    c = C_CHUNK
    w = w_ref[...]
    ghi = g.astype(BF16)
    glo = (g - ghi.astype(F32)).astype(BF16)
    xall = _dot(w, ghi) + _dot(w, glo)
    kk = 1.0 - jnp.exp(g)
    row = lax.broadcasted_iota(jnp.int32, (c, q.shape[1]), 0)
    if reverse:
        row = (c - 1) - row
    gi = xall[0:c]
    glast = xall[(c - 1):c] if not reverse else xall[0:1]
    qs = (q * jnp.exp(gi)).astype(BF16)
    ks = (kk * jnp.exp(xall[c:2 * c])).astype(BF16)
    dec = jnp.exp(glast)
    eqs, eks = [], []
    for lv in range(N_LEVELS):
        h = c >> (lv + 1)
        e = jnp.exp(xall[(2 + lv) * c:(3 + lv) * c])
        right = (row & h) != 0
        eqs.append((jnp.where(right, e, 0.0) * q).astype(BF16))
        eks.append((jnp.where(right, 0.0, e) * kk).astype(BF16))
    eqs.append(q.astype(BF16))
    eks.append(kk.astype(BF16))
    ti = lax.broadcasted_iota(jnp.int32, (2 * c, c), 0) % c
    si = lax.broadcasted_iota(jnp.int32, (2 * c, c), 1)
    lane = _lane((c, LANES))
    left = lane < C_DK
    blockdiag = (lax.broadcasted_iota(jnp.int32, (LANES, LANES), 0) < C_DV) == (_lane((LANES, LANES)) < C_DK)
    outs = []
    for pr in range(C_HEADS // 2):
        sl = slice(pr * LANES, (pr + 1) * LANES)
        a2 = jnp.zeros((2 * c, c), F32)
        for lv in range(N_LEVELS + 1):
            eq = eqs[lv][:, sl]
            eq2 = jnp.concatenate([jnp.where(left, eq, 0), jnp.where(left, 0, eq)], axis=0)
            a = _dot_nt(eq2, eks[lv][:, sl])
            if lv < N_LEVELS:
                sh = N_LEVELS - lv
                mask = (ti >> sh) == (si >> sh)
            else:
                mask = ti == si
            a2 = a2 + jnp.where(mask, a, 0.0)
        r2 = _dot(a2.astype(BF16), v[:, sl])
        st = st_ref[d, pr]
        o = jnp.where(left, r2[:c], r2[c:]) + _dot_nt(qs[:, sl], st.astype(BF16))
        outs.append(o)
        upd = _dot_tn(v[:, sl], ks[:, sl])
        st_ref[d, pr] = st * dec[:, sl] + jnp.where(blockdiag, upd, 0.0)
    return jnp.concatenate(outs, axis=1)


def _hgrn_kernel(qf_ref, gf_ref, vf_ref, qb_ref, gb_ref, vb_ref, wf_ref, wb_ref, of_ref, ob_ref, st_ref):
    @pl.when(pl.program_id(0) == 0)
    def _():
        st_ref[...] = jnp.zeros(st_ref.shape, F32)

    c = C_CHUNK
    for j in range(TM // c):
        sl = slice(j * c, (j + 1) * c)
        of_ref[sl, :] = _hgrn_chunk(qf_ref[sl, :], gf_ref[sl, :], vf_ref[sl, :], st_ref, wf_ref, 0, False)
        rl = slice(TM - (j + 1) * c, TM - j * c)
        ob_ref[rl, :] = _hgrn_chunk(qb_ref[rl, :], gb_ref[rl, :], vb_ref[rl, :], st_ref, wb_ref, 1, True)


def _hgrn(hq, gf, gb, hv):
    n, w = hq.shape
    nt = n // TM
    wf = _hgrn_tables()
    wb = wf.reshape(-1, C_CHUNK, C_CHUNK)[:, ::-1, ::-1].reshape(wf.shape)
    fwd = pl.BlockSpec((TM, w), lambda i: (i, 0))
    bwd = pl.BlockSpec((TM, w), lambda i: (jnp.where(i == 0, 0, nt - i), 0))
    return pl.pallas_call(
        _hgrn_kernel,
        grid=(nt,),
        in_specs=[fwd, fwd, fwd, bwd, bwd, bwd, _const_spec(wf.shape), _const_spec(wb.shape)],
        out_specs=[fwd, bwd],
        out_shape=[jax.ShapeDtypeStruct((n, w), F32)] * 2,
        scratch_shapes=[pltpu.VMEM((2, C_HEADS // 2, LANES, LANES), F32)],
        name="hgrn2",
        compiler_params=_cparams(("arbitrary",)),
    )(hq, gf, hv, hq, gb, hv, jnp.asarray(wf, BF16), jnp.asarray(wb, BF16))


NAT_ROWS = TM // GRID_W


def _natten_kernel(q_ref, k_ref, v_ref, bias_ref, o_ref, *, rows):
    i = pl.program_id(1)
    kc = k_ref[0:TM, :]
    vc = v_ref[0:TM, :]
    left_q = _lane((GRID_W, LANES)) < HEAD_DIM
    for j in range(NAT_ROWS):
        r = i * NAT_ROWS + j
        rs = jnp.clip(r - D_WIN_H // 2, 0, rows - D_WIN_H)
        a0 = rs - r + (D_WIN_H - 1)
        off = pl.multiple_of(TM + rs * GRID_W, GRID_W)
        q = q_ref[j * GRID_W:(j + 1) * GRID_W, :]
        q2 = jnp.concatenate([jnp.where(left_q, q, 0), jnp.where(left_q, 0, q)], axis=0)
        kw = k_ref[pl.ds(off, D_WIN_H * GRID_W), :]
        vw = v_ref[pl.ds(off, D_WIN_H * GRID_W), :]
        sw = _dot_nt(q2, kw) + bias_ref[0, a0]
        sc = _dot_nt(q2, kc)
        m = jnp.maximum(jnp.max(sw, axis=-1, keepdims=True), jnp.max(sc, axis=-1, keepdims=True))
        pw = jnp.exp(sw - m)
        pc = jnp.exp(sc - m)
        l = jnp.sum(pw, axis=-1, keepdims=True) + jnp.sum(pc, axis=-1, keepdims=True)
        o2 = (_dot(pw.astype(BF16), vw) + _dot(pc.astype(BF16), vc)) / l
        o_ref[j * GRID_W:(j + 1) * GRID_W, :] = jnp.where(left_q, o2[:GRID_W], o2[GRID_W:]).astype(BF16)


def _natten(dq, dk, dv, bias):
    n, w = dq.shape
    nt = n // TM - 1
    rows = nt * NAT_ROWS
    return pl.pallas_call(
        functools.partial(_natten_kernel, rows=rows),
        grid=(w // LANES, nt),
        in_specs=[pl.BlockSpec((TM, LANES), lambda p, i: (i + 1, p)),
                  pl.BlockSpec((n, LANES), lambda p, i: (0, p)),
                  pl.BlockSpec((n, LANES), lambda p, i: (0, p)),
                  pl.BlockSpec((1, D_WIN_H) + bias.shape[2:], lambda p, i: (p, 0, 0, 0))],
        out_specs=pl.BlockSpec((TM, LANES), lambda p, i: (i, p)),
        out_shape=jax.ShapeDtypeStruct((nt * TM, w), BF16),
        name="natten",
        compiler_params=_cparams(("arbitrary", "arbitrary")),
    )(dq, dk, dv, bias)


def _natten_bias(rpb):
    cols = np.arange(GRID_W)
    start = np.clip(cols - D_WIN_W // 2, 0, GRID_W - D_WIN_W)
    kc = cols[None, :]
    inside = (kc >= start[:, None]) & (kc < start[:, None] + D_WIN_W)
    rel = np.clip(kc - cols[:, None] + (D_WIN_W - 1), 0, 2 * D_WIN_W - 2)
    tab = jnp.where(inside[None, None], rpb[:, :, rel], NEG)
    a = np.arange(D_WIN_H)[:, None] + np.arange(D_WIN_H)[None, :]
    t = tab[:, a]
    t = jnp.transpose(t, (0, 1, 3, 2, 4)).reshape(D_HEADS // 2, 2, D_WIN_H, GRID_W, D_WIN_H * GRID_W)
    return jnp.transpose(t, (0, 2, 1, 3, 4)).reshape(D_HEADS // 2, D_WIN_H, 2 * GRID_W, D_WIN_H * GRID_W)


def _out_cd_kernel(t_ref, m_ref, of_ref, ob_ref, hg_ref, nat_ref, gn_ref, w_ref, o_ref):
    half = nat_ref.shape[1]
    o = of_ref[...] + ob_ref[...]
    gate = hg_ref[...]
    gate = gate * jax.nn.sigmoid(gate)
    parts = []
    for s in range(half // LANES):
        sl = slice(s * LANES, (s + 1) * LANES)
        parts.append((_half_norm(o[:, sl], gn_ref[...], C_DV) * gate[:, sl]).astype(BF16))
    y = _dot(jnp.concatenate(parts, axis=1), w_ref[:half, :]) + _dot(nat_ref[...], w_ref[half:, :])
    o_ref[...] = t_ref[...] + m_ref[0][5:6] * y


def _out_cd(t, mod, o_f, o_b, hg, nat, gn, w):
    n, d = t.shape
    nt = n // TM - 1
    wd = nat.shape[1]
    lat = lambda x: pl.BlockSpec((TM, x), lambda i: (i + 1, 0))
    return pl.pallas_call(
        _out_cd_kernel,
        grid=(nt,),
        in_specs=[lat(d), pl.BlockSpec((1, N_MOD, d), lambda i: (1, 0, 0)),
                  lat(wd), lat(wd), lat(wd), pl.BlockSpec((TM, wd), lambda i: (i, 0)),
                  _const_spec(gn.shape), _const_spec(w.shape)],
        out_specs=pl.BlockSpec((TM, d), lambda i: (i, 0)),
        out_shape=jax.ShapeDtypeStruct((nt * TM, d), F32),
        name="out_cd",
        compiler_params=_cparams(("arbitrary",)),
    )(t, mod, o_f, o_b, hg, nat, gn, w)


def _rope_tables(n_lat, dim, lane0):
    t = jnp.arange(n_lat, dtype=jnp.int32)

    def one(pos, dh):
        inv = ROPE_THETA ** (-jnp.arange(0, dh, 2, dtype=F32) / dh)
        ang = pos.astype(F32)[:, None] * inv[None, :]
        return jnp.concatenate([ang, ang], axis=-1)

    ang = jnp.concatenate([one(t // GRID_W, dim // 2), one(t % GRID_W, dim // 2)], axis=-1)
    cos, sin = jnp.cos(ang), jnp.sin(ang)
    q = dim // 4
    sign = jnp.where((jnp.arange(dim) % (2 * q)) < q, -1.0, 1.0)
    sin = sin * sign
    if dim == HEAD_DIM:
        cos, sin = jnp.concatenate([cos, cos], -1), jnp.concatenate([sin, sin], -1)
    else:
        pad = ((0, 0), (lane0, LANES - lane0 - dim))
        cos = jnp.pad(cos, pad, constant_values=1.0)
        sin = jnp.pad(sin, pad)
    cos = jnp.concatenate([jnp.ones((TM, LANES), F32), cos], axis=0)
    sin = jnp.concatenate([jnp.zeros((TM, LANES), F32), sin], axis=0)
    return cos, sin


def _pad_lanes(x, lo, width=LANES):
    return jnp.pad(x, [(0, 0)] * (x.ndim - 1) + [(lo, width - lo - x.shape[-1])])


def _ab_weights(w_in, a_qn, a_kn, b_qn, b_kn, b_kvn, wukv):
    d = w_in.shape[0]
    grp = A_HEADS // A_KV_HEADS
    o = 0
    aq = w_in[:, o:o + A_HEADS * HEAD_DIM].reshape(d, A_HEADS, HEAD_DIM); o += A_HEADS * HEAD_DIM
    ak = w_in[:, o:o + A_KV_HEADS * HEAD_DIM]; o += A_KV_HEADS * HEAD_DIM
    av = w_in[:, o:o + A_KV_HEADS * HEAD_DIM]; o += A_KV_HEADS * HEAD_DIM
    bq = w_in[:, o:o + B_HEADS * B_QK].reshape(d, B_HEADS, B_QK); o += B_HEADS * B_QK
    bkv = w_in[:, o:o + B_KV_RANK]; o += B_KV_RANK
    bkr = w_in[:, o:o + B_ROPE]
    aq_slabs = jnp.concatenate([_pad_lanes(aq[:, h], (h // grp) * HEAD_DIM) for h in range(A_HEADS)], axis=1)
    bq_slabs = _pad_lanes(bq, 0).reshape(d, B_HEADS * LANES)
    wp = jnp.concatenate([aq_slabs, ak, av, bq_slabs, bkv, _pad_lanes(bkr, B_NOPE)], axis=1).astype(BF16)
    gq_a = jnp.concatenate([_pad_lanes(a_qn[None], (h // grp) * HEAD_DIM) for h in range(A_HEADS)], axis=1)
    gk_a = jnp.concatenate([a_kn, a_kn])[None]
    gq_b = _pad_lanes(b_qn[None], 0)
    gk_b = _pad_lanes(b_kn[None], 0)
    kvw = wukv.reshape(B_KV_RANK, B_HEADS, B_NOPE + B_VDIM)
    wk = _pad_lanes(kvw[:, :, :B_NOPE], 0).reshape(B_KV_RANK, B_HEADS * LANES)
    wv = kvw[:, :, B_NOPE:].reshape(B_KV_RANK, B_HEADS * B_VDIM)
    wukv_p = jnp.concatenate([wk, wv], axis=1).astype(BF16)
    return wp, wukv_p, gq_a, gk_a, gq_b, gk_b, b_kvn[None]


def kernel(x, c, ctx, c_ctx, norm_g, ada_w, ada_b, ffn_w1, ffn_w3, ffn_w2, ab_w_in, ab_a_qn, ab_a_kn, ab_b_qn,
           ab_b_kn, ab_b_kvn, ab_b_wukv, ab_w_out, cd_w_in, hgrn_lb, cd_c_gn, cd_d_qn, cd_d_kn, cd_d_rpb, cd_w_out):
    batch, seq, d = x.shape
    depth = norm_g.shape[0]
    assert batch == 1 and ctx.shape[1] == TM and seq % TM == 0 and depth == 2

    t = jnp.concatenate([ctx[0], x[0]], axis=0)
    cc = jnp.zeros((8, d), F32).at[0].set(c_ctx).at[1].set(c[0])
    mods = _ada(cc, ada_w, ada_b)[:, :2].reshape(depth, 2, N_MOD, d)
    bf = lambda a: a.astype(BF16)

    def ffn(tt, l, j, latent_only=False):
        return _ffn(tt, mods[l], norm_g[l, 2 * j:2 * j + 1], bf(ffn_w1[l, j]), bf(ffn_w3[l, j]),
                    bf(ffn_w2[l, j]), 6 * j, latent_only)

    t = ffn(t, 0, 0)
    wp, wukv_p, gq_a, gk_a, gq_b, gk_b, gkv = _ab_weights(ab_w_in[0], ab_a_qn[0], ab_a_kn[0], ab_b_qn[0],
                                                         ab_b_kn[0], ab_b_kvn[0], ab_b_wukv[0])
    cos_a, sin_a = _rope_tables(seq, HEAD_DIM, 0)
    cos_b, sin_b = _rope_tables(seq, B_ROPE, B_NOPE)
    qt, k, vta, vtb = _proj_ab(t, mods[0], norm_g[0, 1:2], wp, wukv_p, gq_a, gk_a, gq_b, gk_b, gkv,
                               cos_a, sin_a, cos_b, sin_b)
    oa = _attn_a(qt, k, vta)
    ob = _attn_b(qt, k, vtb)
    t = _out_ab(t, mods[0], oa, ob, bf(ab_w_out[0]))
    t = ffn(t, 0, 1)

    t = ffn(t, 1, 0)
    gq = jnp.concatenate([cd_d_qn[0], cd_d_qn[0]])[None]
    gk = jnp.concatenate([cd_d_kn[0], cd_d_kn[0]])[None]
    gn = jnp.concatenate([cd_c_gn[0], cd_c_gn[0]])[None]
    hq, gf, gb, hv, hg, dq, dk, dv = _proj_cd(t, mods[1], norm_g[1, 1:2], bf(cd_w_in[0]), hgrn_lb, gq, gk, 1)
    o_f, o_b = _hgrn(hq, gf, gb, hv)
    nat = _natten(dq, dk, dv, _natten_bias(cd_d_rpb[0]))
    xl = _out_cd(t, mods[1], o_f, o_b, hg, nat, gn, bf(cd_w_out[0]))
    xl = ffn(xl, 1, 1, latent_only=True)
    return xl[None]
```

```python
import functools
import math

import numpy as np
import jax
import jax.numpy as jnp
from jax import lax
from jax.experimental import pallas as pl
from jax.experimental.pallas import tpu as pltpu

F32 = jnp.float32
BF16 = jnp.bfloat16

GRID_W = 64
HEAD_DIM = 64
EPS = 1e-6
ROPE_THETA = 10000.0
N_MOD = 9
A_HEADS, A_KV_HEADS = 8, 2
B_HEADS, B_NOPE, B_ROPE, B_VDIM, B_KV_RANK = 8, 64, 32, 64, 256
B_QK = B_NOPE + B_ROPE
C_HEADS, C_DK, C_DV, C_CHUNK = 8, 64, 64, 64
D_HEADS, D_WIN_H, D_WIN_W = 8, 8, 16

LANES = 128
TM = 256
VMEM_LIMIT = 56 * 1024 * 1024
NEG = -1e30
LOG2E = math.log2(math.e)


def _cparams(sem):
    return pltpu.CompilerParams(dimension_semantics=sem, vmem_limit_bytes=VMEM_LIMIT)


def _const_spec(shape):
    nd = len(shape)
    return pl.BlockSpec(shape, lambda *_: (0,) * nd, pipeline_mode=pl.Buffered(1))


def _lane(shape):
    return lax.broadcasted_iota(jnp.int32, shape, len(shape) - 1)


def _rms_mod(x, g, shift, scale):
    y = x * lax.rsqrt(jnp.mean(x * x, axis=-1, keepdims=True) + EPS) * g
    return y * (1.0 + scale) + shift


def _half_norm(x, gain, n):
    left = _lane(x.shape) < n
    x2 = x * x
    sl = jnp.sum(jnp.where(left, x2, 0.0), axis=-1, keepdims=True)
    sr = jnp.sum(jnp.where(left, 0.0, x2), axis=-1, keepdims=True)
    r = jnp.where(left, lax.rsqrt(sl / n + EPS), lax.rsqrt(sr / n + EPS))
    return x * r * gain


def _slab_norm(x, gain, n):
    ss = jnp.sum(x * x, axis=-1, keepdims=True)
    return x * lax.rsqrt(ss / n + EPS) * gain


def _rope(x, cos, sin_signed, half):
    first = (_lane(x.shape) % (2 * half)) < half
    w = x.shape[-1]
    partner = jnp.where(first, pltpu.roll(x, w - half, 1), pltpu.roll(x, half, 1))
    return x * cos + partner * sin_signed


def _dot(a, b):
    return jnp.dot(a, b, preferred_element_type=F32)


def _dot_nt(a, b):
    return lax.dot_general(a, b, (((1,), (1,)), ((), ())), preferred_element_type=F32)


def _dot_tn(a, b):
    return lax.dot_general(a, b, (((0,), (0,)), ((), ())), preferred_element_type=F32)


def _ada_kernel(c_ref, w_ref, b_ref, o_ref):
    c = c_ref[...]
    a = c * jax.nn.sigmoid(c)
    o_ref[0] = jnp.dot(a, w_ref[0], preferred_element_type=F32, precision=lax.Precision.HIGHEST) + b_ref[0]


def _ada(cc, ada_w, ada_b):
    depth, d, n = ada_w.shape
    bn = n // 8
    return pl.pallas_call(
        _ada_kernel,
        grid=(depth, n // bn),
        in_specs=[pl.BlockSpec((8, d), lambda l, j: (0, 0)),
                  pl.BlockSpec((1, d, bn), lambda l, j: (l, 0, j)),
                  pl.BlockSpec((1, 1, bn), lambda l, j: (l, 0, j))],
        out_specs=pl.BlockSpec((1, 8, bn), lambda l, j: (l, 0, j)),
        out_shape=jax.ShapeDtypeStruct((depth, 8, n), F32),
        name="ada_mod",
        compiler_params=_cparams(("arbitrary", "arbitrary")),
    )(cc, ada_w, ada_b.reshape(depth, 1, n))


def _ffn_body(x, m, g, w1_ref, w3_ref, w2_ref, base):
    h = _rms_mod(x, g, m[base:base + 1], m[base + 1:base + 2]).astype(BF16)
    a = _dot(h, w1_ref[...])
    b = _dot(h, w3_ref[...])
    u = (a * jax.nn.sigmoid(a) * b).astype(BF16)
    return x + 0.5 * m[base + 2:base + 3] * _dot(u, w2_ref[...])


def _ffn_kernel(t_ref, m_ref, g_ref, w1_ref, w3_ref, w2_ref, o_ref, *, base):
    o_ref[...] = _ffn_body(t_ref[...], m_ref[0], g_ref[...], w1_ref, w3_ref, w2_ref, base)


def _ffn_first_kernel(c_ref, x_ref, m_ref, g_ref, w1_ref, w3_ref, w2_ref, o_ref, *, base):
    x = jnp.where(pl.program_id(0) == 0, c_ref[...], x_ref[...])
    o_ref[...] = _ffn_body(x, m_ref[0], g_ref[...], w1_ref, w3_ref, w2_ref, base)


def _ffn_first(ctx, x, mod, g, w1, w3, w2, base):
    n, d = x.shape
    f = w1.shape[1]
    return pl.pallas_call(
        functools.partial(_ffn_first_kernel, base=base),
        grid=(n // TM + 1,),
        in_specs=[pl.BlockSpec((TM, d), lambda i: (0, 0)),
                  pl.BlockSpec((TM, d), lambda i: (jnp.maximum(i - 1, 0), 0)),
                  pl.BlockSpec((1, N_MOD, d), lambda i: (jnp.minimum(i, 1), 0, 0)),
                  _const_spec((1, d)), _const_spec((d, f)), _const_spec((d, f)), _const_spec((f, d))],
        out_specs=pl.BlockSpec((TM, d), lambda i: (i, 0)),
        out_shape=jax.ShapeDtypeStruct((n + TM, d), F32),
        name="half_ffn_first",
        compiler_params=_cparams(("arbitrary",)),
    )(ctx, x, mod, g, w1, w3, w2)


def _ffn(t, mod, g, w1, w3, w2, base, latent_only):
    n, d = t.shape
    f = w1.shape[1]
    mod_idx = (lambda i: (1, 0, 0)) if latent_only else (lambda i: (jnp.minimum(i, 1), 0, 0))
    return pl.pallas_call(
        functools.partial(_ffn_kernel, base=base),
        grid=(n // TM,),
        in_specs=[pl.BlockSpec((TM, d), lambda i: (i, 0)),
                  pl.BlockSpec((1, N_MOD, d), mod_idx),
                  _const_spec((1, d)), _const_spec((d, f)), _const_spec((d, f)), _const_spec((f, d))],
        out_specs=pl.BlockSpec((TM, d), lambda i: (i, 0)),
        out_shape=jax.ShapeDtypeStruct((n, d), F32),
        name="half_ffn",
        compiler_params=_cparams(("arbitrary",)),
    )(t, mod, g, w1, w3, w2)


VROWS = HEAD_DIM + 16


def _store_values_t(ref, head0, x):
    xt = x.T.astype(BF16)
    for e in range(2):
        r0 = (head0 + e) * VROWS
        ref[0, r0:r0 + HEAD_DIM, :] = xt[e * HEAD_DIM:(e + 1) * HEAD_DIM]
        ref[0, r0 + HEAD_DIM:r0 + VROWS, :] = jnp.ones((VROWS - HEAD_DIM, x.shape[0]), BF16)

def _proj_ab_kernel(t_ref, m_ref, g_ref, wp_ref, wukv_ref, gq_a_ref, gk_a_ref, gq_b_ref, gk_b_ref, gkv_ref,
                    cos_a_ref, sin_a_ref, cos_b_ref, sin_b_ref,
                    qt_ref, k_ref, vta_ref, vtb_ref):
    m = m_ref[0]
    h = _rms_mod(t_ref[...], g_ref[...], m[3:4], m[4:5]).astype(BF16)
    p = _dot(h, wp_ref[...])
    cos_a, sin_a = cos_a_ref[...], sin_a_ref[...]
    cos_b, sin_b = cos_b_ref[...], sin_b_ref[...]
    o_ak, o_av, o_bq, o_kv, o_kr = 8 * LANES, 9 * LANES, 10 * LANES, 18 * LANES, 20 * LANES

    for hd in range(A_HEADS):
        x = _slab_norm(p[:, hd * LANES:(hd + 1) * LANES], gq_a_ref[:, hd * LANES:(hd + 1) * LANES], HEAD_DIM)
        x = _rope(x, cos_a, sin_a, HEAD_DIM // 4) * (HEAD_DIM ** -0.5 * LOG2E)
        qt_ref[0, hd * LANES:(hd + 1) * LANES, :] = x.T.astype(BF16)
    x = _half_norm(p[:, o_ak:o_ak + LANES], gk_a_ref[...], HEAD_DIM)
    k_ref[:, B_HEADS * LANES:(B_HEADS + 1) * LANES] = _rope(x, cos_a, sin_a, HEAD_DIM // 4).astype(BF16)
    _store_values_t(vta_ref, 0, p[:, o_av:o_av + LANES])
    for hd in range(B_HEADS):
        x = _slab_norm(p[:, o_bq + hd * LANES:o_bq + (hd + 1) * LANES], gq_b_ref[...], B_QK)
        x = _rope(x, cos_b, sin_b, B_ROPE // 4) * (B_QK ** -0.5 * LOG2E)
        qt_ref[0, (A_HEADS + hd) * LANES:(A_HEADS + hd + 1) * LANES, :] = x.T.astype(BF16)
    c = p[:, o_kv:o_kv + B_KV_RANK]
    cn = (c * lax.rsqrt(jnp.mean(c * c, axis=-1, keepdims=True) + EPS) * gkv_ref[...]).astype(BF16)
    kv = _dot(cn, wukv_ref[...])
    kr = p[:, o_kr:o_kr + LANES]
    for hd in range(B_HEADS):
        x = _slab_norm(kv[:, hd * LANES:(hd + 1) * LANES] + kr, gk_b_ref[...], B_QK)
        k_ref[:, hd * LANES:(hd + 1) * LANES] = _rope(x, cos_b, sin_b, B_ROPE // 4).astype(BF16)
    for pr in range(B_HEADS // 2):
        o = (B_HEADS + pr) * LANES
        _store_values_t(vtb_ref, 2 * pr, kv[:, o:o + LANES])


def _proj_ab(t, mod, g, wp, wukv, gq_a, gk_a, gq_b, gk_b, gkv, cos_a, sin_a, cos_b, sin_b):
    n, d = t.shape
    nt = n // TM
    nq = (A_HEADS + B_HEADS) * LANES
    tile = lambda w: pl.BlockSpec((TM, w), lambda i: (i, 0))
    return pl.pallas_call(
        _proj_ab_kernel,
        grid=(nt,),
        in_specs=[tile(d),
                  pl.BlockSpec((1, N_MOD, d), lambda i: (jnp.minimum(i, 1), 0, 0)),
                  _const_spec((1, d)), _const_spec(wp.shape), _const_spec(wukv.shape),
                  _const_spec(gq_a.shape), _const_spec(gk_a.shape), _const_spec(gq_b.shape),
                  _const_spec(gk_b.shape), _const_spec(gkv.shape),
                  tile(LANES), tile(LANES), tile(LANES), tile(LANES)],
        out_specs=[pl.BlockSpec((1, nq, TM), lambda i: (i, 0, 0)),
                   pl.BlockSpec((TM, (B_HEADS + 1) * LANES), lambda i: (i, 0)),
                   pl.BlockSpec((1, A_KV_HEADS * VROWS, TM), lambda i: (i, 0, 0)),
                   pl.BlockSpec((1, B_HEADS * VROWS, TM), lambda i: (i, 0, 0))],
        out_shape=[jax.ShapeDtypeStruct((nt, nq, TM), BF16),
                   jax.ShapeDtypeStruct((n, (B_HEADS + 1) * LANES), BF16),
                   jax.ShapeDtypeStruct((nt, A_KV_HEADS * VROWS, TM), BF16),
                   jax.ShapeDtypeStruct((nt, B_HEADS * VROWS, TM), BF16)],
        name="proj_ab",
        compiler_params=_cparams(("arbitrary",)),
    )(t, mod, g, wp, wukv, gq_a, gk_a, gq_b, gk_b, gkv, cos_a, sin_a, cos_b, sin_b)


def _attn_kernel(qt_ref, k_ref, vt_ref, o_ref, s_ref, mx_ref, *, n_heads, k_slab, v_rows, n_tiles):
    heads = range(n_heads)
    is_ctx = pl.program_id(1) == 0
    n_pairs = jnp.where(is_ctx, 0, (n_tiles - 1) // 2)
    last = jnp.where(is_ctx, 0, n_tiles - 1)

    def scores(kt, slot):
        off = pl.multiple_of(kt * TM, TM)
        for hl in heads:
            r = _dot(k_ref[pl.ds(off, TM), k_slab[hl] * LANES:(k_slab[hl] + 1) * LANES],
                     qt_ref[0, hl * LANES:(hl + 1) * LANES, :])
            s_ref[slot, hl] = r
            mx_ref[slot, hl] = jnp.max(r, axis=0, keepdims=True)

    def consume(kt, slot, carry):
        m, acc = carry
        m_new = tuple(jnp.maximum(m[hl], mx_ref[slot, hl]) for hl in heads)
        alpha = tuple(jnp.exp2(m[hl] - m_new[hl]) for hl in heads)
        pm = tuple(jnp.exp2(s_ref[slot, hl] - m_new[hl]).astype(BF16) for hl in heads)
        acc = tuple(alpha[hl] * acc[hl] + _dot(vt_ref[kt, v_rows[hl] * VROWS:(v_rows[hl] + 1) * VROWS, :], pm[hl])
                    for hl in heads)
        return m_new, acc

    pairs_per_trip = max(u for u in (8, 4, 2, 1) if ((n_tiles - 1) // 2) % u == 0)

    def body(j, carry):
        for u in range(pairs_per_trip):
            t0 = 2 * (j * pairs_per_trip + u)
            scores(t0 + 1, 1)
            carry = consume(t0, 0, carry)
            scores(t0 + 2, 0)
            carry = consume(t0 + 1, 1, carry)
        return carry

    rep = lambda v: (v,) * n_heads
    init = (rep(jnp.full((1, TM), -jnp.inf, F32)), rep(jnp.zeros((VROWS, TM), F32)))
    scores(0, 0)
    _, acc = consume(last, 0, lax.fori_loop(0, n_pairs // pairs_per_trip, body, init))
    for pr in range(n_heads // 2):
        outs = [acc[2 * pr + e][:HEAD_DIM] / acc[2 * pr + e][HEAD_DIM:HEAD_DIM + 1] for e in range(2)]
        o_ref[:, pr * LANES:(pr + 1) * LANES] = jnp.concatenate(outs, axis=0).T.astype(BF16)


def _attn_scratch(n_heads):
    return [pltpu.VMEM((2, n_heads, TM, TM), F32), pltpu.VMEM((2, n_heads, 1, TM), F32)]


def _attn_a(qt, k, vta):
    nt = qt.shape[0]
    assert nt % 2 == 1
    n = nt * TM
    gh = A_HEADS // A_KV_HEADS
    kern = functools.partial(_attn_kernel, n_heads=gh, k_slab=(0,) * gh, v_rows=(0,) * gh, n_tiles=nt)
    return pl.pallas_call(
        kern,
        grid=(A_KV_HEADS, nt),
        in_specs=[pl.BlockSpec((1, gh * LANES, TM), lambda g, i: (i, g, 0)),
                  pl.BlockSpec((n, LANES), lambda g, i: (0, B_HEADS)),
                  pl.BlockSpec((nt, VROWS, TM), lambda g, i: (0, g, 0))],
        out_specs=pl.BlockSpec((TM, gh * HEAD_DIM), lambda g, i: (i, g)),
        out_shape=jax.ShapeDtypeStruct((n, A_HEADS * HEAD_DIM), BF16),
        scratch_shapes=_attn_scratch(gh),
        name="attn_gqa",
        compiler_params=_cparams(("arbitrary", "arbitrary")),
    )(qt, k, vta)


def _attn_b(qt, k, vtb):
    nt = qt.shape[0]
    n = nt * TM
    gh = 4
    hs = tuple(range(gh))
    kern = functools.partial(_attn_kernel, n_heads=gh, k_slab=hs, v_rows=hs, n_tiles=nt)
    return pl.pallas_call(
        kern,
        grid=(B_HEADS // gh, nt),
        in_specs=[pl.BlockSpec((1, gh * LANES, TM), lambda j, i: (i, A_HEADS // gh + j, 0)),
                  pl.BlockSpec((n, gh * LANES), lambda j, i: (0, j), pipeline_mode=pl.Buffered(1)),
                  pl.BlockSpec((nt, gh * VROWS, TM), lambda j, i: (0, j, 0), pipeline_mode=pl.Buffered(1))],
        out_specs=pl.BlockSpec((TM, gh * B_VDIM), lambda j, i: (i, j)),
        out_shape=jax.ShapeDtypeStruct((n, B_HEADS * B_VDIM), BF16),
        scratch_shapes=_attn_scratch(gh),
        name="attn_mla",
        compiler_params=_cparams(("arbitrary", "arbitrary")),
    )(qt, k, vtb)


def _out_ab_kernel(t_ref, m_ref, oa_ref, ob_ref, w_ref, o_ref):
    half = oa_ref.shape[1]
    y = _dot(oa_ref[...], w_ref[:half, :]) + _dot(ob_ref[...], w_ref[half:, :])
    o_ref[...] = t_ref[...] + m_ref[0][5:6] * y


def _out_ab(t, mod, oa, ob, w):
    n, d = t.shape
    tile = lambda wd: pl.BlockSpec((TM, wd), lambda i: (i, 0))
    return pl.pallas_call(
        _out_ab_kernel,
        grid=(n // TM,),
        in_specs=[tile(d), pl.BlockSpec((1, N_MOD, d), lambda i: (jnp.minimum(i, 1), 0, 0)),
                  tile(oa.shape[1]), tile(ob.shape[1]), _const_spec(w.shape)],
        out_specs=tile(d),
        out_shape=jax.ShapeDtypeStruct((n, d), F32),
        name="out_ab",
        compiler_params=_cparams(("arbitrary",)),
    )(t, mod, oa, ob, w)


def _proj_cd_kernel(t_ref, m_ref, g_ref, w_ref, lbp_ref, gq_ref, gk_ref,
                    hq_ref, gf_ref, gb_ref, hv_ref, hg_ref, dq_ref, dk_ref, dv_ref, *, layer):
    m = m_ref[0]
    h = _rms_mod(t_ref[...], g_ref[...], m[3:4], m[4:5]).astype(BF16)
    p = _dot(h, w_ref[...])
    w = C_HEADS * C_DK
    x = p[:, 0:w]
    hq_ref[...] = x * jax.nn.sigmoid(x)
    lbp = lbp_ref[...]
    e = jnp.exp(lbp - jnp.max(lbp, axis=0, keepdims=True))
    sm = e / jnp.sum(e, axis=0, keepdims=True)
    lb = jnp.sum(sm[1:layer + 1], axis=0)
    for dr, ref in ((0, gf_ref), (1, gb_ref)):
        z = p[:, (1 + dr) * w:(2 + dr) * w]
        lbd = lb[dr:dr + 1]
        ref[...] = jnp.log(lbd + (1.0 - lbd) * jax.nn.sigmoid(z))
    hv_ref[...] = p[:, 3 * w:4 * w].astype(BF16)
    hg_ref[...] = p[:, 4 * w:5 * w]
    for s in range(w // LANES):
        sl = slice(s * LANES, (s + 1) * LANES)
        q = _half_norm(p[:, 5 * w + s * LANES:5 * w + (s + 1) * LANES], gq_ref[...], HEAD_DIM)
        dq_ref[:, sl] = (q * (HEAD_DIM ** -0.5 * LOG2E)).astype(BF16)
        dk_ref[:, sl] = _half_norm(p[:, 6 * w + s * LANES:6 * w + (s + 1) * LANES], gk_ref[...], HEAD_DIM).astype(BF16)
    dv_ref[...] = p[:, 7 * w:8 * w].astype(BF16)


def _proj_cd(t, mod, g, w_in, hgrn_lb, gq, gk, layer):
    n, d = t.shape
    w = C_HEADS * C_DK
    tile = lambda wd: pl.BlockSpec((TM, wd), lambda i: (i, 0))
    shp = lambda dt: jax.ShapeDtypeStruct((n, w), dt)
    return pl.pallas_call(
        functools.partial(_proj_cd_kernel, layer=layer),
        grid=(n // TM,),
        in_specs=[tile(d), pl.BlockSpec((1, N_MOD, d), lambda i: (jnp.minimum(i, 1), 0, 0)),
                  _const_spec((1, d)), _const_spec(w_in.shape), _const_spec(hgrn_lb.shape),
                  _const_spec(gq.shape), _const_spec(gk.shape)],
        out_specs=[tile(w)] * 8,
        out_shape=[shp(F32), shp(F32), shp(F32), shp(BF16), shp(F32), shp(BF16), shp(BF16), shp(BF16)],
        name="proj_cd",
        compiler_params=_cparams(("arbitrary",)),
    )(t, mod, g, w_in, hgrn_lb, gq, gk)


N_LEVELS = int(math.log2(C_CHUNK))


def _hgrn_tables():
    t = np.arange(C_CHUNK)
    return (t[None, :] <= t[:, None]).astype(np.float32)


def _level_boundary(gi, h, reverse):
    c = gi.shape[0]
    if h >= 4:
        blocks = []
        for s in range(0, c, 2 * h):
            b = s + (h if reverse else h - 1)
            blocks.append(jnp.broadcast_to(gi[b:b + 1], (2 * h, gi.shape[1])))
        return jnp.concatenate(blocks, axis=0)
    pos = lax.broadcasted_iota(jnp.int32, gi.shape, 0) % (2 * h)
    gb = gi
    for p in range(2 * h):
        d = (h if reverse else h - 1) - p
        if d != 0:
            gb = jnp.where(pos == p, pltpu.roll(gi, (c - d) % c, 0), gb)
    return gb


def _level_masks(reverse):
    c = C_CHUNK
    ti = lax.broadcasted_iota(jnp.int32, (2 * c, c), 0) % c
    si = lax.broadcasted_iota(jnp.int32, (2 * c, c), 1)
    tq, sk = (si, ti) if reverse else (ti, si)
    masks = []
    for lv in range(N_LEVELS):
        h = c >> (lv + 1)
        sh = N_LEVELS - lv
        masks.append(((ti >> sh) == (si >> sh)) & ((tq & h) != 0) & ((sk & h) == 0))
    masks.append(ti == si)
    return masks


def _hgrn_chunks(jobs, st_ref):
    c = C_CHUNK
    gis = []
    for q, g, v, tri, masks, d, reverse in jobs:
        ghi = g.astype(BF16)
        glo = (g - ghi.astype(F32)).astype(BF16)
        gis.append(_dot(tri, ghi) + _dot(tri, glo))
    ops = []
    for (q, g, v, tri, masks, d, reverse), gi in zip(jobs, gis):
        kk = 1.0 - jnp.exp(g)
        glast = gi[(c - 1):c] if not reverse else gi[0:1]
        qs = (q * jnp.exp(gi)).astype(BF16)
        ks = (kk * jnp.exp(glast - gi)).astype(BF16)
        eqs, eks = [], []
        for lv in range(N_LEVELS):
            e = jnp.exp(-jnp.abs(gi - _level_boundary(gi, c >> (lv + 1), reverse)))
            eqs.append((e * q).astype(BF16))
            eks.append((e * kk).astype(BF16))
        eqs.append(q.astype(BF16))
        eks.append(kk.astype(BF16))
        ops.append((eqs, eks, qs, ks, jnp.exp(glast)))
    left = _lane((c, LANES)) < C_DK
    blockdiag = (lax.broadcasted_iota(jnp.int32, (LANES, LANES), 0) < C_DV) == (_lane((LANES, LANES)) < C_DK)
    outs = [[] for _ in jobs]
    for pr in range(C_HEADS // 2):
        sl = slice(pr * LANES, (pr + 1) * LANES)
        a2s = [jnp.zeros((2 * c, c), F32) for _ in jobs]
        for lv in range(N_LEVELS + 1):
            for n, job in enumerate(jobs):
                eq = ops[n][0][lv][:, sl]
                eq2 = jnp.concatenate([jnp.where(left, eq, 0), jnp.where(left, 0, eq)], axis=0)
                a = _dot_nt(eq2, ops[n][1][lv][:, sl])
                a2s[n] = jnp.where(job[4][lv], a, a2s[n])
        r2s = [_dot(a2s[n].astype(BF16), job[2][:, sl]) for n, job in enumerate(jobs)]
        upds = [_dot_tn(job[2][:, sl], ops[n][3][:, sl]) for n, job in enumerate(jobs)]
        for n, job in enumerate(jobs):
            d = job[5]
            st = st_ref[d, pr]
            outs[n].append(jnp.where(left, r2s[n][:c], r2s[n][c:]) + _dot_nt(ops[n][2][:, sl], st.astype(BF16)))
            st_ref[d, pr] = st * ops[n][4][:, sl] + jnp.where(blockdiag, upds[n], 0.0)
    return [jnp.concatenate(o, axis=1) for o in outs]


HGRN_CHUNKS_PER_PASS = 2


def _hgrn_kernel(qf_ref, gf_ref, vf_ref, qb_ref, gb_ref, vb_ref, wf_ref, wb_ref, of_ref, ob_ref, st_ref):
    @pl.when(pl.program_id(0) == 0)
    def _():
        st_ref[...] = jnp.zeros(st_ref.shape, F32)

    c = C_CHUNK
    mf, mb = _level_masks(False), _level_masks(True)
    tf, tb = wf_ref[...], wb_ref[...]
    for j0 in range(0, TM // c, HGRN_CHUNKS_PER_PASS):
        js = range(j0, j0 + HGRN_CHUNKS_PER_PASS)
        fs = [slice(j * c, (j + 1) * c) for j in js]
        bs = [slice(TM - (j + 1) * c, TM - j * c) for j in js]
        jobs = ([(qf_ref[s, :], gf_ref[s, :], vf_ref[s, :], tf, mf, 0, False) for s in fs]
                + [(qb_ref[s, :], gb_ref[s, :], vb_ref[s, :], tb, mb, 1, True) for s in bs])
        outs = _hgrn_chunks(jobs, st_ref)
        for s, o in zip(fs, outs[:len(fs)]):
            of_ref[s, :] = o
        for s, o in zip(bs, outs[len(fs):]):
            ob_ref[s, :] = o


def _hgrn(hq, gf, gb, hv):
    n, w = hq.shape
    nt = n // TM
    wf = _hgrn_tables()
    wb = wf.reshape(-1, C_CHUNK, C_CHUNK)[:, ::-1, ::-1].reshape(wf.shape)
    fwd = pl.BlockSpec((TM, w), lambda i: (i, 0))
    bwd = pl.BlockSpec((TM, w), lambda i: (jnp.where(i == 0, 0, nt - i), 0))
    return pl.pallas_call(
        _hgrn_kernel,
        grid=(nt,),
        in_specs=[fwd, fwd, fwd, bwd, bwd, bwd, _const_spec(wf.shape), _const_spec(wb.shape)],
        out_specs=[fwd, bwd],
        out_shape=[jax.ShapeDtypeStruct((n, w), F32)] * 2,
        scratch_shapes=[pltpu.VMEM((2, C_HEADS // 2, LANES, LANES), F32)],
        name="hgrn2",
        compiler_params=_cparams(("arbitrary",)),
    )(hq, gf, hv, hq, gb, hv, jnp.asarray(wf, BF16), jnp.asarray(wb, BF16))


NAT_ROWS = TM // GRID_W


def _natten_kernel(q_ref, k_ref, v_ref, bias_ref, o_ref, *, rows):
    i = pl.program_id(1)
    kc = k_ref[0:TM, :]
    vc = v_ref[0:TM, :]
    left_q = _lane((GRID_W, LANES)) < HEAD_DIM
    q2s, offs, sws = [], [], []
    for j in range(NAT_ROWS):
        r = i * NAT_ROWS + j
        rs = jnp.clip(r - D_WIN_H // 2, 0, rows - D_WIN_H)
        offs.append(pl.multiple_of(TM + rs * GRID_W, GRID_W))
        q = q_ref[j * GRID_W:(j + 1) * GRID_W, :]
        q2s.append(jnp.concatenate([jnp.where(left_q, q, 0), jnp.where(left_q, 0, q)], axis=0))
        kw = k_ref[pl.ds(offs[j], D_WIN_H * GRID_W), :]
        sws.append(_dot_nt(q2s[j], kw) + bias_ref[0, rs - r + (D_WIN_H - 1)])
    sc = _dot_nt(jnp.concatenate(q2s, axis=0), kc)
    pws, pcs, ls = [], [], []
    for j in range(NAT_ROWS):
        scj = sc[j * 2 * GRID_W:(j + 1) * 2 * GRID_W]
        m = jnp.maximum(jnp.max(sws[j], axis=-1, keepdims=True), jnp.max(scj, axis=-1, keepdims=True))
        pw = jnp.exp2(sws[j] - m)
        pc = jnp.exp2(scj - m)
        ls.append(jnp.sum(pw, axis=-1, keepdims=True) + jnp.sum(pc, axis=-1, keepdims=True))
        pws.append(pw.astype(BF16))
        pcs.append(pc.astype(BF16))
    oc = _dot(jnp.concatenate(pcs, axis=0), vc)
    for j in range(NAT_ROWS):
        vw = v_ref[pl.ds(offs[j], D_WIN_H * GRID_W), :]
        o2 = (_dot(pws[j], vw) + oc[j * 2 * GRID_W:(j + 1) * 2 * GRID_W]) / ls[j]
        o_ref[j * GRID_W:(j + 1) * GRID_W, :] = jnp.where(left_q, o2[:GRID_W], o2[GRID_W:]).astype(BF16)


def _natten(dq, dk, dv, bias):
    n, w = dq.shape
    nt = n // TM - 1
    rows = nt * NAT_ROWS
    return pl.pallas_call(
        functools.partial(_natten_kernel, rows=rows),
        grid=(w // LANES, nt),
        in_specs=[pl.BlockSpec((TM, LANES), lambda p, i: (i + 1, p)),
                  pl.BlockSpec((n, LANES), lambda p, i: (0, p)),
                  pl.BlockSpec((n, LANES), lambda p, i: (0, p)),
                  pl.BlockSpec((1, D_WIN_H) + bias.shape[2:], lambda p, i: (p, 0, 0, 0))],
        out_specs=pl.BlockSpec((TM, LANES), lambda p, i: (i, p)),
        out_shape=jax.ShapeDtypeStruct((nt * TM, w), BF16),
        name="natten",
        compiler_params=_cparams(("arbitrary", "arbitrary")),
    )(dq, dk, dv, bias)


def _natten_bias(rpb):
    cols = np.arange(GRID_W)
    start = np.clip(cols - D_WIN_W // 2, 0, GRID_W - D_WIN_W)
    kc = cols[None, :]
    inside = (kc >= start[:, None]) & (kc < start[:, None] + D_WIN_W)
    rel = np.clip(kc - cols[:, None] + (D_WIN_W - 1), 0, 2 * D_WIN_W - 2)
    tab = jnp.where(inside[None, None], rpb[:, :, rel] * LOG2E, NEG)
    a = np.arange(D_WIN_H)[:, None] + np.arange(D_WIN_H)[None, :]
    t = tab[:, a]
    t = jnp.transpose(t, (0, 1, 3, 2, 4)).reshape(D_HEADS // 2, 2, D_WIN_H, GRID_W, D_WIN_H * GRID_W)
    return jnp.transpose(t, (0, 2, 1, 3, 4)).reshape(D_HEADS // 2, D_WIN_H, 2 * GRID_W, D_WIN_H * GRID_W)


def _out_cd_kernel(t_ref, m_ref, of_ref, ob_ref, hg_ref, nat_ref, gn_ref, w_ref, o_ref):
    half = nat_ref.shape[1]
    o = of_ref[...] + ob_ref[...]
    gate = hg_ref[...]
    gate = gate * jax.nn.sigmoid(gate)
    parts = []
    for s in range(half // LANES):
        sl = slice(s * LANES, (s + 1) * LANES)
        parts.append((_half_norm(o[:, sl], gn_ref[...], C_DV) * gate[:, sl]).astype(BF16))
    y = _dot(jnp.concatenate(parts, axis=1), w_ref[:half, :]) + _dot(nat_ref[...], w_ref[half:, :])
    o_ref[...] = t_ref[...] + m_ref[0][5:6] * y


def _out_cd(t, mod, o_f, o_b, hg, nat, gn, w):
    n, d = t.shape
    nt = n // TM - 1
    wd = nat.shape[1]
    lat = lambda x: pl.BlockSpec((TM, x), lambda i: (i + 1, 0))
    return pl.pallas_call(
        _out_cd_kernel,
        grid=(nt,),
        in_specs=[lat(d), pl.BlockSpec((1, N_MOD, d), lambda i: (1, 0, 0)),
                  lat(wd), lat(wd), lat(wd), pl.BlockSpec((TM, wd), lambda i: (i, 0)),
                  _const_spec(gn.shape), _const_spec(w.shape)],
        out_specs=pl.BlockSpec((TM, d), lambda i: (i, 0)),
        out_shape=jax.ShapeDtypeStruct((nt * TM, d), F32),
        name="out_cd",
        compiler_params=_cparams(("arbitrary",)),
    )(t, mod, o_f, o_b, hg, nat, gn, w)


def _rope_tables(n_lat, dim, lane0):
    rows = n_lat // GRID_W
    dh = dim // 2
    q = dim // 4
    sign = jnp.where(jnp.arange(dh) < q, -1.0, 1.0)

    def one(n_pos):
        inv = ROPE_THETA ** (-jnp.arange(0, dh, 2, dtype=F32) / dh)
        ang = jnp.arange(n_pos, dtype=jnp.int32).astype(F32)[:, None] * inv[None, :]
        ang = jnp.concatenate([ang, ang], axis=-1)
        return jnp.cos(ang), jnp.sin(ang) * sign

    (cos_r, sin_r), (cos_c, sin_c) = one(rows), one(GRID_W)
    grid = lambda r, c: jnp.concatenate([jnp.broadcast_to(r[:, None], (rows, GRID_W, dh)),
                                         jnp.broadcast_to(c[None], (rows, GRID_W, dh))], axis=-1).reshape(n_lat, dim)
    cos, sin = grid(cos_r, cos_c), grid(sin_r, sin_c)
    if dim == HEAD_DIM:
        cos, sin = jnp.concatenate([cos, cos], -1), jnp.concatenate([sin, sin], -1)
    else:
        pad = ((0, 0), (lane0, LANES - lane0 - dim))
        cos = jnp.pad(cos, pad, constant_values=1.0)
        sin = jnp.pad(sin, pad)
    cos = jnp.concatenate([jnp.ones((TM, LANES), F32), cos], axis=0)
    sin = jnp.concatenate([jnp.zeros((TM, LANES), F32), sin], axis=0)
    return cos, sin


def _pad_lanes(x, lo, width=LANES):
    return jnp.pad(x, [(0, 0)] * (x.ndim - 1) + [(lo, width - lo - x.shape[-1])])


def _ab_weights(w_in, a_qn, a_kn, b_qn, b_kn, b_kvn, wukv):
    d = w_in.shape[0]
    grp = A_HEADS // A_KV_HEADS
    o = 0
    aq = w_in[:, o:o + A_HEADS * HEAD_DIM].reshape(d, A_HEADS, HEAD_DIM); o += A_HEADS * HEAD_DIM
    ak = w_in[:, o:o + A_KV_HEADS * HEAD_DIM]; o += A_KV_HEADS * HEAD_DIM
    av = w_in[:, o:o + A_KV_HEADS * HEAD_DIM]; o += A_KV_HEADS * HEAD_DIM
    bq = w_in[:, o:o + B_HEADS * B_QK].reshape(d, B_HEADS, B_QK); o += B_HEADS * B_QK
    bkv = w_in[:, o:o + B_KV_RANK]; o += B_KV_RANK
    bkr = w_in[:, o:o + B_ROPE]
    aq_slabs = jnp.concatenate([_pad_lanes(aq[:, h], (h // grp) * HEAD_DIM) for h in range(A_HEADS)], axis=1)
    bq_slabs = _pad_lanes(bq, 0).reshape(d, B_HEADS * LANES)
    wp = jnp.concatenate([aq_slabs, ak, av, bq_slabs, bkv, _pad_lanes(bkr, B_NOPE)], axis=1).astype(BF16)
    gq_a = jnp.concatenate([_pad_lanes(a_qn[None], (h // grp) * HEAD_DIM) for h in range(A_HEADS)], axis=1)
    gk_a = jnp.concatenate([a_kn, a_kn])[None]
    gq_b = _pad_lanes(b_qn[None], 0)
    gk_b = _pad_lanes(b_kn[None], 0)
    kvw = wukv.reshape(B_KV_RANK, B_HEADS, B_NOPE + B_VDIM)
    wk = _pad_lanes(kvw[:, :, :B_NOPE], 0).reshape(B_KV_RANK, B_HEADS * LANES)
    wv = kvw[:, :, B_NOPE:].reshape(B_KV_RANK, B_HEADS * B_VDIM)
    wukv_p = jnp.concatenate([wk, wv], axis=1).astype(BF16)
    return wp, wukv_p, gq_a, gk_a, gq_b, gk_b, b_kvn[None]


def kernel(x, c, ctx, c_ctx, norm_g, ada_w, ada_b, ffn_w1, ffn_w3, ffn_w2, ab_w_in, ab_a_qn, ab_a_kn, ab_b_qn,
           ab_b_kn, ab_b_kvn, ab_b_wukv, ab_w_out, cd_w_in, hgrn_lb, cd_c_gn, cd_d_qn, cd_d_kn, cd_d_rpb, cd_w_out):
    batch, seq, d = x.shape
    depth = norm_g.shape[0]
    assert batch == 1 and ctx.shape[1] == TM and seq % TM == 0 and depth == 2

    cc = jnp.zeros((8, d), F32).at[0].set(c_ctx).at[1].set(c[0])
    mods = _ada(cc, ada_w, ada_b)[:, :2].reshape(depth, 2, N_MOD, d)
    bf = lambda a: a.astype(BF16)

    def ffn(tt, l, j, latent_only=False):
        return _ffn(tt, mods[l], norm_g[l, 2 * j:2 * j + 1], bf(ffn_w1[l, j]), bf(ffn_w3[l, j]),
                    bf(ffn_w2[l, j]), 6 * j, latent_only)

    t = _ffn_first(ctx[0], x[0], mods[0], norm_g[0, 0:1], bf(ffn_w1[0, 0]), bf(ffn_w3[0, 0]), bf(ffn_w2[0, 0]), 0)
    wp, wukv_p, gq_a, gk_a, gq_b, gk_b, gkv = _ab_weights(ab_w_in[0], ab_a_qn[0], ab_a_kn[0], ab_b_qn[0],
                                                         ab_b_kn[0], ab_b_kvn[0], ab_b_wukv[0])
    cos_a, sin_a = _rope_tables(seq, HEAD_DIM, 0)
    cos_b, sin_b = _rope_tables(seq, B_ROPE, B_NOPE)
    qt, k, vta, vtb = _proj_ab(t, mods[0], norm_g[0, 1:2], wp, wukv_p, gq_a, gk_a, gq_b, gk_b, gkv,
                               cos_a, sin_a, cos_b, sin_b)
    oa = _attn_a(qt, k, vta)
    ob = _attn_b(qt, k, vtb)
    t = _out_ab(t, mods[0], oa, ob, bf(ab_w_out[0]))
    t = ffn(t, 0, 1)

    t = ffn(t, 1, 0)
    gq = jnp.concatenate([cd_d_qn[0], cd_d_qn[0]])[None]
    gk = jnp.concatenate([cd_d_kn[0], cd_d_kn[0]])[None]
    gn = jnp.concatenate([cd_c_gn[0], cd_c_gn[0]])[None]
    hq, gf, gb, hv, hg, dq, dk, dv = _proj_cd(t, mods[1], norm_g[1, 1:2], bf(cd_w_in[0]), hgrn_lb, gq, gk, 1)
    o_f, o_b = _hgrn(hq, gf, gb, hv)
    nat = _natten(dq, dk, dv, _natten_bias(cd_d_rpb[0]))
    xl = _out_cd(t, mods[1], o_f, o_b, hg, nat, gn, bf(cd_w_out[0]))
    xl = ffn(xl, 1, 1, latent_only=True)
    return xl[None]
```

```python
import functools
import math

import numpy as np
import jax
import jax.numpy as jnp
from jax import lax
from jax.experimental import pallas as pl
from jax.experimental.pallas import tpu as pltpu

F32 = jnp.float32
BF16 = jnp.bfloat16

GRID_W = 64
HEAD_DIM = 64
EPS = 1e-6
ROPE_THETA = 10000.0
N_MOD = 9
A_HEADS, A_KV_HEADS = 8, 2
B_HEADS, B_NOPE, B_ROPE, B_VDIM, B_KV_RANK = 8, 64, 32, 64, 256
B_QK = B_NOPE + B_ROPE
C_HEADS, C_DK, C_DV, C_CHUNK = 8, 64, 64, 64
D_HEADS, D_WIN_H, D_WIN_W = 8, 8, 16

LANES = 128
TM = 256
VMEM_LIMIT = 56 * 1024 * 1024
NEG = -1e30
LOG2E = math.log2(math.e)


def _cparams(sem):
    return pltpu.CompilerParams(dimension_semantics=sem, vmem_limit_bytes=VMEM_LIMIT)


def _const_spec(shape):
    nd = len(shape)
    return pl.BlockSpec(shape, lambda *_: (0,) * nd, pipeline_mode=pl.Buffered(1))


def _lane(shape):
    return lax.broadcasted_iota(jnp.int32, shape, len(shape) - 1)


def _rms_mod(x, g, shift, scale):
    y = x * lax.rsqrt(jnp.mean(x * x, axis=-1, keepdims=True) + EPS) * g
    return y * (1.0 + scale) + shift


def _half_norm(x, gain, n):
    left = _lane(x.shape) < n
    x2 = x * x
    sl = jnp.sum(jnp.where(left, x2, 0.0), axis=-1, keepdims=True)
    sr = jnp.sum(jnp.where(left, 0.0, x2), axis=-1, keepdims=True)
    r = jnp.where(left, lax.rsqrt(sl / n + EPS), lax.rsqrt(sr / n + EPS))
    return x * r * gain


def _slab_norm(x, gain, n):
    ss = jnp.sum(x * x, axis=-1, keepdims=True)
    return x * lax.rsqrt(ss / n + EPS) * gain


def _rope(x, cos, sin_signed, half):
    first = (_lane(x.shape) % (2 * half)) < half
    w = x.shape[-1]
    partner = jnp.where(first, pltpu.roll(x, w - half, 1), pltpu.roll(x, half, 1))
    return x * cos + partner * sin_signed


def _dot(a, b):
    return jnp.dot(a, b, preferred_element_type=F32)


def _dot_nt(a, b):
    return lax.dot_general(a, b, (((1,), (1,)), ((), ())), preferred_element_type=F32)


def _dot_tn(a, b):
    return lax.dot_general(a, b, (((0,), (0,)), ((), ())), preferred_element_type=F32)


def _ada_kernel(c_ref, w_ref, b_ref, o_ref):
    c = c_ref[...]
    a = c * jax.nn.sigmoid(c)
    o_ref[0] = jnp.dot(a, w_ref[0], preferred_element_type=F32, precision=lax.Precision.HIGHEST) + b_ref[0]


def _ada(cc, ada_w, ada_b):
    depth, d, n = ada_w.shape
    bn = n // 8
    return pl.pallas_call(
        _ada_kernel,
        grid=(depth, n // bn),
        in_specs=[pl.BlockSpec((8, d), lambda l, j: (0, 0)),
                  pl.BlockSpec((1, d, bn), lambda l, j: (l, 0, j)),
                  pl.BlockSpec((1, 1, bn), lambda l, j: (l, 0, j))],
        out_specs=pl.BlockSpec((1, 8, bn), lambda l, j: (l, 0, j)),
        out_shape=jax.ShapeDtypeStruct((depth, 8, n), F32),
        name="ada_mod",
        compiler_params=_cparams(("arbitrary", "arbitrary")),
    )(cc, ada_w, ada_b.reshape(depth, 1, n))


def _ffn_body(x, m, g, w1_ref, w3_ref, w2_ref, base):
    h = _rms_mod(x, g, m[base:base + 1], m[base + 1:base + 2]).astype(BF16)
    a = _dot(h, w1_ref[...])
    b = _dot(h, w3_ref[...])
    u = (a * jax.nn.sigmoid(a) * b).astype(BF16)
    return x + 0.5 * m[base + 2:base + 3] * _dot(u, w2_ref[...])


def _ffn_specs(w1, w2, l, j):
    sel = lambda a: pl.BlockSpec((None, None) + a.shape[2:], lambda *_: (l, j, 0, 0), pipeline_mode=pl.Buffered(1))
    return [sel(w1), sel(w1), sel(w2)]


def _tile_spec(width, first_tile=0):
    return pl.BlockSpec((TM, width), lambda i: (i + first_tile, 0))


def _mod_spec(d, latent_only=False):
    return pl.BlockSpec((1, N_MOD, d), (lambda i: (1, 0, 0)) if latent_only else (lambda i: (jnp.minimum(i, 1), 0, 0)))


VROWS = HEAD_DIM + 16


def _store_values_t(ref, n_heads, xt):
    for e in range(n_heads):
        r0 = e * VROWS
        ref[0, r0:r0 + HEAD_DIM, :] = xt[e * HEAD_DIM:(e + 1) * HEAD_DIM].astype(BF16)
        ref[0, r0 + HEAD_DIM:r0 + VROWS, :] = jnp.ones((VROWS - HEAD_DIM, xt.shape[1]), BF16)


def _rope_t(x, cos, sin_signed, half, lo, hi):
    blocks = [x[:lo]] if lo else []
    for r in range(lo, hi, 2 * half):
        blocks += [x[r + half:r + 2 * half], x[r:r + half]]
    if hi < x.shape[0]:
        blocks.append(x[hi:])
    return x * cos + jnp.concatenate(blocks, axis=0) * sin_signed

def _proj_ab_body(t, m, g, wk_ref, wqv_ref, wuk_ref, wuv_ref, gq_ref, gk_a_ref, gk_b_ref, gkv_ref, gkvt_ref,
                  cos_a_ref, sin_a_ref, cos_b_ref, sin_b_ref, cost_a_ref, sint_a_ref, cost_b_ref, sint_b_ref,
                  qt_ref, k_ref, vta_ref, vtb_ref):
    hf = _rms_mod(t, g, m[3:4], m[4:5])
    h = hf.astype(BF16)
    ht = hf.T.astype(BF16)
    pk = _dot(h, wk_ref[...])
    pt = _dot(wqv_ref[...], ht)
    nq = A_HEADS + B_HEADS
    for hd in range(nq):
        x = pt[hd * LANES:(hd + 1) * LANES]
        ss = jnp.sum(x * x, axis=0, keepdims=True)
        x = x * lax.rsqrt(ss / (HEAD_DIM if hd < A_HEADS else B_QK) + EPS) * gq_ref[hd * LANES:(hd + 1) * LANES, :]
        if hd < A_HEADS:
            x = _rope_t(x, cost_a_ref[...], sint_a_ref[...], HEAD_DIM // 4, 0, LANES)
        else:
            x = _rope_t(x, cost_b_ref[...], sint_b_ref[...], B_ROPE // 4, B_NOPE, B_QK)
        qt_ref[0, hd * LANES:(hd + 1) * LANES, :] = x.astype(BF16)
    _store_values_t(vta_ref, A_KV_HEADS, pt[nq * LANES:(nq + 1) * LANES])
    ct = pt[(nq + 1) * LANES:(nq + 1) * LANES + B_KV_RANK]
    cnt = (ct * lax.rsqrt(jnp.mean(ct * ct, axis=0, keepdims=True) + EPS) * gkvt_ref[...]).astype(BF16)
    _store_values_t(vtb_ref, B_HEADS, _dot(wuv_ref[...], cnt))
    cos_a, sin_a = cos_a_ref[...], sin_a_ref[...]
    cos_b, sin_b = cos_b_ref[...], sin_b_ref[...]
    x = _half_norm(pk[:, :LANES], gk_a_ref[...], HEAD_DIM)
    k_ref[:, B_HEADS * LANES:(B_HEADS + 1) * LANES] = _rope(x, cos_a, sin_a, HEAD_DIM // 4).astype(BF16)
    c = pk[:, LANES:LANES + B_KV_RANK]
    cn = (c * lax.rsqrt(jnp.mean(c * c, axis=-1, keepdims=True) + EPS) * gkv_ref[...]).astype(BF16)
    kn = _dot(cn, wuk_ref[...])
    kr = pk[:, LANES + B_KV_RANK:]
    for hd in range(B_HEADS):
        x = _slab_norm(kn[:, hd * LANES:(hd + 1) * LANES] + kr, gk_b_ref[...], B_QK)
        k_ref[:, hd * LANES:(hd + 1) * LANES] = _rope(x, cos_b, sin_b, B_ROPE // 4).astype(BF16)


def _first_kernel(c_ref, x_ref, m_ref, g_ref, w1_ref, w3_ref, w2_ref, *refs):
    t_ref = refs[-5]
    x = jnp.where(pl.program_id(0) == 0, c_ref[...], x_ref[...])
    m, g = m_ref[0], g_ref[...]
    t = _ffn_body(x, m, g[0:1], w1_ref, w3_ref, w2_ref, 0)
    t_ref[...] = t
    _proj_ab_body(t, m, g[1:2], *refs[:-5], *refs[-4:])


def _first(ctx, x, mod, g, w1, w3, w2, consts, tables, tables_t):
    seq, d = x.shape
    nt = seq // TM + 1
    n = nt * TM
    nq = (A_HEADS + B_HEADS) * LANES
    return pl.pallas_call(
        _first_kernel,
        grid=(nt,),
        in_specs=[pl.BlockSpec((TM, d), lambda i: (0, 0)),
                  pl.BlockSpec((TM, d), lambda i: (jnp.maximum(i - 1, 0), 0)),
                  _mod_spec(d), _const_spec(g.shape)] + _ffn_specs(w1, w2, 0, 0)
                 + [_const_spec(a.shape) for a in consts]
                 + [_tile_spec(LANES)] * len(tables)
                 + [pl.BlockSpec((LANES, TM), lambda i: (0, i))] * len(tables_t),
        out_specs=[_tile_spec(d),
                   pl.BlockSpec((1, nq, TM), lambda i: (i, 0, 0)),
                   _tile_spec((B_HEADS + 1) * LANES),
                   pl.BlockSpec((1, A_KV_HEADS * VROWS, TM), lambda i: (i, 0, 0)),
                   pl.BlockSpec((1, B_HEADS * VROWS, TM), lambda i: (i, 0, 0))],
        out_shape=[jax.ShapeDtypeStruct((n, d), F32),
                   jax.ShapeDtypeStruct((nt, nq, TM), BF16),
                   jax.ShapeDtypeStruct((n, (B_HEADS + 1) * LANES), BF16),
                   jax.ShapeDtypeStruct((nt, A_KV_HEADS * VROWS, TM), BF16),
                   jax.ShapeDtypeStruct((nt, B_HEADS * VROWS, TM), BF16)],
        name="ffn_proj_ab",
        compiler_params=_cparams(("arbitrary",)),
    )(ctx, x, mod, g, w1, w3, w2, *consts, *tables, *tables_t)


def _attn_kernel(qt_ref, k_ref, vt_ref, o_ref, s_ref, mx_ref, *, n_heads, k_slab, v_rows, n_tiles):
    heads = range(n_heads)
    is_ctx = pl.program_id(1) == 0
    n_pairs = jnp.where(is_ctx, 0, (n_tiles - 1) // 2)
    last = jnp.where(is_ctx, 0, n_tiles - 1)

    def scores(kt, slot):
        off = pl.multiple_of(kt * TM, TM)
        for hl in heads:
            r = _dot(k_ref[pl.ds(off, TM), k_slab[hl] * LANES:(k_slab[hl] + 1) * LANES],
                     qt_ref[0, hl * LANES:(hl + 1) * LANES, :])
            s_ref[slot, hl] = r
            mx_ref[slot, hl] = jnp.max(r, axis=0, keepdims=True)

    def consume(kt, slot, carry):
        m, acc = carry
        m_new = tuple(jnp.maximum(m[hl], mx_ref[slot, hl]) for hl in heads)
        alpha = tuple(jnp.exp2(m[hl] - m_new[hl]) for hl in heads)
        pm = tuple(jnp.exp2(s_ref[slot, hl] - m_new[hl]).astype(BF16) for hl in heads)
        acc = tuple(alpha[hl] * acc[hl] + _dot(vt_ref[kt, v_rows[hl] * VROWS:(v_rows[hl] + 1) * VROWS, :], pm[hl])
                    for hl in heads)
        return m_new, acc

    pairs_per_trip = max(u for u in (8, 4, 2, 1) if ((n_tiles - 1) // 2) % u == 0)

    def body(j, carry):
        for u in range(pairs_per_trip):
            t0 = 2 * (j * pairs_per_trip + u)
            scores(t0 + 1, 1)
            carry = consume(t0, 0, carry)
            scores(t0 + 2, 0)
            carry = consume(t0 + 1, 1, carry)
        return carry

    rep = lambda v: (v,) * n_heads
    init = (rep(jnp.full((1, TM), -jnp.inf, F32)), rep(jnp.zeros((VROWS, TM), F32)))
    scores(0, 0)
    _, acc = consume(last, 0, lax.fori_loop(0, n_pairs // pairs_per_trip, body, init))
    for pr in range(n_heads // 2):
        outs = [acc[2 * pr + e][:HEAD_DIM] / acc[2 * pr + e][HEAD_DIM:HEAD_DIM + 1] for e in range(2)]
        o_ref[:, pr * LANES:(pr + 1) * LANES] = jnp.concatenate(outs, axis=0).T.astype(BF16)


def _attn_scratch(n_heads):
    return [pltpu.VMEM((2, n_heads, TM, TM), F32), pltpu.VMEM((2, n_heads, 1, TM), F32)]


def _attn_a(qt, k, vta):
    nt = qt.shape[0]
    assert nt % 2 == 1
    n = nt * TM
    gh = A_HEADS // A_KV_HEADS
    kern = functools.partial(_attn_kernel, n_heads=gh, k_slab=(0,) * gh, v_rows=(0,) * gh, n_tiles=nt)
    return pl.pallas_call(
        kern,
        grid=(A_KV_HEADS, nt),
        in_specs=[pl.BlockSpec((1, gh * LANES, TM), lambda g, i: (i, g, 0)),
                  pl.BlockSpec((n, LANES), lambda g, i: (0, B_HEADS)),
                  pl.BlockSpec((nt, VROWS, TM), lambda g, i: (0, g, 0))],
        out_specs=pl.BlockSpec((TM, gh * HEAD_DIM), lambda g, i: (i, g)),
        out_shape=jax.ShapeDtypeStruct((n, A_HEADS * HEAD_DIM), BF16),
        scratch_shapes=_attn_scratch(gh),
        name="attn_gqa",
        compiler_params=_cparams(("arbitrary", "arbitrary")),
    )(qt, k, vta)


def _attn_b(qt, k, vtb):
    nt = qt.shape[0]
    n = nt * TM
    gh = 4
    hs = tuple(range(gh))
    kern = functools.partial(_attn_kernel, n_heads=gh, k_slab=hs, v_rows=hs, n_tiles=nt)
    return pl.pallas_call(
        kern,
        grid=(B_HEADS // gh, nt),
        in_specs=[pl.BlockSpec((1, gh * LANES, TM), lambda j, i: (i, A_HEADS // gh + j, 0)),
                  pl.BlockSpec((n, gh * LANES), lambda j, i: (0, j), pipeline_mode=pl.Buffered(1)),
                  pl.BlockSpec((nt, gh * VROWS, TM), lambda j, i: (0, j, 0), pipeline_mode=pl.Buffered(1))],
        out_specs=pl.BlockSpec((TM, gh * B_VDIM), lambda j, i: (i, j)),
        out_shape=jax.ShapeDtypeStruct((n, B_HEADS * B_VDIM), BF16),
        scratch_shapes=_attn_scratch(gh),
        name="attn_mla",
        compiler_params=_cparams(("arbitrary", "arbitrary")),
    )(qt, k, vtb)


def _out_ab_kernel(t_ref, m_ref, g_ref, oa_ref, ob_ref, w_ref, w1_ref, w3_ref, w2_ref, o_ref):
    m = m_ref[0]
    half = oa_ref.shape[1]
    y = _dot(oa_ref[...], w_ref[:half, :]) + _dot(ob_ref[...], w_ref[half:, :])
    x = t_ref[...] + m[5:6] * y
    o_ref[...] = _ffn_body(x, m, g_ref[2:3], w1_ref, w3_ref, w2_ref, 6)


def _out_ab(t, mod, g, oa, ob, w, w1, w3, w2):
    n, d = t.shape
    return pl.pallas_call(
        _out_ab_kernel,
        grid=(n // TM,),
        in_specs=[_tile_spec(d), _mod_spec(d), _const_spec(g.shape), _tile_spec(oa.shape[1]),
                  _tile_spec(ob.shape[1]), _const_spec(w.shape)] + _ffn_specs(w1, w2, 0, 1),
        out_specs=_tile_spec(d),
        out_shape=jax.ShapeDtypeStruct((n, d), F32),
        name="out_ab_ffn",
        compiler_params=_cparams(("arbitrary",)),
    )(t, mod, g, oa, ob, w, w1, w3, w2)


def _proj_cd_kernel(t_ref, m_ref, g_ref, w1_ref, w3_ref, w2_ref, w_ref, lbp_ref, gq_ref, gk_ref,
                    t_out_ref, hq_ref, gf_ref, gb_ref, hv_ref, hg_ref, dq_ref, dk_ref, dv_ref, *, layer):
    m = m_ref[0]
    t = _ffn_body(t_ref[...], m, g_ref[0:1], w1_ref, w3_ref, w2_ref, 0)
    t_out_ref[...] = t
    h = _rms_mod(t, g_ref[1:2], m[3:4], m[4:5]).astype(BF16)
    p = _dot(h, w_ref[...])
    w = C_HEADS * C_DK
    x = p[:, 0:w]
    hq_ref[...] = x * jax.nn.sigmoid(x)
    lbp = lbp_ref[...]
    e = jnp.exp(lbp - jnp.max(lbp, axis=0, keepdims=True))
    sm = e / jnp.sum(e, axis=0, keepdims=True)
    lb = jnp.sum(sm[1:layer + 1], axis=0)
    for dr, ref in ((0, gf_ref), (1, gb_ref)):
        z = p[:, (1 + dr) * w:(2 + dr) * w]
        lbd = lb[dr:dr + 1]
        ref[...] = jnp.log(lbd + (1.0 - lbd) * jax.nn.sigmoid(z))
    hv_ref[...] = p[:, 3 * w:4 * w].astype(BF16)
    hg_ref[...] = p[:, 4 * w:5 * w]
    for s in range(w // LANES):
        sl = slice(s * LANES, (s + 1) * LANES)
        q = _half_norm(p[:, 5 * w + s * LANES:5 * w + (s + 1) * LANES], gq_ref[...], HEAD_DIM)
        dq_ref[:, sl] = (q * (HEAD_DIM ** -0.5 * LOG2E)).astype(BF16)
        dk_ref[:, sl] = _half_norm(p[:, 6 * w + s * LANES:6 * w + (s + 1) * LANES], gk_ref[...], HEAD_DIM).astype(BF16)
    dv_ref[...] = p[:, 7 * w:8 * w].astype(BF16)


def _proj_cd(t, mod, g, w1, w3, w2, w_in, hgrn_lb, gq, gk, layer):
    n, d = t.shape
    w = C_HEADS * C_DK
    shp = lambda dt: jax.ShapeDtypeStruct((n, w), dt)
    return pl.pallas_call(
        functools.partial(_proj_cd_kernel, layer=layer),
        grid=(n // TM,),
        in_specs=[_tile_spec(d), _mod_spec(d), _const_spec(g.shape)] + _ffn_specs(w1, w2, layer, 0)
                 + [_const_spec(w_in.shape), _const_spec(hgrn_lb.shape), _const_spec(gq.shape), _const_spec(gk.shape)],
        out_specs=[_tile_spec(d)] + [_tile_spec(w)] * 8,
        out_shape=[jax.ShapeDtypeStruct((n, d), F32),
                   shp(F32), shp(F32), shp(F32), shp(BF16), shp(F32), shp(BF16), shp(BF16), shp(BF16)],
        name="ffn_proj_cd",
        compiler_params=_cparams(("arbitrary",)),
    )(t, mod, g, w1, w3, w2, w_in, hgrn_lb, gq, gk)


N_LEVELS = int(math.log2(C_CHUNK))


def _hgrn_tables():
    t = np.arange(C_CHUNK)
    return (t[None, :] <= t[:, None]).astype(np.float32)


def _level_boundary(gi, h, reverse):
    c = gi.shape[0]
    if h >= 4:
        blocks = []
        for s in range(0, c, 2 * h):
            b = s + (h if reverse else h - 1)
            blocks.append(jnp.broadcast_to(gi[b:b + 1], (2 * h, gi.shape[1])))
        return jnp.concatenate(blocks, axis=0)
    pos = lax.broadcasted_iota(jnp.int32, gi.shape, 0) % (2 * h)
    gb = gi
    for p in range(2 * h):
        d = (h if reverse else h - 1) - p
        if d != 0:
            gb = jnp.where(pos == p, pltpu.roll(gi, (c - d) % c, 0), gb)
    return gb


def _level_masks(reverse):
    c = C_CHUNK
    ti = lax.broadcasted_iota(jnp.int32, (2 * c, c), 0) % c
    si = lax.broadcasted_iota(jnp.int32, (2 * c, c), 1)
    tq, sk = (si, ti) if reverse else (ti, si)
    masks = []
    for lv in range(N_LEVELS):
        h = c >> (lv + 1)
        sh = N_LEVELS - lv
        masks.append(((ti >> sh) == (si >> sh)) & ((tq & h) != 0) & ((sk & h) == 0))
    masks.append(ti == si)
    return masks


def _hgrn_chunks(jobs, st_ref):
    c = C_CHUNK
    gis = []
    for q, g, v, tri, masks, d, reverse in jobs:
        ghi = g.astype(BF16)
        glo = (g - ghi.astype(F32)).astype(BF16)
        gis.append(_dot(tri, ghi) + _dot(tri, glo))
    ops = []
    for (q, g, v, tri, masks, d, reverse), gi in zip(jobs, gis):
        kk = 1.0 - jnp.exp(g)
        glast = gi[(c - 1):c] if not reverse else gi[0:1]
        qs = (q * jnp.exp(gi)).astype(BF16)
        ks = (kk * jnp.exp(glast - gi)).astype(BF16)
        eqs, eks = [], []
        for lv in range(N_LEVELS):
            e = jnp.exp(-jnp.abs(gi - _level_boundary(gi, c >> (lv + 1), reverse)))
            eqs.append((e * q).astype(BF16))
            eks.append((e * kk).astype(BF16))
        eqs.append(q.astype(BF16))
        eks.append(kk.astype(BF16))
        ops.append((eqs, eks, qs, ks, jnp.exp(glast)))
    left = _lane((c, LANES)) < C_DK
    blockdiag = (lax.broadcasted_iota(jnp.int32, (LANES, LANES), 0) < C_DV) == (_lane((LANES, LANES)) < C_DK)
    outs = [[] for _ in jobs]
    for pr in range(C_HEADS // 2):
        sl = slice(pr * LANES, (pr + 1) * LANES)
        a2s = [jnp.zeros((2 * c, c), F32) for _ in jobs]
        for lv in range(N_LEVELS + 1):
            for n, job in enumerate(jobs):
                eq = ops[n][0][lv][:, sl]
                eq2 = jnp.concatenate([jnp.where(left, eq, 0), jnp.where(left, 0, eq)], axis=0)
                a = _dot_nt(eq2, ops[n][1][lv][:, sl])
                a2s[n] = jnp.where(job[4][lv], a, a2s[n])
        r2s = [_dot(a2s[n].astype(BF16), job[2][:, sl]) for n, job in enumerate(jobs)]
        upds = [_dot_tn(job[2][:, sl], ops[n][3][:, sl]) for n, job in enumerate(jobs)]
        for n, job in enumerate(jobs):
            d = job[5]
            st = st_ref[d, pr]
            outs[n].append(jnp.where(left, r2s[n][:c], r2s[n][c:]) + _dot_nt(ops[n][2][:, sl], st.astype(BF16)))
            st_ref[d, pr] = st * ops[n][4][:, sl] + jnp.where(blockdiag, upds[n], 0.0)
    return [jnp.concatenate(o, axis=1) for o in outs]


HGRN_CHUNKS_PER_PASS = 2


def _hgrn_kernel(qf_ref, gf_ref, vf_ref, qb_ref, gb_ref, vb_ref, wf_ref, wb_ref, of_ref, ob_ref, st_ref):
    @pl.when(pl.program_id(0) == 0)
    def _():
        st_ref[...] = jnp.zeros(st_ref.shape, F32)

    c = C_CHUNK
    mf, mb = _level_masks(False), _level_masks(True)
    tf, tb = wf_ref[...], wb_ref[...]
    for j0 in range(0, TM // c, HGRN_CHUNKS_PER_PASS):
        js = range(j0, j0 + HGRN_CHUNKS_PER_PASS)
        fs = [slice(j * c, (j + 1) * c) for j in js]
        bs = [slice(TM - (j + 1) * c, TM - j * c) for j in js]
        jobs = ([(qf_ref[s, :], gf_ref[s, :], vf_ref[s, :], tf, mf, 0, False) for s in fs]
                + [(qb_ref[s, :], gb_ref[s, :], vb_ref[s, :], tb, mb, 1, True) for s in bs])
        outs = _hgrn_chunks(jobs, st_ref)
        for s, o in zip(fs, outs[:len(fs)]):
            of_ref[s, :] = o
        for s, o in zip(bs, outs[len(fs):]):
            ob_ref[s, :] = o


def _hgrn(hq, gf, gb, hv):
    n, w = hq.shape
    nt = n // TM
    wf = _hgrn_tables()
    wb = wf.reshape(-1, C_CHUNK, C_CHUNK)[:, ::-1, ::-1].reshape(wf.shape)
    fwd = pl.BlockSpec((TM, w), lambda i: (i, 0))
    bwd = pl.BlockSpec((TM, w), lambda i: (jnp.where(i == 0, 0, nt - i), 0))
    return pl.pallas_call(
        _hgrn_kernel,
        grid=(nt,),
        in_specs=[fwd, fwd, fwd, bwd, bwd, bwd, _const_spec(wf.shape), _const_spec(wb.shape)],
        out_specs=[fwd, bwd],
        out_shape=[jax.ShapeDtypeStruct((n, w), F32)] * 2,
        scratch_shapes=[pltpu.VMEM((2, C_HEADS // 2, LANES, LANES), F32)],
        name="hgrn2",
        compiler_params=_cparams(("arbitrary",)),
    )(hq, gf, hv, hq, gb, hv, jnp.asarray(wf, BF16), jnp.asarray(wb, BF16))


NAT_ROWS = TM // GRID_W


def _natten_kernel(q_ref, k_ref, v_ref, bias_ref, o_ref, *, rows):
    i = pl.program_id(1)
    kc = k_ref[0:TM, :]
    vc = v_ref[0:TM, :]
    left_q = _lane((GRID_W, LANES)) < HEAD_DIM
    q2s, offs, sws = [], [], []
    for j in range(NAT_ROWS):
        r = i * NAT_ROWS + j
        rs = jnp.clip(r - D_WIN_H // 2, 0, rows - D_WIN_H)
        offs.append(pl.multiple_of(TM + rs * GRID_W, GRID_W))
        q = q_ref[j * GRID_W:(j + 1) * GRID_W, :]
        q2s.append(jnp.concatenate([jnp.where(left_q, q, 0), jnp.where(left_q, 0, q)], axis=0))
        kw = k_ref[pl.ds(offs[j], D_WIN_H * GRID_W), :]
        a0 = rs - r + (D_WIN_H - 1)
        bias = jnp.concatenate([bias_ref[0, a0 + 2 * u] for u in range(D_WIN_H // 2)], axis=1)
        sws.append(_dot_nt(q2s[j], kw) + bias)
    sc = _dot_nt(jnp.concatenate(q2s, axis=0), kc)
    pws, pcs, ls = [], [], []
    for j in range(NAT_ROWS):
        scj = sc[j * 2 * GRID_W:(j + 1) * 2 * GRID_W]
        m = jnp.maximum(jnp.max(sws[j], axis=-1, keepdims=True), jnp.max(scj, axis=-1, keepdims=True))
        pw = jnp.exp2(sws[j] - m)
        pc = jnp.exp2(scj - m)
        ls.append(jnp.sum(pw, axis=-1, keepdims=True) + jnp.sum(pc, axis=-1, keepdims=True))
        pws.append(pw.astype(BF16))
        pcs.append(pc.astype(BF16))
    oc = _dot(jnp.concatenate(pcs, axis=0), vc)
    for j in range(NAT_ROWS):
        vw = v_ref[pl.ds(offs[j], D_WIN_H * GRID_W), :]
        o2 = (_dot(pws[j], vw) + oc[j * 2 * GRID_W:(j + 1) * 2 * GRID_W]) / ls[j]
        o_ref[j * GRID_W:(j + 1) * GRID_W, :] = jnp.where(left_q, o2[:GRID_W], o2[GRID_W:]).astype(BF16)


def _natten(dq, dk, dv, bias):
    n, w = dq.shape
    nt = n // TM - 1
    rows = nt * NAT_ROWS
    return pl.pallas_call(
        functools.partial(_natten_kernel, rows=rows),
        grid=(w // LANES, nt),
        in_specs=[pl.BlockSpec((TM, LANES), lambda p, i: (i + 1, p)),
                  pl.BlockSpec((n, LANES), lambda p, i: (0, p)),
                  pl.BlockSpec((n, LANES), lambda p, i: (0, p)),
                  pl.BlockSpec((1,) + bias.shape[1:], lambda p, i: (p, 0, 0, 0))],
        out_specs=pl.BlockSpec((TM, LANES), lambda p, i: (i, p)),
        out_shape=jax.ShapeDtypeStruct((nt * TM, w), BF16),
        name="natten",
        compiler_params=_cparams(("arbitrary", "arbitrary")),
    )(dq, dk, dv, bias)


def _natten_bias(rpb):
    cols = np.arange(GRID_W)
    start = np.clip(cols - D_WIN_W // 2, 0, GRID_W - D_WIN_W)
    kc = cols[None, :]
    inside = (kc >= start[:, None]) & (kc < start[:, None] + D_WIN_W)
    rel = np.clip(kc - cols[:, None] + (D_WIN_W - 1), 0, 2 * D_WIN_W - 2)
    tab = jnp.where(inside[None, None], rpb[:, :, rel] * LOG2E, NEG)
    na = tab.shape[1]
    t = jnp.transpose(tab.reshape(D_HEADS // 2, 2, na, GRID_W, GRID_W), (0, 2, 1, 3, 4))
    t = t.reshape(D_HEADS // 2, na, 2 * GRID_W, GRID_W)
    return jnp.concatenate([t[:, :-1], t[:, 1:]], axis=-1)


def _out_cd_kernel(t_ref, m_ref, g_ref, of_ref, ob_ref, hg_ref, nat_ref, gn_ref, w_ref, w1_ref, w3_ref, w2_ref, o_ref):
    m = m_ref[0]
    half = nat_ref.shape[1]
    o = of_ref[...] + ob_ref[...]
    gate = hg_ref[...]
    gate = gate * jax.nn.sigmoid(gate)
    parts = []
    for s in range(half // LANES):
        sl = slice(s * LANES, (s + 1) * LANES)
        parts.append((_half_norm(o[:, sl], gn_ref[...], C_DV) * gate[:, sl]).astype(BF16))
    y = _dot(jnp.concatenate(parts, axis=1), w_ref[:half, :]) + _dot(nat_ref[...], w_ref[half:, :])
    x = t_ref[...] + m[5:6] * y
    o_ref[...] = _ffn_body(x, m, g_ref[2:3], w1_ref, w3_ref, w2_ref, 6)


def _out_cd(t, mod, g, o_f, o_b, hg, nat, gn, w, w1, w3, w2, layer):
    n, d = t.shape
    nt = n // TM - 1
    wd = nat.shape[1]
    return pl.pallas_call(
        _out_cd_kernel,
        grid=(nt,),
        in_specs=[_tile_spec(d, 1), _mod_spec(d, latent_only=True), _const_spec(g.shape),
                  _tile_spec(wd, 1), _tile_spec(wd, 1), _tile_spec(wd, 1), _tile_spec(wd),
                  _const_spec(gn.shape), _const_spec(w.shape)] + _ffn_specs(w1, w2, layer, 1),
        out_specs=_tile_spec(d),
        out_shape=jax.ShapeDtypeStruct((nt * TM, d), F32),
        name="out_cd_ffn",
        compiler_params=_cparams(("arbitrary",)),
    )(t, mod, g, o_f, o_b, hg, nat, gn, w, w1, w3, w2)


def _rope_tables(n_lat, dim, lane0):
    rows = n_lat // GRID_W
    dh = dim // 2
    q = dim // 4
    sign = jnp.where(jnp.arange(dh) < q, -1.0, 1.0)

    def one(n_pos):
        inv = ROPE_THETA ** (-jnp.arange(0, dh, 2, dtype=F32) / dh)
        ang = jnp.arange(n_pos, dtype=jnp.int32).astype(F32)[:, None] * inv[None, :]
        ang = jnp.concatenate([ang, ang], axis=-1)
        return jnp.cos(ang), jnp.sin(ang) * sign

    (cos_r, sin_r), (cos_c, sin_c) = one(rows), one(GRID_W)
    grid = lambda r, c: jnp.concatenate([jnp.broadcast_to(r[:, None], (rows, GRID_W, dh)),
                                         jnp.broadcast_to(c[None], (rows, GRID_W, dh))], axis=-1).reshape(n_lat, dim)
    cos, sin = grid(cos_r, cos_c), grid(sin_r, sin_c)
    if dim == HEAD_DIM:
        cos, sin = jnp.concatenate([cos, cos], -1), jnp.concatenate([sin, sin], -1)
    else:
        pad = ((0, 0), (lane0, LANES - lane0 - dim))
        cos = jnp.pad(cos, pad, constant_values=1.0)
        sin = jnp.pad(sin, pad)
    cos = jnp.concatenate([jnp.ones((TM, LANES), F32), cos], axis=0)
    sin = jnp.concatenate([jnp.zeros((TM, LANES), F32), sin], axis=0)
    return cos, sin


def _pad_lanes(x, lo, width=LANES):
    return jnp.pad(x, [(0, 0)] * (x.ndim - 1) + [(lo, width - lo - x.shape[-1])])


def _ab_weights(w_in, a_qn, a_kn, b_qn, b_kn, b_kvn, wukv):
    d = w_in.shape[0]
    grp = A_HEADS // A_KV_HEADS
    o = 0
    aq = w_in[:, o:o + A_HEADS * HEAD_DIM].reshape(d, A_HEADS, HEAD_DIM); o += A_HEADS * HEAD_DIM
    ak = w_in[:, o:o + A_KV_HEADS * HEAD_DIM]; o += A_KV_HEADS * HEAD_DIM
    av = w_in[:, o:o + A_KV_HEADS * HEAD_DIM]; o += A_KV_HEADS * HEAD_DIM
    bq = w_in[:, o:o + B_HEADS * B_QK].reshape(d, B_HEADS, B_QK); o += B_HEADS * B_QK
    bkv = w_in[:, o:o + B_KV_RANK]; o += B_KV_RANK
    bkr = w_in[:, o:o + B_ROPE]
    aq_slabs = jnp.concatenate([_pad_lanes(aq[:, h], (h // grp) * HEAD_DIM) for h in range(A_HEADS)], axis=1)
    bq_slabs = _pad_lanes(bq, 0).reshape(d, B_HEADS * LANES)
    wk = jnp.concatenate([ak, bkv, _pad_lanes(bkr, B_NOPE)], axis=1).astype(BF16)
    wqv = jnp.concatenate([aq_slabs, bq_slabs, av, bkv], axis=1).T.astype(BF16)
    kvw = wukv.reshape(B_KV_RANK, B_HEADS, B_NOPE + B_VDIM)
    wuk = _pad_lanes(kvw[:, :, :B_NOPE], 0).reshape(B_KV_RANK, B_HEADS * LANES).astype(BF16)
    wuv = kvw[:, :, B_NOPE:].reshape(B_KV_RANK, B_HEADS * B_VDIM).T.astype(BF16)
    gq_a = jnp.concatenate([_pad_lanes(a_qn, (h // grp) * HEAD_DIM) for h in range(A_HEADS)])
    gq_b = jnp.tile(_pad_lanes(b_qn, 0), B_HEADS)
    gq = jnp.concatenate([gq_a * (HEAD_DIM ** -0.5 * LOG2E), gq_b * (B_QK ** -0.5 * LOG2E)])
    gq = jnp.broadcast_to(gq[:, None], (gq.shape[0], TM))
    gk_a = jnp.concatenate([a_kn, a_kn])[None]
    gk_b = _pad_lanes(b_kn[None], 0)
    gkvt = jnp.broadcast_to(b_kvn[:, None], (B_KV_RANK, TM))
    return [wk, wqv, wuk, wuv, gq, gk_a, gk_b, b_kvn[None], gkvt]


def kernel(x, c, ctx, c_ctx, norm_g, ada_w, ada_b, ffn_w1, ffn_w3, ffn_w2, ab_w_in, ab_a_qn, ab_a_kn, ab_b_qn,
           ab_b_kn, ab_b_kvn, ab_b_wukv, ab_w_out, cd_w_in, hgrn_lb, cd_c_gn, cd_d_qn, cd_d_kn, cd_d_rpb, cd_w_out):
    batch, seq, d = x.shape
    depth = norm_g.shape[0]
    assert batch == 1 and ctx.shape[1] == TM and seq % TM == 0 and depth == 2

    cc = jnp.zeros((8, d), F32).at[0].set(c_ctx).at[1].set(c[0])
    mods = _ada(cc, ada_w, ada_b)[:, :2].reshape(depth, 2, N_MOD, d)
    bf = lambda a: a.astype(BF16)
    w1, w3, w2 = bf(ffn_w1), bf(ffn_w3), bf(ffn_w2)

    consts = _ab_weights(ab_w_in[0], ab_a_qn[0], ab_a_kn[0], ab_b_qn[0], ab_b_kn[0], ab_b_kvn[0], ab_b_wukv[0])
    tables = _rope_tables(seq, HEAD_DIM, 0) + _rope_tables(seq, B_ROPE, B_NOPE)
    t, qt, k, vta, vtb = _first(ctx[0], x[0], mods[0], norm_g[0], w1, w3, w2, consts, tables,
                                tuple(a.T for a in tables))
    oa = _attn_a(qt, k, vta)
    ob = _attn_b(qt, k, vtb)
    t = _out_ab(t, mods[0], norm_g[0], oa, ob, bf(ab_w_out[0]), w1, w3, w2)

    gq = jnp.concatenate([cd_d_qn[0], cd_d_qn[0]])[None]
    gk = jnp.concatenate([cd_d_kn[0], cd_d_kn[0]])[None]
    gn = jnp.concatenate([cd_c_gn[0], cd_c_gn[0]])[None]
    t, hq, gf, gb, hv, hg, dq, dk, dv = _proj_cd(t, mods[1], norm_g[1], w1, w3, w2, bf(cd_w_in[0]), hgrn_lb, gq, gk, 1)
    o_f, o_b = _hgrn(hq, gf, gb, hv)
    nat = _natten(dq, dk, dv, _natten_bias(cd_d_rpb[0]))
    xl = _out_cd(t, mods[1], norm_g[1], o_f, o_b, hg, nat, gn, bf(cd_w_out[0]), w1, w3, w2, 1)
    return xl[None]
```

```python
import functools
import math

import numpy as np
import jax
import jax.numpy as jnp
from jax import lax
from jax.experimental import pallas as pl
from jax.experimental.pallas import tpu as pltpu

F32 = jnp.float32
BF16 = jnp.bfloat16

GRID_W = 64
HEAD_DIM = 64
EPS = 1e-6
ROPE_THETA = 10000.0
N_MOD = 9
A_HEADS, A_KV_HEADS = 8, 2
B_HEADS, B_NOPE, B_ROPE, B_VDIM, B_KV_RANK = 8, 64, 32, 64, 256
B_QK = B_NOPE + B_ROPE
C_HEADS, C_DK, C_DV, C_CHUNK = 8, 64, 64, 64
D_HEADS, D_WIN_H, D_WIN_W = 8, 8, 16

LANES = 128
TM = 256
VMEM_LIMIT = 56 * 1024 * 1024
NEG = -1e30
LOG2E = math.log2(math.e)


def _cparams(sem):
    return pltpu.CompilerParams(dimension_semantics=sem, vmem_limit_bytes=VMEM_LIMIT)


def _const_spec(shape):
    nd = len(shape)
    return pl.BlockSpec(shape, lambda *_: (0,) * nd, pipeline_mode=pl.Buffered(1))


def _lane(shape):
    return lax.broadcasted_iota(jnp.int32, shape, len(shape) - 1)


def _rms_mod(x, g, shift, scale):
    y = x * lax.rsqrt(jnp.mean(x * x, axis=-1, keepdims=True) + EPS) * g
    return y * (1.0 + scale) + shift


def _half_norm(x, gain, n):
    left = _lane(x.shape) < n
    x2 = x * x
    sl = jnp.sum(jnp.where(left, x2, 0.0), axis=-1, keepdims=True)
    sr = jnp.sum(jnp.where(left, 0.0, x2), axis=-1, keepdims=True)
    r = jnp.where(left, lax.rsqrt(sl / n + EPS), lax.rsqrt(sr / n + EPS))
    return x * r * gain


def _slab_norm(x, gain, n):
    ss = jnp.sum(x * x, axis=-1, keepdims=True)
    return x * lax.rsqrt(ss / n + EPS) * gain


def _rope(x, cos, sin_signed, half):
    first = (_lane(x.shape) % (2 * half)) < half
    w = x.shape[-1]
    partner = jnp.where(first, pltpu.roll(x, w - half, 1), pltpu.roll(x, half, 1))
    return x * cos + partner * sin_signed


def _dot(a, b):
    return jnp.dot(a, b, preferred_element_type=F32)


def _dot_nt(a, b):
    return lax.dot_general(a, b, (((1,), (1,)), ((), ())), preferred_element_type=F32)


def _dot_tn(a, b):
    return lax.dot_general(a, b, (((0,), (0,)), ((), ())), preferred_element_type=F32)


def _ada_kernel(c_ref, w_ref, b_ref, o_ref):
    c = c_ref[...]
    a = c * jax.nn.sigmoid(c)
    o_ref[0] = jnp.dot(a, w_ref[0], preferred_element_type=F32, precision=lax.Precision.HIGHEST) + b_ref[0]


def _ada(cc, ada_w, ada_b):
    depth, d, n = ada_w.shape
    bn = n // 4
    return pl.pallas_call(
        _ada_kernel,
        grid=(depth, n // bn),
        in_specs=[pl.BlockSpec((8, d), lambda l, j: (0, 0)),
                  pl.BlockSpec((1, d, bn), lambda l, j: (l, 0, j)),
                  pl.BlockSpec((1, 1, bn), lambda l, j: (l, 0, j))],
        out_specs=pl.BlockSpec((1, 8, bn), lambda l, j: (l, 0, j)),
        out_shape=jax.ShapeDtypeStruct((depth, 8, n), F32),
        name="ada_mod",
        compiler_params=_cparams(("arbitrary", "arbitrary")),
    )(cc, ada_w, ada_b.reshape(depth, 1, n))


def _ffn_body(x, m, g, w1_ref, w3_ref, w2_ref, base):
    h = _rms_mod(x, g, m[base:base + 1], m[base + 1:base + 2]).astype(BF16)
    a = _dot(h, w1_ref[...])
    b = _dot(h, w3_ref[...])
    u = (a * jax.nn.sigmoid(a) * b).astype(BF16)
    return x + 0.5 * m[base + 2:base + 3] * _dot(u, w2_ref[...])


def _ffn_specs(w1, w2, l, j):
    sel = lambda a: pl.BlockSpec((None, None) + a.shape[2:], lambda *_: (l, j, 0, 0), pipeline_mode=pl.Buffered(1))
    return [sel(w1), sel(w1), sel(w2)]


def _tile_spec(width, first_tile=0):
    return pl.BlockSpec((TM, width), lambda i: (i + first_tile, 0))


def _mod_spec(d, latent_only=False):
    return pl.BlockSpec((1, N_MOD, d), (lambda i: (1, 0, 0)) if latent_only else (lambda i: (jnp.minimum(i, 1), 0, 0)))


VROWS = HEAD_DIM + 16


def _store_values_t(ref, n_heads, xt):
    for e in range(n_heads):
        r0 = e * VROWS
        ref[0, r0:r0 + HEAD_DIM, :] = xt[e * HEAD_DIM:(e + 1) * HEAD_DIM].astype(BF16)
        ref[0, r0 + HEAD_DIM:r0 + VROWS, :] = jnp.ones((VROWS - HEAD_DIM, xt.shape[1]), BF16)


def _rope_t(x, cos, sin_signed, half, lo, hi):
    blocks = [x[:lo]] if lo else []
    for r in range(lo, hi, 2 * half):
        blocks += [x[r + half:r + 2 * half], x[r:r + half]]
    if hi < x.shape[0]:
        blocks.append(x[hi:])
    return x * cos + jnp.concatenate(blocks, axis=0) * sin_signed

def _proj_ab_body(t, m, g, wk_ref, wqv_ref, wuk_ref, wuv_ref, gq_ref, gk_a_ref, gk_b_ref, gkv_ref, gkvt_ref,
                  cos_a_ref, sin_a_ref, cos_b_ref, sin_b_ref, cost_a_ref, sint_a_ref, cost_b_ref, sint_b_ref,
                  qt_ref, k_ref, vta_ref, vtb_ref):
    hf = _rms_mod(t, g, m[3:4], m[4:5])
    h = hf.astype(BF16)
    ht = hf.T.astype(BF16)
    pk = _dot(h, wk_ref[...])
    c = pk[:, LANES:LANES + B_KV_RANK]
    cn = (c * lax.rsqrt(jnp.mean(c * c, axis=-1, keepdims=True) + EPS) * gkv_ref[...]).astype(BF16)
    kn = _dot(cn, wuk_ref[...])
    pt = _dot(wqv_ref[...], ht)
    nq = A_HEADS + B_HEADS
    cos_a, sin_a = cos_a_ref[...], sin_a_ref[...]
    x = _half_norm(pk[:, :LANES], gk_a_ref[...], HEAD_DIM)
    k_ref[:, B_HEADS * LANES:(B_HEADS + 1) * LANES] = _rope(x, cos_a, sin_a, HEAD_DIM // 4).astype(BF16)
    kr = pk[:, LANES + B_KV_RANK:2 * LANES + B_KV_RANK]
    gcos = gk_b_ref[0:1] * cos_b_ref[...]
    rot = pk[:, 2 * LANES + B_KV_RANK:] * gk_b_ref[1:2] * sin_b_ref[...]
    xs = [kn[:, hd * LANES:(hd + 1) * LANES] + kr for hd in range(B_HEADS)]
    rs = [lax.rsqrt(jnp.sum(x * x, axis=-1, keepdims=True) / B_QK + EPS) for x in xs]
    for hd, (x, r) in enumerate(zip(xs, rs)):
        k_ref[:, hd * LANES:(hd + 1) * LANES] = (r * (x * gcos + rot)).astype(BF16)
    xs = [pt[hd * LANES:(hd + 1) * LANES] for hd in range(nq)]
    rs = [lax.rsqrt(jnp.sum(x * x, axis=0, keepdims=True) / (HEAD_DIM if hd < A_HEADS else B_QK) + EPS)
          for hd, x in enumerate(xs)]
    xs = [x * r * gq_ref[hd * LANES:(hd + 1) * LANES, :] for hd, (x, r) in enumerate(zip(xs, rs))]
    for hd, x in enumerate(xs):
        if hd < A_HEADS:
            x = _rope_t(x, cost_a_ref[...], sint_a_ref[...], HEAD_DIM // 4, 0, LANES)
        else:
            x = _rope_t(x, cost_b_ref[...], sint_b_ref[...], B_ROPE // 4, B_NOPE, B_QK)
        qt_ref[0, hd * LANES:(hd + 1) * LANES, :] = x.astype(BF16)
    _store_values_t(vta_ref, A_KV_HEADS, pt[nq * LANES:(nq + 1) * LANES])
    ct = pt[(nq + 1) * LANES:(nq + 1) * LANES + B_KV_RANK]
    cnt = (ct * lax.rsqrt(jnp.mean(ct * ct, axis=0, keepdims=True) + EPS) * gkvt_ref[...]).astype(BF16)
    _store_values_t(vtb_ref, B_HEADS, _dot(wuv_ref[...], cnt))


def _first_kernel(c_ref, x_ref, m_ref, g_ref, w1_ref, w3_ref, w2_ref, *refs):
    t_ref = refs[-5]
    x = jnp.where(pl.program_id(0) == 0, c_ref[...], x_ref[...])
    m, g = m_ref[0], g_ref[...]
    t = _ffn_body(x, m, g[0:1], w1_ref, w3_ref, w2_ref, 0)
    t_ref[...] = t
    _proj_ab_body(t, m, g[1:2], *refs[:-5], *refs[-4:])


def _first(ctx, x, mod, g, w1, w3, w2, consts, tables, tables_t):
    seq, d = x.shape
    nt = seq // TM + 1
    n = nt * TM
    nq = (A_HEADS + B_HEADS) * LANES
    return pl.pallas_call(
        _first_kernel,
        grid=(nt,),
        in_specs=[pl.BlockSpec((TM, d), lambda i: (0, 0)),
                  pl.BlockSpec((TM, d), lambda i: (jnp.maximum(i - 1, 0), 0)),
                  _mod_spec(d), _const_spec(g.shape)] + _ffn_specs(w1, w2, 0, 0)
                 + [_const_spec(a.shape) for a in consts]
                 + [_tile_spec(LANES)] * len(tables)
                 + [pl.BlockSpec((LANES, TM), lambda i: (0, i))] * len(tables_t),
        out_specs=[_tile_spec(d),
                   pl.BlockSpec((1, nq, TM), lambda i: (i, 0, 0)),
                   _tile_spec((B_HEADS + 1) * LANES),
                   pl.BlockSpec((1, A_KV_HEADS * VROWS, TM), lambda i: (i, 0, 0)),
                   pl.BlockSpec((1, B_HEADS * VROWS, TM), lambda i: (i, 0, 0))],
        out_shape=[jax.ShapeDtypeStruct((n, d), F32),
                   jax.ShapeDtypeStruct((nt, nq, TM), BF16),
                   jax.ShapeDtypeStruct((n, (B_HEADS + 1) * LANES), BF16),
                   jax.ShapeDtypeStruct((nt, A_KV_HEADS * VROWS, TM), BF16),
                   jax.ShapeDtypeStruct((nt, B_HEADS * VROWS, TM), BF16)],
        name="ffn_proj_ab",
        compiler_params=_cparams(("arbitrary",)),
    )(ctx, x, mod, g, w1, w3, w2, *consts, *tables, *tables_t)


def _attn_kernel(qt_ref, k_ref, vt_ref, o_ref, s_ref, mx_ref, *, n_heads, k_slab, v_rows, n_tiles):
    heads = range(n_heads)
    is_ctx = pl.program_id(1) == 0
    n_pairs = jnp.where(is_ctx, 0, (n_tiles - 1) // 2)
    last = jnp.where(is_ctx, 0, n_tiles - 1)

    def scores(kt, slot):
        off = pl.multiple_of(kt * TM, TM)
        for hl in heads:
            r = _dot(k_ref[pl.ds(off, TM), k_slab[hl] * LANES:(k_slab[hl] + 1) * LANES],
                     qt_ref[0, hl * LANES:(hl + 1) * LANES, :])
            s_ref[slot, hl] = r
            mx_ref[slot, hl] = jnp.max(r, axis=0, keepdims=True)

    def consume(kt, slot, carry):
        m, acc = carry
        m_new = tuple(jnp.maximum(m[hl], mx_ref[slot, hl]) for hl in heads)
        alpha = tuple(jnp.exp2(m[hl] - m_new[hl]) for hl in heads)
        pm = tuple(jnp.exp2(s_ref[slot, hl] - m_new[hl]).astype(BF16) for hl in heads)
        acc = tuple(alpha[hl] * acc[hl] + _dot(vt_ref[kt, v_rows[hl] * VROWS:(v_rows[hl] + 1) * VROWS, :], pm[hl])
                    for hl in heads)
        return m_new, acc

    pairs_per_trip = max(u for u in (8, 4, 2, 1) if ((n_tiles - 1) // 2) % u == 0)

    def body(j, carry):
        for u in range(pairs_per_trip):
            t0 = 2 * (j * pairs_per_trip + u)
            scores(t0 + 1, 1)
            carry = consume(t0, 0, carry)
            scores(t0 + 2, 0)
            carry = consume(t0 + 1, 1, carry)
        return carry

    rep = lambda v: (v,) * n_heads
    init = (rep(jnp.full((1, TM), -jnp.inf, F32)), rep(jnp.zeros((VROWS, TM), F32)))
    scores(0, 0)
    _, acc = consume(last, 0, lax.fori_loop(0, n_pairs // pairs_per_trip, body, init))
    for pr in range(n_heads // 2):
        outs = [acc[2 * pr + e][:HEAD_DIM] / acc[2 * pr + e][HEAD_DIM:HEAD_DIM + 1] for e in range(2)]
        o_ref[:, pr * LANES:(pr + 1) * LANES] = jnp.concatenate(outs, axis=0).T.astype(BF16)


def _attn_scratch(n_heads):
    return [pltpu.VMEM((2, n_heads, TM, TM), F32), pltpu.VMEM((2, n_heads, 1, TM), F32)]


def _attn_a(qt, k, vta):
    nt = qt.shape[0]
    assert nt % 2 == 1
    n = nt * TM
    gh = A_HEADS // A_KV_HEADS
    kern = functools.partial(_attn_kernel, n_heads=gh, k_slab=(0,) * gh, v_rows=(0,) * gh, n_tiles=nt)
    return pl.pallas_call(
        kern,
        grid=(A_KV_HEADS, nt),
        in_specs=[pl.BlockSpec((1, gh * LANES, TM), lambda g, i: (i, g, 0)),
                  pl.BlockSpec((n, LANES), lambda g, i: (0, B_HEADS)),
                  pl.BlockSpec((nt, VROWS, TM), lambda g, i: (0, g, 0))],
        out_specs=pl.BlockSpec((TM, gh * HEAD_DIM), lambda g, i: (i, g)),
        out_shape=jax.ShapeDtypeStruct((n, A_HEADS * HEAD_DIM), BF16),
        scratch_shapes=_attn_scratch(gh),
        name="attn_gqa",
        compiler_params=_cparams(("arbitrary", "arbitrary")),
    )(qt, k, vta)


def _attn_b(qt, k, vtb):
    nt = qt.shape[0]
    n = nt * TM
    gh = 4
    hs = tuple(range(gh))
    kern = functools.partial(_attn_kernel, n_heads=gh, k_slab=hs, v_rows=hs, n_tiles=nt)
    return pl.pallas_call(
        kern,
        grid=(B_HEADS // gh, nt),
        in_specs=[pl.BlockSpec((1, gh * LANES, TM), lambda j, i: (i, A_HEADS // gh + j, 0)),
                  pl.BlockSpec((n, gh * LANES), lambda j, i: (0, j), pipeline_mode=pl.Buffered(1)),
                  pl.BlockSpec((nt, gh * VROWS, TM), lambda j, i: (0, j, 0), pipeline_mode=pl.Buffered(1))],
        out_specs=pl.BlockSpec((TM, gh * B_VDIM), lambda j, i: (i, j)),
        out_shape=jax.ShapeDtypeStruct((n, B_HEADS * B_VDIM), BF16),
        scratch_shapes=_attn_scratch(gh),
        name="attn_mla",
        compiler_params=_cparams(("arbitrary", "arbitrary")),
    )(qt, k, vtb)


def _out_ab_kernel(t_ref, m_ref, g_ref, oa_ref, ob_ref, w_ref, w1_ref, w3_ref, w2_ref, o_ref):
    m = m_ref[0]
    half = oa_ref.shape[1]
    y = _dot(oa_ref[...], w_ref[:half, :]) + _dot(ob_ref[...], w_ref[half:, :])
    x = t_ref[...] + m[5:6] * y
    o_ref[...] = _ffn_body(x, m, g_ref[2:3], w1_ref, w3_ref, w2_ref, 6)


def _out_ab(t, mod, g, oa, ob, w, w1, w3, w2):
    n, d = t.shape
    return pl.pallas_call(
        _out_ab_kernel,
        grid=(n // TM,),
        in_specs=[_tile_spec(d), _mod_spec(d), _const_spec(g.shape), _tile_spec(oa.shape[1]),
                  _tile_spec(ob.shape[1]), _const_spec(w.shape)] + _ffn_specs(w1, w2, 0, 1),
        out_specs=_tile_spec(d),
        out_shape=jax.ShapeDtypeStruct((n, d), F32),
        name="out_ab_ffn",
        compiler_params=_cparams(("arbitrary",)),
    )(t, mod, g, oa, ob, w, w1, w3, w2)


def _proj_cd_kernel(t_ref, m_ref, g_ref, w1_ref, w3_ref, w2_ref, w_ref, lbp_ref, gq_ref, gk_ref,
                    t_out_ref, hq_ref, gf_ref, gb_ref, hv_ref, hg_ref, dq_ref, dk_ref, dv_ref, *, layer):
    m = m_ref[0]
    t = _ffn_body(t_ref[...], m, g_ref[0:1], w1_ref, w3_ref, w2_ref, 0)
    t_out_ref[...] = t
    h = _rms_mod(t, g_ref[1:2], m[3:4], m[4:5]).astype(BF16)
    p = _dot(h, w_ref[...])
    w = C_HEADS * C_DK
    x = p[:, 0:w]
    hq_ref[...] = x * jax.nn.sigmoid(x)
    lbp = lbp_ref[...]
    e = jnp.exp(lbp - jnp.max(lbp, axis=0, keepdims=True))
    sm = e / jnp.sum(e, axis=0, keepdims=True)
    lb = jnp.sum(sm[1:layer + 1], axis=0)
    for dr, ref in ((0, gf_ref), (1, gb_ref)):
        z = p[:, (1 + dr) * w:(2 + dr) * w]
        lbd = lb[dr:dr + 1]
        ref[...] = jnp.log(lbd + (1.0 - lbd) * jax.nn.sigmoid(z))
    hv_ref[...] = p[:, 3 * w:4 * w].astype(BF16)
    hg_ref[...] = p[:, 4 * w:5 * w]
    for s in range(w // LANES):
        sl = slice(s * LANES, (s + 1) * LANES)
        q = _half_norm(p[:, 5 * w + s * LANES:5 * w + (s + 1) * LANES], gq_ref[...], HEAD_DIM)
        dq_ref[:, sl] = (q * (HEAD_DIM ** -0.5 * LOG2E)).astype(BF16)
        dk_ref[:, sl] = _half_norm(p[:, 6 * w + s * LANES:6 * w + (s + 1) * LANES], gk_ref[...], HEAD_DIM).astype(BF16)
    dv_ref[...] = p[:, 7 * w:8 * w].astype(BF16)


def _proj_cd(t, mod, g, w1, w3, w2, w_in, hgrn_lb, gq, gk, layer):
    n, d = t.shape
    w = C_HEADS * C_DK
    shp = lambda dt: jax.ShapeDtypeStruct((n, w), dt)
    return pl.pallas_call(
        functools.partial(_proj_cd_kernel, layer=layer),
        grid=(n // TM,),
        in_specs=[_tile_spec(d), _mod_spec(d), _const_spec(g.shape)] + _ffn_specs(w1, w2, layer, 0)
                 + [_const_spec(w_in.shape), _const_spec(hgrn_lb.shape), _const_spec(gq.shape), _const_spec(gk.shape)],
        out_specs=[_tile_spec(d)] + [_tile_spec(w)] * 8,
        out_shape=[jax.ShapeDtypeStruct((n, d), F32),
                   shp(F32), shp(F32), shp(F32), shp(BF16), shp(F32), shp(BF16), shp(BF16), shp(BF16)],
        name="ffn_proj_cd",
        compiler_params=_cparams(("arbitrary",)),
    )(t, mod, g, w1, w3, w2, w_in, hgrn_lb, gq, gk)


N_LEVELS = int(math.log2(C_CHUNK))


def _hgrn_tables():
    t = np.arange(C_CHUNK)
    return (t[None, :] <= t[:, None]).astype(np.float32)


def _level_boundary(gi, h, reverse):
    c = gi.shape[0]
    if h >= 4:
        blocks = []
        for s in range(0, c, 2 * h):
            b = s + (h if reverse else h - 1)
            blocks.append(jnp.broadcast_to(gi[b:b + 1], (2 * h, gi.shape[1])))
        return jnp.concatenate(blocks, axis=0)
    pos = lax.broadcasted_iota(jnp.int32, gi.shape, 0) % (2 * h)
    gb = gi
    for p in range(2 * h):
        d = (h if reverse else h - 1) - p
        if d != 0:
            gb = jnp.where(pos == p, pltpu.roll(gi, (c - d) % c, 0), gb)
    return gb


def _level_masks(reverse):
    c = C_CHUNK
    ti = lax.broadcasted_iota(jnp.int32, (2 * c, c), 0) % c
    si = lax.broadcasted_iota(jnp.int32, (2 * c, c), 1)
    tq, sk = (si, ti) if reverse else (ti, si)
    masks = []
    for lv in range(N_LEVELS):
        h = c >> (lv + 1)
        sh = N_LEVELS - lv
        masks.append(((ti >> sh) == (si >> sh)) & ((tq & h) != 0) & ((sk & h) == 0))
    masks.append(ti == si)
    return masks


def _hgrn_chunks(jobs, st_ref):
    c = C_CHUNK
    gis = []
    for q, g, v, tri, masks, d, reverse in jobs:
        ghi = g.astype(BF16)
        glo = (g - ghi.astype(F32)).astype(BF16)
        gis.append(_dot(tri, ghi) + _dot(tri, glo))
    ops = []
    for (q, g, v, tri, masks, d, reverse), gi in zip(jobs, gis):
        kk = 1.0 - jnp.exp(g)
        glast = gi[(c - 1):c] if not reverse else gi[0:1]
        qs = (q * jnp.exp(gi)).astype(BF16)
        ks = (kk * jnp.exp(glast - gi)).astype(BF16)
        eqs, eks = [], []
        for lv in range(N_LEVELS):
            e = jnp.exp(-jnp.abs(gi - _level_boundary(gi, c >> (lv + 1), reverse)))
            eqs.append((e * q).astype(BF16))
            eks.append((e * kk).astype(BF16))
        eqs.append(q.astype(BF16))
        eks.append(kk.astype(BF16))
        ops.append((eqs, eks, qs, ks, jnp.exp(glast)))
    left = _lane((c, LANES)) < C_DK
    blockdiag = (lax.broadcasted_iota(jnp.int32, (LANES, LANES), 0) < C_DV) == (_lane((LANES, LANES)) < C_DK)
    outs = [[] for _ in jobs]
    for pr in range(C_HEADS // 2):
        sl = slice(pr * LANES, (pr + 1) * LANES)
        a2s = [jnp.zeros((2 * c, c), F32) for _ in jobs]
        for lv in range(N_LEVELS + 1):
            for n, job in enumerate(jobs):
                eq = ops[n][0][lv][:, sl]
                eq2 = jnp.concatenate([jnp.where(left, eq, 0), jnp.where(left, 0, eq)], axis=0)
                a = _dot_nt(eq2, ops[n][1][lv][:, sl])
                a2s[n] = jnp.where(job[4][lv], a, a2s[n])
        r2s = [_dot(a2s[n].astype(BF16), job[2][:, sl]) for n, job in enumerate(jobs)]
        upds = [_dot_tn(job[2][:, sl], ops[n][3][:, sl]) for n, job in enumerate(jobs)]
        for n, job in enumerate(jobs):
            d = job[5]
            st = st_ref[d, pr]
            outs[n].append(jnp.where(left, r2s[n][:c], r2s[n][c:]) + _dot_nt(ops[n][2][:, sl], st.astype(BF16)))
            st_ref[d, pr] = st * ops[n][4][:, sl] + jnp.where(blockdiag, upds[n], 0.0)
    return [jnp.concatenate(o, axis=1) for o in outs]


HGRN_CHUNKS_PER_PASS = 4


def _hgrn_kernel(qf_ref, gf_ref, vf_ref, qb_ref, gb_ref, vb_ref, wf_ref, wb_ref, of_ref, ob_ref, st_ref):
    @pl.when(pl.program_id(0) == 0)
    def _():
        st_ref[...] = jnp.zeros(st_ref.shape, F32)

    c = C_CHUNK
    mf, mb = _level_masks(False), _level_masks(True)
    tf, tb = wf_ref[...], wb_ref[...]
    for j0 in range(0, TM // c, HGRN_CHUNKS_PER_PASS):
        js = range(j0, j0 + HGRN_CHUNKS_PER_PASS)
        fs = [slice(j * c, (j + 1) * c) for j in js]
        bs = [slice(TM - (j + 1) * c, TM - j * c) for j in js]
        jobs = ([(qf_ref[s, :], gf_ref[s, :], vf_ref[s, :], tf, mf, 0, False) for s in fs]
                + [(qb_ref[s, :], gb_ref[s, :], vb_ref[s, :], tb, mb, 1, True) for s in bs])
        outs = _hgrn_chunks(jobs, st_ref)
        for s, o in zip(fs, outs[:len(fs)]):
            of_ref[s, :] = o
        for s, o in zip(bs, outs[len(fs):]):
            ob_ref[s, :] = o


def _hgrn(hq, gf, gb, hv):
    n, w = hq.shape
    nt = n // TM
    wf = _hgrn_tables()
    wb = wf.reshape(-1, C_CHUNK, C_CHUNK)[:, ::-1, ::-1].reshape(wf.shape)
    fwd = pl.BlockSpec((TM, w), lambda i: (i, 0))
    bwd = pl.BlockSpec((TM, w), lambda i: (jnp.where(i == 0, 0, nt - i), 0))
    return pl.pallas_call(
        _hgrn_kernel,
        grid=(nt,),
        in_specs=[fwd, fwd, fwd, bwd, bwd, bwd, _const_spec(wf.shape), _const_spec(wb.shape)],
        out_specs=[fwd, bwd],
        out_shape=[jax.ShapeDtypeStruct((n, w), F32)] * 2,
        scratch_shapes=[pltpu.VMEM((2, C_HEADS // 2, LANES, LANES), F32)],
        name="hgrn2",
        compiler_params=_cparams(("arbitrary",)),
    )(hq, gf, hv, hq, gb, hv, jnp.asarray(wf, BF16), jnp.asarray(wb, BF16))


NAT_ROWS = TM // GRID_W


def _natten_kernel(q_ref, k_ref, v_ref, bias_ref, o_ref, *, rows):
    i = pl.program_id(1)
    kc = k_ref[0:TM, :]
    vc = v_ref[0:TM, :]
    left_q = _lane((GRID_W, LANES)) < HEAD_DIM
    q2s, offs, sws = [], [], []
    for j in range(NAT_ROWS):
        r = i * NAT_ROWS + j
        rs = jnp.clip(r - D_WIN_H // 2, 0, rows - D_WIN_H)
        offs.append(pl.multiple_of(TM + rs * GRID_W, GRID_W))
        q = q_ref[j * GRID_W:(j + 1) * GRID_W, :]
        q2s.append(jnp.concatenate([jnp.where(left_q, q, 0), jnp.where(left_q, 0, q)], axis=0))
        kw = k_ref[pl.ds(offs[j], D_WIN_H * GRID_W), :]
        a0 = rs - r + (D_WIN_H - 1)
        bias = jnp.concatenate([bias_ref[0, a0 + 2 * u] for u in range(D_WIN_H // 2)], axis=1)
        sws.append(_dot_nt(q2s[j], kw) + bias)
    sc = _dot_nt(jnp.concatenate(q2s, axis=0), kc)
    pws, pcs, ls = [], [], []
    for j in range(NAT_ROWS):
        scj = sc[j * 2 * GRID_W:(j + 1) * 2 * GRID_W]
        m = jnp.maximum(jnp.max(sws[j], axis=-1, keepdims=True), jnp.max(scj, axis=-1, keepdims=True))
        pw = jnp.exp2(sws[j] - m)
        pc = jnp.exp2(scj - m)
        ls.append(jnp.sum(pw, axis=-1, keepdims=True) + jnp.sum(pc, axis=-1, keepdims=True))
        pws.append(pw.astype(BF16))
        pcs.append(pc.astype(BF16))
    oc = _dot(jnp.concatenate(pcs, axis=0), vc)
    for j in range(NAT_ROWS):
        vw = v_ref[pl.ds(offs[j], D_WIN_H * GRID_W), :]
        o2 = (_dot(pws[j], vw) + oc[j * 2 * GRID_W:(j + 1) * 2 * GRID_W]) / ls[j]
        o_ref[j * GRID_W:(j + 1) * GRID_W, :] = jnp.where(left_q, o2[:GRID_W], o2[GRID_W:]).astype(BF16)


def _natten(dq, dk, dv, bias):
    n, w = dq.shape
    nt = n // TM - 1
    rows = nt * NAT_ROWS
    return pl.pallas_call(
        functools.partial(_natten_kernel, rows=rows),
        grid=(w // LANES, nt),
        in_specs=[pl.BlockSpec((TM, LANES), lambda p, i: (i + 1, p)),
                  pl.BlockSpec((n, LANES), lambda p, i: (0, p)),
                  pl.BlockSpec((n, LANES), lambda p, i: (0, p)),
                  pl.BlockSpec((1,) + bias.shape[1:], lambda p, i: (p, 0, 0, 0))],
        out_specs=pl.BlockSpec((TM, LANES), lambda p, i: (i, p)),
        out_shape=jax.ShapeDtypeStruct((nt * TM, w), BF16),
        name="natten",
        compiler_params=_cparams(("arbitrary", "arbitrary")),
    )(dq, dk, dv, bias)


def _natten_bias(rpb):
    cols = np.arange(GRID_W)
    start = np.clip(cols - D_WIN_W // 2, 0, GRID_W - D_WIN_W)
    kc = cols[None, :]
    inside = (kc >= start[:, None]) & (kc < start[:, None] + D_WIN_W)
    rel = kc - cols[:, None] + (D_WIN_W - 1)
    onehot = (rel[:, :, None] == np.arange(2 * D_WIN_W - 1)).astype(np.float32)
    tab = jnp.einsum("har,ckr->hack", rpb, onehot, precision=lax.Precision.HIGHEST)
    tab = jnp.where(inside[None, None], tab * LOG2E, NEG)
    na = tab.shape[1]
    t = jnp.transpose(tab.reshape(D_HEADS // 2, 2, na, GRID_W, GRID_W), (0, 2, 1, 3, 4))
    t = t.reshape(D_HEADS // 2, na, 2 * GRID_W, GRID_W)
    return jnp.concatenate([t[:, :-1], t[:, 1:]], axis=-1)


def _out_cd_kernel(t_ref, m_ref, g_ref, of_ref, ob_ref, hg_ref, nat_ref, gn_ref, w_ref, w1_ref, w3_ref, w2_ref, o_ref):
    m = m_ref[0]
    half = nat_ref.shape[1]
    o = of_ref[...] + ob_ref[...]
    gate = hg_ref[...]
    gate = gate * jax.nn.sigmoid(gate)
    parts = []
    for s in range(half // LANES):
        sl = slice(s * LANES, (s + 1) * LANES)
        parts.append((_half_norm(o[:, sl], gn_ref[...], C_DV) * gate[:, sl]).astype(BF16))
    y = _dot(jnp.concatenate(parts, axis=1), w_ref[:half, :]) + _dot(nat_ref[...], w_ref[half:, :])
    x = t_ref[...] + m[5:6] * y
    o_ref[...] = _ffn_body(x, m, g_ref[2:3], w1_ref, w3_ref, w2_ref, 6)


def _out_cd(t, mod, g, o_f, o_b, hg, nat, gn, w, w1, w3, w2, layer):
    n, d = t.shape
    nt = n // TM - 1
    wd = nat.shape[1]
    return pl.pallas_call(
        _out_cd_kernel,
        grid=(nt,),
        in_specs=[_tile_spec(d, 1), _mod_spec(d, latent_only=True), _const_spec(g.shape),
                  _tile_spec(wd, 1), _tile_spec(wd, 1), _tile_spec(wd, 1), _tile_spec(wd),
                  _const_spec(gn.shape), _const_spec(w.shape)] + _ffn_specs(w1, w2, layer, 1),
        out_specs=_tile_spec(d),
        out_shape=jax.ShapeDtypeStruct((nt * TM, d), F32),
        name="out_cd_ffn",
        compiler_params=_cparams(("arbitrary",)),
    )(t, mod, g, o_f, o_b, hg, nat, gn, w, w1, w3, w2)


def _rope_tables(n_lat, dim, lane0):
    rows = n_lat // GRID_W
    dh = dim // 2
    q = dim // 4
    sign = jnp.where(jnp.arange(dh) < q, -1.0, 1.0)

    def one(n_pos):
        inv = ROPE_THETA ** (-jnp.arange(0, dh, 2, dtype=F32) / dh)
        ang = jnp.arange(n_pos, dtype=jnp.int32).astype(F32)[:, None] * inv[None, :]
        ang = jnp.concatenate([ang, ang], axis=-1)
        return jnp.cos(ang), jnp.sin(ang) * sign

    (cos_r, sin_r), (cos_c, sin_c) = one(rows), one(GRID_W)
    grid = lambda r, c: jnp.concatenate([jnp.broadcast_to(r[:, None], (rows, GRID_W, dh)),
                                         jnp.broadcast_to(c[None], (rows, GRID_W, dh))], axis=-1).reshape(n_lat, dim)
    cos, sin = grid(cos_r, cos_c), grid(sin_r, sin_c)
    if dim == HEAD_DIM:
        cos, sin = jnp.concatenate([cos, cos], -1), jnp.concatenate([sin, sin], -1)
    else:
        pad = ((0, 0), (lane0, LANES - lane0 - dim))
        cos = jnp.pad(cos, pad, constant_values=1.0)
        sin = jnp.pad(sin, pad)
    cos = jnp.concatenate([jnp.ones((TM, LANES), F32), cos], axis=0)
    sin = jnp.concatenate([jnp.zeros((TM, LANES), F32), sin], axis=0)
    return cos, sin


def _pad_lanes(x, lo, width=LANES):
    return jnp.pad(x, [(0, 0)] * (x.ndim - 1) + [(lo, width - lo - x.shape[-1])])


def _ab_weights(w_in, a_qn, a_kn, b_qn, b_kn, b_kvn, wukv):
    d = w_in.shape[0]
    grp = A_HEADS // A_KV_HEADS
    o = 0
    aq = w_in[:, o:o + A_HEADS * HEAD_DIM].reshape(d, A_HEADS, HEAD_DIM); o += A_HEADS * HEAD_DIM
    ak = w_in[:, o:o + A_KV_HEADS * HEAD_DIM]; o += A_KV_HEADS * HEAD_DIM
    av = w_in[:, o:o + A_KV_HEADS * HEAD_DIM]; o += A_KV_HEADS * HEAD_DIM
    bq = w_in[:, o:o + B_HEADS * B_QK].reshape(d, B_HEADS, B_QK); o += B_HEADS * B_QK
    bkv = w_in[:, o:o + B_KV_RANK]; o += B_KV_RANK
    bkr = w_in[:, o:o + B_ROPE]
    aq_slabs = jnp.concatenate([_pad_lanes(aq[:, h], (h // grp) * HEAD_DIM) for h in range(A_HEADS)], axis=1)
    bq_slabs = _pad_lanes(bq, 0).reshape(d, B_HEADS * LANES)
    half = B_ROPE // 4
    partner = np.where(np.arange(B_ROPE) % (2 * half) < half, np.arange(B_ROPE) + half, np.arange(B_ROPE) - half)
    wk = jnp.concatenate([ak, bkv, _pad_lanes(bkr, B_NOPE), _pad_lanes(bkr[:, partner], B_NOPE)], axis=1).astype(BF16)
    wqv = jnp.concatenate([aq_slabs, bq_slabs, av, bkv], axis=1).T.astype(BF16)
    kvw = wukv.reshape(B_KV_RANK, B_HEADS, B_NOPE + B_VDIM)
    wuk = _pad_lanes(kvw[:, :, :B_NOPE], 0).reshape(B_KV_RANK, B_HEADS * LANES).astype(BF16)
    wuv = kvw[:, :, B_NOPE:].reshape(B_KV_RANK, B_HEADS * B_VDIM).T.astype(BF16)
    gq_a = jnp.concatenate([_pad_lanes(a_qn, (h // grp) * HEAD_DIM) for h in range(A_HEADS)])
    gq_b = jnp.tile(_pad_lanes(b_qn, 0), B_HEADS)
    gq = jnp.concatenate([gq_a * (HEAD_DIM ** -0.5 * LOG2E), gq_b * (B_QK ** -0.5 * LOG2E)])
    gq = jnp.broadcast_to(gq[:, None], (gq.shape[0], TM))
    gk_a = jnp.concatenate([a_kn, a_kn])[None]
    gk_b = jnp.stack([_pad_lanes(b_kn, 0), _pad_lanes(b_kn[B_NOPE:][partner], B_NOPE)])
    gkvt = jnp.broadcast_to(b_kvn[:, None], (B_KV_RANK, TM))
    return [wk, wqv, wuk, wuv, gq, gk_a, gk_b, b_kvn[None], gkvt]


def kernel(x, c, ctx, c_ctx, norm_g, ada_w, ada_b, ffn_w1, ffn_w3, ffn_w2, ab_w_in, ab_a_qn, ab_a_kn, ab_b_qn,
           ab_b_kn, ab_b_kvn, ab_b_wukv, ab_w_out, cd_w_in, hgrn_lb, cd_c_gn, cd_d_qn, cd_d_kn, cd_d_rpb, cd_w_out):
    batch, seq, d = x.shape
    depth = norm_g.shape[0]
    assert batch == 1 and ctx.shape[1] == TM and seq % TM == 0 and depth == 2

    cc = jnp.zeros((8, d), F32).at[0].set(c_ctx).at[1].set(c[0])
    mods = _ada(cc, ada_w, ada_b)[:, :2].reshape(depth, 2, N_MOD, d)
    bf = lambda a: a.astype(BF16)
    w1, w3, w2 = bf(ffn_w1), bf(ffn_w3), bf(ffn_w2)

    consts = _ab_weights(ab_w_in[0], ab_a_qn[0], ab_a_kn[0], ab_b_qn[0], ab_b_kn[0], ab_b_kvn[0], ab_b_wukv[0])
    tables = _rope_tables(seq, HEAD_DIM, 0) + _rope_tables(seq, B_ROPE, B_NOPE)
    t, qt, k, vta, vtb = _first(ctx[0], x[0], mods[0], norm_g[0], w1, w3, w2, consts, tables,
                                tuple(a.T for a in tables))
    oa = _attn_a(qt, k, vta)
    ob = _attn_b(qt, k, vtb)
    t = _out_ab(t, mods[0], norm_g[0], oa, ob, bf(ab_w_out[0]), w1, w3, w2)

    gq = jnp.concatenate([cd_d_qn[0], cd_d_qn[0]])[None]
    gk = jnp.concatenate([cd_d_kn[0], cd_d_kn[0]])[None]
    gn = jnp.concatenate([cd_c_gn[0], cd_c_gn[0]])[None]
    t, hq, gf, gb, hv, hg, dq, dk, dv = _proj_cd(t, mods[1], norm_g[1], w1, w3, w2, bf(cd_w_in[0]), hgrn_lb, gq, gk, 1)
    o_f, o_b = _hgrn(hq, gf, gb, hv)
    nat = _natten(dq, dk, dv, _natten_bias(cd_d_rpb[0]))
    xl = _out_cd(t, mods[1], norm_g[1], o_f, o_b, hg, nat, gn, bf(cd_w_out[0]), w1, w3, w2, 1)
    return xl[None]
```

```python
import functools
import math

import numpy as np
import jax
import jax.numpy as jnp
from jax import lax
from jax.experimental import pallas as pl
from jax.experimental.pallas import tpu as pltpu

F32 = jnp.float32
BF16 = jnp.bfloat16

GRID_W = 64
HEAD_DIM = 64
EPS = 1e-6
ROPE_THETA = 10000.0
N_MOD = 9
A_HEADS, A_KV_HEADS = 8, 2
B_HEADS, B_NOPE, B_ROPE, B_VDIM, B_KV_RANK = 8, 64, 32, 64, 256
B_QK = B_NOPE + B_ROPE
C_HEADS, C_DK, C_DV, C_CHUNK = 8, 64, 64, 64
D_HEADS, D_WIN_H, D_WIN_W = 8, 8, 16

LANES = 128
TM = 256
VMEM_LIMIT = 56 * 1024 * 1024
NEG = -1e30
LOG2E = math.log2(math.e)


def _cparams(sem):
    return pltpu.CompilerParams(dimension_semantics=sem, vmem_limit_bytes=VMEM_LIMIT)


def _const_spec(shape):
    nd = len(shape)
    return pl.BlockSpec(shape, lambda *_: (0,) * nd, pipeline_mode=pl.Buffered(1))


def _lane(shape):
    return lax.broadcasted_iota(jnp.int32, shape, len(shape) - 1)


def _rms_mod(x, g, shift, scale):
    y = x * lax.rsqrt(jnp.mean(x * x, axis=-1, keepdims=True) + EPS) * g
    return y * (1.0 + scale) + shift


def _half_norm(x, gain, n):
    left = _lane(x.shape) < n
    x2 = x * x
    sl = jnp.sum(jnp.where(left, x2, 0.0), axis=-1, keepdims=True)
    sr = jnp.sum(jnp.where(left, 0.0, x2), axis=-1, keepdims=True)
    r = jnp.where(left, lax.rsqrt(sl / n + EPS), lax.rsqrt(sr / n + EPS))
    return x * r * gain


def _slab_norm(x, gain, n):
    ss = jnp.sum(x * x, axis=-1, keepdims=True)
    return x * lax.rsqrt(ss / n + EPS) * gain


def _rope(x, cos, sin_signed, half):
    first = (_lane(x.shape) % (2 * half)) < half
    w = x.shape[-1]
    partner = jnp.where(first, pltpu.roll(x, w - half, 1), pltpu.roll(x, half, 1))
    return x * cos + partner * sin_signed


def _dot(a, b):
    return jnp.dot(a, b, preferred_element_type=F32)


def _dot_nt(a, b):
    return lax.dot_general(a, b, (((1,), (1,)), ((), ())), preferred_element_type=F32)


def _dot_tn(a, b):
    return lax.dot_general(a, b, (((0,), (0,)), ((), ())), preferred_element_type=F32)


def _ada_kernel(c_ref, w_ref, b_ref, o_ref):
    c = c_ref[...]
    a = c * jax.nn.sigmoid(c)
    o_ref[0] = jnp.dot(a, w_ref[0], preferred_element_type=F32, precision=lax.Precision.HIGHEST) + b_ref[0]


def _ada(cc, ada_w, ada_b):
    depth, d, n = ada_w.shape
    bn = n // 4
    return pl.pallas_call(
        _ada_kernel,
        grid=(depth, n // bn),
        in_specs=[pl.BlockSpec((8, d), lambda l, j: (0, 0)),
                  pl.BlockSpec((1, d, bn), lambda l, j: (l, 0, j)),
                  pl.BlockSpec((1, 1, bn), lambda l, j: (l, 0, j))],
        out_specs=pl.BlockSpec((1, 8, bn), lambda l, j: (l, 0, j)),
        out_shape=jax.ShapeDtypeStruct((depth, 8, n), F32),
        name="ada_mod",
        compiler_params=_cparams(("arbitrary", "arbitrary")),
    )(cc, ada_w, ada_b.reshape(depth, 1, n))


def _ffn_body(x, m, g, w1_ref, w3_ref, w2_ref, base):
    h = _rms_mod(x, g, m[base:base + 1], m[base + 1:base + 2]).astype(BF16)
    a = _dot(h, w1_ref[...])
    b = _dot(h, w3_ref[...])
    u = (a * jax.nn.sigmoid(a) * b).astype(BF16)
    return x + 0.5 * m[base + 2:base + 3] * _dot(u, w2_ref[...])


def _ffn_specs(w1, w2, l, j):
    sel = lambda a: pl.BlockSpec((None, None) + a.shape[2:], lambda *_: (l, j, 0, 0), pipeline_mode=pl.Buffered(1))
    return [sel(w1), sel(w1), sel(w2)]


def _tile_spec(width, first_tile=0):
    return pl.BlockSpec((TM, width), lambda i: (i + first_tile, 0))


def _mod_spec(d, latent_only=False):
    return pl.BlockSpec((1, N_MOD, d), (lambda i: (1, 0, 0)) if latent_only else (lambda i: (jnp.minimum(i, 1), 0, 0)))


VROWS = HEAD_DIM + 16


def _store_values_t(ref, n_heads, xt):
    for e in range(n_heads):
        r0 = e * VROWS
        ref[0, r0:r0 + HEAD_DIM, :] = xt[e * HEAD_DIM:(e + 1) * HEAD_DIM].astype(BF16)
        ref[0, r0 + HEAD_DIM:r0 + VROWS, :] = jnp.ones((VROWS - HEAD_DIM, xt.shape[1]), BF16)


def _rope_t(x, cos, sin_signed, half, lo, hi):
    blocks = [x[:lo]] if lo else []
    for r in range(lo, hi, 2 * half):
        blocks += [x[r + half:r + 2 * half], x[r:r + half]]
    if hi < x.shape[0]:
        blocks.append(x[hi:])
    return x * cos + jnp.concatenate(blocks, axis=0) * sin_signed

def _proj_ab_body(t, m, g, wk_ref, wqv_ref, wuk_ref, wuv_ref, gq_ref, gk_a_ref, gk_b_ref, gkv_ref, gkvt_ref,
                  cos_a_ref, sin_a_ref, cos_b_ref, sin_b_ref, cost_a_ref, sint_a_ref, cost_b_ref, sint_b_ref,
                  qt_ref, k_ref, vta_ref, vtb_ref):
    hf = _rms_mod(t, g, m[3:4], m[4:5])
    h = hf.astype(BF16)
    ht = hf.T.astype(BF16)
    pk = _dot(h, wk_ref[...])
    c = pk[:, LANES:LANES + B_KV_RANK]
    cn = (c * lax.rsqrt(jnp.mean(c * c, axis=-1, keepdims=True) + EPS) * gkv_ref[...]).astype(BF16)
    kn = _dot(cn, wuk_ref[...])
    pt = _dot(wqv_ref[...], ht)
    nq = A_HEADS + B_HEADS
    cos_a, sin_a = cos_a_ref[...], sin_a_ref[...]
    x = _half_norm(pk[:, :LANES], gk_a_ref[...], HEAD_DIM)
    k_ref[:, B_HEADS * LANES:(B_HEADS + 1) * LANES] = _rope(x, cos_a, sin_a, HEAD_DIM // 4).astype(BF16)
    kr = pk[:, LANES + B_KV_RANK:2 * LANES + B_KV_RANK]
    gcos = gk_b_ref[0:1] * cos_b_ref[...]
    rot = pk[:, 2 * LANES + B_KV_RANK:] * gk_b_ref[1:2] * sin_b_ref[...]
    xs = [kn[:, hd * LANES:(hd + 1) * LANES] + kr for hd in range(B_HEADS)]
    rs = [lax.rsqrt(jnp.sum(x * x, axis=-1, keepdims=True) / B_QK + EPS) for x in xs]
    for hd, (x, r) in enumerate(zip(xs, rs)):
        k_ref[:, hd * LANES:(hd + 1) * LANES] = (r * (x * gcos + rot)).astype(BF16)
    xs = [pt[hd * LANES:(hd + 1) * LANES] for hd in range(nq)]
    rs = [lax.rsqrt(jnp.sum(x * x, axis=0, keepdims=True) / (HEAD_DIM if hd < A_HEADS else B_QK) + EPS)
          for hd, x in enumerate(xs)]
    xs = [x * r * gq_ref[hd * LANES:(hd + 1) * LANES, :] for hd, (x, r) in enumerate(zip(xs, rs))]
    for hd, x in enumerate(xs):
        if hd < A_HEADS:
            x = _rope_t(x, cost_a_ref[...], sint_a_ref[...], HEAD_DIM // 4, 0, LANES)
        else:
            x = _rope_t(x, cost_b_ref[...], sint_b_ref[...], B_ROPE // 4, B_NOPE, B_QK)
        qt_ref[0, hd * LANES:(hd + 1) * LANES, :] = x.astype(BF16)
    _store_values_t(vta_ref, A_KV_HEADS, pt[nq * LANES:(nq + 1) * LANES])
    ct = pt[(nq + 1) * LANES:(nq + 1) * LANES + B_KV_RANK]
    cnt = (ct * lax.rsqrt(jnp.mean(ct * ct, axis=0, keepdims=True) + EPS) * gkvt_ref[...]).astype(BF16)
    _store_values_t(vtb_ref, B_HEADS, _dot(wuv_ref[...], cnt))


def _first_kernel(c_ref, x_ref, m_ref, g_ref, w1_ref, w3_ref, w2_ref, *refs):
    t_ref = refs[-5]
    x = jnp.where(pl.program_id(0) == 0, c_ref[...], x_ref[...])
    m, g = m_ref[0], g_ref[...]
    t = _ffn_body(x, m, g[0:1], w1_ref, w3_ref, w2_ref, 0)
    t_ref[...] = t
    _proj_ab_body(t, m, g[1:2], *refs[:-5], *refs[-4:])


def _first(ctx, x, mod, g, w1, w3, w2, consts, tables, tables_t):
    seq, d = x.shape
    nt = seq // TM + 1
    n = nt * TM
    nq = (A_HEADS + B_HEADS) * LANES
    return pl.pallas_call(
        _first_kernel,
        grid=(nt,),
        in_specs=[pl.BlockSpec((TM, d), lambda i: (0, 0)),
                  pl.BlockSpec((TM, d), lambda i: (jnp.maximum(i - 1, 0), 0)),
                  _mod_spec(d), _const_spec(g.shape)] + _ffn_specs(w1, w2, 0, 0)
                 + [_const_spec(a.shape) for a in consts]
                 + [_tile_spec(LANES)] * len(tables)
                 + [pl.BlockSpec((LANES, TM), lambda i: (0, i))] * len(tables_t),
        out_specs=[_tile_spec(d),
                   pl.BlockSpec((1, nq, TM), lambda i: (i, 0, 0)),
                   _tile_spec((B_HEADS + 1) * LANES),
                   pl.BlockSpec((1, A_KV_HEADS * VROWS, TM), lambda i: (i, 0, 0)),
                   pl.BlockSpec((1, B_HEADS * VROWS, TM), lambda i: (i, 0, 0))],
        out_shape=[jax.ShapeDtypeStruct((n, d), F32),
                   jax.ShapeDtypeStruct((nt, nq, TM), BF16),
                   jax.ShapeDtypeStruct((n, (B_HEADS + 1) * LANES), BF16),
                   jax.ShapeDtypeStruct((nt, A_KV_HEADS * VROWS, TM), BF16),
                   jax.ShapeDtypeStruct((nt, B_HEADS * VROWS, TM), BF16)],
        name="ffn_proj_ab",
        compiler_params=_cparams(("arbitrary",)),
    )(ctx, x, mod, g, w1, w3, w2, *consts, *tables, *tables_t)


def _attn_kernel(qt_ref, k_ref, vt_ref, o_ref, s_ref, mx_ref, *, n_heads, k_slab, v_rows, n_tiles):
    heads = range(n_heads)
    is_ctx = pl.program_id(1) == 0
    n_pairs = jnp.where(is_ctx, 0, (n_tiles - 1) // 2)
    last = jnp.where(is_ctx, 0, n_tiles - 1)

    def scores(kt, slot):
        off = pl.multiple_of(kt * TM, TM)
        for hl in heads:
            r = _dot(k_ref[pl.ds(off, TM), k_slab[hl] * LANES:(k_slab[hl] + 1) * LANES],
                     qt_ref[0, hl * LANES:(hl + 1) * LANES, :])
            s_ref[slot, hl] = r
            mx_ref[slot, hl] = jnp.max(r, axis=0, keepdims=True)

    def consume(kt, slot, carry):
        m, acc = carry
        m_new = tuple(jnp.maximum(m[hl], mx_ref[slot, hl]) for hl in heads)
        alpha = tuple(jnp.exp2(m[hl] - m_new[hl]) for hl in heads)
        pm = tuple(jnp.exp2(s_ref[slot, hl] - m_new[hl]).astype(BF16) for hl in heads)
        acc = tuple(alpha[hl] * acc[hl] + _dot(vt_ref[kt, v_rows[hl] * VROWS:(v_rows[hl] + 1) * VROWS, :], pm[hl])
                    for hl in heads)
        return m_new, acc

    pairs_per_trip = max(u for u in (16, 8, 4, 2, 1) if ((n_tiles - 1) // 2) % u == 0)

    def body(j, carry):
        for u in range(pairs_per_trip):
            t0 = 2 * (j * pairs_per_trip + u)
            scores(t0 + 1, 1)
            carry = consume(t0, 0, carry)
            scores(t0 + 2, 0)
            carry = consume(t0 + 1, 1, carry)
        return carry

    rep = lambda v: (v,) * n_heads
    init = (rep(jnp.full((1, TM), -jnp.inf, F32)), rep(jnp.zeros((VROWS, TM), F32)))
    scores(0, 0)
    _, acc = consume(last, 0, lax.fori_loop(0, n_pairs // pairs_per_trip, body, init))
    for pr in range(n_heads // 2):
        outs = [acc[2 * pr + e][:HEAD_DIM] / acc[2 * pr + e][HEAD_DIM:HEAD_DIM + 1] for e in range(2)]
        o_ref[:, pr * LANES:(pr + 1) * LANES] = jnp.concatenate(outs, axis=0).T.astype(BF16)


def _attn_scratch(n_heads):
    return [pltpu.VMEM((2, n_heads, TM, TM), F32), pltpu.VMEM((2, n_heads, 1, TM), F32)]


def _attn_a(qt, k, vta):
    nt = qt.shape[0]
    assert nt % 2 == 1
    n = nt * TM
    gh = A_HEADS // A_KV_HEADS
    kern = functools.partial(_attn_kernel, n_heads=gh, k_slab=(0,) * gh, v_rows=(0,) * gh, n_tiles=nt)
    return pl.pallas_call(
        kern,
        grid=(A_KV_HEADS, nt),
        in_specs=[pl.BlockSpec((1, gh * LANES, TM), lambda g, i: (i, g, 0)),
                  pl.BlockSpec((n, LANES), lambda g, i: (0, B_HEADS)),
                  pl.BlockSpec((nt, VROWS, TM), lambda g, i: (0, g, 0))],
        out_specs=pl.BlockSpec((TM, gh * HEAD_DIM), lambda g, i: (i, g)),
        out_shape=jax.ShapeDtypeStruct((n, A_HEADS * HEAD_DIM), BF16),
        scratch_shapes=_attn_scratch(gh),
        name="attn_gqa",
        compiler_params=_cparams(("arbitrary", "arbitrary")),
    )(qt, k, vta)


def _attn_b(qt, k, vtb):
    nt = qt.shape[0]
    n = nt * TM
    gh = 4
    hs = tuple(range(gh))
    kern = functools.partial(_attn_kernel, n_heads=gh, k_slab=hs, v_rows=hs, n_tiles=nt)
    return pl.pallas_call(
        kern,
        grid=(B_HEADS // gh, nt),
        in_specs=[pl.BlockSpec((1, gh * LANES, TM), lambda j, i: (i, A_HEADS // gh + j, 0)),
                  pl.BlockSpec((n, gh * LANES), lambda j, i: (0, j), pipeline_mode=pl.Buffered(1)),
                  pl.BlockSpec((nt, gh * VROWS, TM), lambda j, i: (0, j, 0), pipeline_mode=pl.Buffered(1))],
        out_specs=pl.BlockSpec((TM, gh * B_VDIM), lambda j, i: (i, j)),
        out_shape=jax.ShapeDtypeStruct((n, B_HEADS * B_VDIM), BF16),
        scratch_shapes=_attn_scratch(gh),
        name="attn_mla",
        compiler_params=_cparams(("arbitrary", "arbitrary")),
    )(qt, k, vtb)


def _out_ab_kernel(t_ref, m_ref, g_ref, oa_ref, ob_ref, w_ref, w1_ref, w3_ref, w2_ref, o_ref):
    m = m_ref[0]
    half = oa_ref.shape[1]
    y = _dot(oa_ref[...], w_ref[:half, :]) + _dot(ob_ref[...], w_ref[half:, :])
    x = t_ref[...] + m[5:6] * y
    o_ref[...] = _ffn_body(x, m, g_ref[2:3], w1_ref, w3_ref, w2_ref, 6)


def _out_ab(t, mod, g, oa, ob, w, w1, w3, w2):
    n, d = t.shape
    return pl.pallas_call(
        _out_ab_kernel,
        grid=(n // TM,),
        in_specs=[_tile_spec(d), _mod_spec(d), _const_spec(g.shape), _tile_spec(oa.shape[1]),
                  _tile_spec(ob.shape[1]), _const_spec(w.shape)] + _ffn_specs(w1, w2, 0, 1),
        out_specs=_tile_spec(d),
        out_shape=jax.ShapeDtypeStruct((n, d), F32),
        name="out_ab_ffn",
        compiler_params=_cparams(("arbitrary",)),
    )(t, mod, g, oa, ob, w, w1, w3, w2)


def _proj_cd_kernel(t_ref, m_ref, g_ref, w1_ref, w3_ref, w2_ref, w_ref, lbp_ref, gq_ref, gk_ref,
                    t_out_ref, hq_ref, gf_ref, gb_ref, hv_ref, hg_ref, dq_ref, dk_ref, dv_ref, *, layer):
    m = m_ref[0]
    t = _ffn_body(t_ref[...], m, g_ref[0:1], w1_ref, w3_ref, w2_ref, 0)
    t_out_ref[...] = t
    h = _rms_mod(t, g_ref[1:2], m[3:4], m[4:5]).astype(BF16)
    p = _dot(h, w_ref[...])
    w = C_HEADS * C_DK
    x = p[:, 0:w]
    hq_ref[...] = x * jax.nn.sigmoid(x)
    lbp = lbp_ref[...]
    e = jnp.exp(lbp - jnp.max(lbp, axis=0, keepdims=True))
    sm = e / jnp.sum(e, axis=0, keepdims=True)
    lb = jnp.sum(sm[1:layer + 1], axis=0)
    for dr, ref in ((0, gf_ref), (1, gb_ref)):
        z = p[:, (1 + dr) * w:(2 + dr) * w]
        lbd = lb[dr:dr + 1]
        ref[...] = jnp.log(lbd + (1.0 - lbd) * jax.nn.sigmoid(z))
    hv_ref[...] = p[:, 3 * w:4 * w].astype(BF16)
    hg_ref[...] = p[:, 4 * w:5 * w]
    for s in range(w // LANES):
        sl = slice(s * LANES, (s + 1) * LANES)
        q = _half_norm(p[:, 5 * w + s * LANES:5 * w + (s + 1) * LANES], gq_ref[...], HEAD_DIM)
        dq_ref[:, sl] = (q * (HEAD_DIM ** -0.5 * LOG2E)).astype(BF16)
        dk_ref[:, sl] = _half_norm(p[:, 6 * w + s * LANES:6 * w + (s + 1) * LANES], gk_ref[...], HEAD_DIM).astype(BF16)
    dv_ref[...] = p[:, 7 * w:8 * w].astype(BF16)


def _proj_cd(t, mod, g, w1, w3, w2, w_in, hgrn_lb, gq, gk, layer):
    n, d = t.shape
    w = C_HEADS * C_DK
    shp = lambda dt: jax.ShapeDtypeStruct((n, w), dt)
    return pl.pallas_call(
        functools.partial(_proj_cd_kernel, layer=layer),
        grid=(n // TM,),
        in_specs=[_tile_spec(d), _mod_spec(d), _const_spec(g.shape)] + _ffn_specs(w1, w2, layer, 0)
                 + [_const_spec(w_in.shape), _const_spec(hgrn_lb.shape), _const_spec(gq.shape), _const_spec(gk.shape)],
        out_specs=[_tile_spec(d)] + [_tile_spec(w)] * 8,
        out_shape=[jax.ShapeDtypeStruct((n, d), F32),
                   shp(F32), shp(F32), shp(F32), shp(BF16), shp(F32), shp(BF16), shp(BF16), shp(BF16)],
        name="ffn_proj_cd",
        compiler_params=_cparams(("arbitrary",)),
    )(t, mod, g, w1, w3, w2, w_in, hgrn_lb, gq, gk)


N_LEVELS = int(math.log2(C_CHUNK))


def _hgrn_tables():
    t = np.arange(C_CHUNK)
    return (t[None, :] <= t[:, None]).astype(np.float32)


def _level_boundary(gi, h, reverse):
    c = gi.shape[0]
    if h >= 4:
        blocks = []
        for s in range(0, c, 2 * h):
            b = s + (h if reverse else h - 1)
            blocks.append(jnp.broadcast_to(gi[b:b + 1], (2 * h, gi.shape[1])))
        return jnp.concatenate(blocks, axis=0)
    pos = lax.broadcasted_iota(jnp.int32, gi.shape, 0) % (2 * h)
    gb = gi
    for p in range(2 * h):
        d = (h if reverse else h - 1) - p
        if d != 0:
            gb = jnp.where(pos == p, pltpu.roll(gi, (c - d) % c, 0), gb)
    return gb


def _level_masks(reverse):
    c = C_CHUNK
    ti = lax.broadcasted_iota(jnp.int32, (2 * c, c), 0) % c
    si = lax.broadcasted_iota(jnp.int32, (2 * c, c), 1)
    tq, sk = (si, ti) if reverse else (ti, si)
    masks = []
    for lv in range(N_LEVELS):
        h = c >> (lv + 1)
        sh = N_LEVELS - lv
        masks.append(((ti >> sh) == (si >> sh)) & ((tq & h) != 0) & ((sk & h) == 0))
    masks.append(ti == si)
    return masks


def _hgrn_chunks(jobs, st_ref):
    c = C_CHUNK
    gis = []
    for q, g, v, tri, masks, d, reverse in jobs:
        ghi = g.astype(BF16)
        glo = (g - ghi.astype(F32)).astype(BF16)
        gis.append(_dot(tri, ghi) + _dot(tri, glo))
    ops = []
    for (q, g, v, tri, masks, d, reverse), gi in zip(jobs, gis):
        f = jnp.exp(g)
        kk = 1.0 - f
        glast = gi[(c - 1):c] if not reverse else gi[0:1]
        qs = (q * jnp.exp(gi)).astype(BF16)
        ks = (kk * jnp.exp(glast - gi)).astype(BF16)
        last_visited = (lax.broadcasted_iota(jnp.int32, g.shape, 0) & 1) == (0 if reverse else 1)
        eqs, eks = [], []
        for lv in range(N_LEVELS):
            if lv == N_LEVELS - 1:
                e = jnp.where(last_visited, f, 1.0)
            else:
                e = jnp.exp(-jnp.abs(gi - _level_boundary(gi, c >> (lv + 1), reverse)))
            eqs.append((e * q).astype(BF16))
            eks.append((e * kk).astype(BF16))
        eqs.append(q.astype(BF16))
        eks.append(kk.astype(BF16))
        ops.append((eqs, eks, qs, ks, jnp.exp(glast)))
    left = _lane((c, LANES)) < C_DK
    blockdiag = (lax.broadcasted_iota(jnp.int32, (LANES, LANES), 0) < C_DV) == (_lane((LANES, LANES)) < C_DK)
    outs = [[] for _ in jobs]
    for pr in range(C_HEADS // 2):
        sl = slice(pr * LANES, (pr + 1) * LANES)
        a2s = [jnp.zeros((2 * c, c), F32) for _ in jobs]
        for lv in range(N_LEVELS + 1):
            for n, job in enumerate(jobs):
                eq = ops[n][0][lv][:, sl]
                eq2 = jnp.concatenate([jnp.where(left, eq, 0), jnp.where(left, 0, eq)], axis=0)
                a = _dot_nt(eq2, ops[n][1][lv][:, sl])
                a2s[n] = jnp.where(job[4][lv], a, a2s[n])
        r2s = [_dot(a2s[n].astype(BF16), job[2][:, sl]) for n, job in enumerate(jobs)]
        upds = [_dot_tn(job[2][:, sl], ops[n][3][:, sl]) for n, job in enumerate(jobs)]
        for n, job in enumerate(jobs):
            d = job[5]
            st = st_ref[d, pr]
            outs[n].append(jnp.where(left, r2s[n][:c], r2s[n][c:]) + _dot_nt(ops[n][2][:, sl], st.astype(BF16)))
            st_ref[d, pr] = st * ops[n][4][:, sl] + jnp.where(blockdiag, upds[n], 0.0)
    return [jnp.concatenate(o, axis=1) for o in outs]


HGRN_CHUNKS_PER_PASS = 4


def _hgrn_kernel(qf_ref, gf_ref, vf_ref, qb_ref, gb_ref, vb_ref, wf_ref, wb_ref, of_ref, ob_ref, st_ref):
    @pl.when(pl.program_id(0) == 0)
    def _():
        st_ref[...] = jnp.zeros(st_ref.shape, F32)

    c = C_CHUNK
    mf, mb = _level_masks(False), _level_masks(True)
    tf, tb = wf_ref[...], wb_ref[...]
    for j0 in range(0, TM // c, HGRN_CHUNKS_PER_PASS):
        js = range(j0, j0 + HGRN_CHUNKS_PER_PASS)
        fs = [slice(j * c, (j + 1) * c) for j in js]
        bs = [slice(TM - (j + 1) * c, TM - j * c) for j in js]
        jobs = ([(qf_ref[s, :], gf_ref[s, :], vf_ref[s, :], tf, mf, 0, False) for s in fs]
                + [(qb_ref[s, :], gb_ref[s, :], vb_ref[s, :], tb, mb, 1, True) for s in bs])
        outs = _hgrn_chunks(jobs, st_ref)
        for s, o in zip(fs, outs[:len(fs)]):
            of_ref[s, :] = o
        for s, o in zip(bs, outs[len(fs):]):
            ob_ref[s, :] = o


def _hgrn(hq, gf, gb, hv):
    n, w = hq.shape
    nt = n // TM
    wf = _hgrn_tables()
    wb = wf.reshape(-1, C_CHUNK, C_CHUNK)[:, ::-1, ::-1].reshape(wf.shape)
    fwd = pl.BlockSpec((TM, w), lambda i: (i, 0))
    bwd = pl.BlockSpec((TM, w), lambda i: (jnp.where(i == 0, 0, nt - i), 0))
    return pl.pallas_call(
        _hgrn_kernel,
        grid=(nt,),
        in_specs=[fwd, fwd, fwd, bwd, bwd, bwd, _const_spec(wf.shape), _const_spec(wb.shape)],
        out_specs=[fwd, bwd],
        out_shape=[jax.ShapeDtypeStruct((n, w), F32)] * 2,
        scratch_shapes=[pltpu.VMEM((2, C_HEADS // 2, LANES, LANES), F32)],
        name="hgrn2",
        compiler_params=_cparams(("arbitrary",)),
    )(hq, gf, hv, hq, gb, hv, jnp.asarray(wf, BF16), jnp.asarray(wb, BF16))


NAT_TILES = 2
NAT_ROWS = NAT_TILES * TM // GRID_W


def _natten_kernel(*refs, rows):
    q_refs, (k_ref, v_ref, bias_ref, o_ref) = refs[:NAT_TILES], refs[NAT_TILES:]
    rows_per_tile = TM // GRID_W
    i = pl.program_id(1)
    kc = k_ref[0:TM, :]
    vc = v_ref[0:TM, :]
    left_q = _lane((GRID_W, LANES)) < HEAD_DIM
    q2s, offs, sws = [], [], []
    for j in range(NAT_ROWS):
        r = i * NAT_ROWS + j
        rs = jnp.clip(r - D_WIN_H // 2, 0, rows - D_WIN_H)
        offs.append(pl.multiple_of(TM + rs * GRID_W, GRID_W))
        jt = j % rows_per_tile
        q = q_refs[j // rows_per_tile][jt * GRID_W:(jt + 1) * GRID_W, :]
        q2s.append(jnp.concatenate([jnp.where(left_q, q, 0), jnp.where(left_q, 0, q)], axis=0))
        kw = k_ref[pl.ds(offs[j], D_WIN_H * GRID_W), :]
        a0 = rs - r + (D_WIN_H - 1)
        bias = jnp.concatenate([bias_ref[0, a0 + 2 * u] for u in range(D_WIN_H // 2)], axis=1)
        sws.append(_dot_nt(q2s[j], kw) + bias)
    sc = _dot_nt(jnp.concatenate(q2s, axis=0), kc)
    pws, pcs, ls = [], [], []
    for j in range(NAT_ROWS):
        scj = sc[j * 2 * GRID_W:(j + 1) * 2 * GRID_W]
        m = jnp.maximum(jnp.max(sws[j], axis=-1, keepdims=True), jnp.max(scj, axis=-1, keepdims=True))
        pw = jnp.exp2(sws[j] - m)
        pc = jnp.exp2(scj - m)
        ls.append(jnp.sum(pw, axis=-1, keepdims=True) + jnp.sum(pc, axis=-1, keepdims=True))
        pws.append(pw.astype(BF16))
        pcs.append(pc.astype(BF16))
    oc = _dot(jnp.concatenate(pcs, axis=0), vc)
    for j in range(NAT_ROWS):
        vw = v_ref[pl.ds(offs[j], D_WIN_H * GRID_W), :]
        o2 = (_dot(pws[j], vw) + oc[j * 2 * GRID_W:(j + 1) * 2 * GRID_W]) / ls[j]
        o_ref[j * GRID_W:(j + 1) * GRID_W, :] = jnp.where(left_q, o2[:GRID_W], o2[GRID_W:]).astype(BF16)


def _natten(dq, dk, dv, bias):
    n, w = dq.shape
    nt = n // TM - 1
    assert nt % NAT_TILES == 0
    q_spec = lambda u: pl.BlockSpec((TM, LANES), lambda p, i: (NAT_TILES * i + u + 1, p))
    return pl.pallas_call(
        functools.partial(_natten_kernel, rows=nt * TM // GRID_W),
        grid=(w // LANES, nt // NAT_TILES),
        in_specs=[q_spec(u) for u in range(NAT_TILES)]
                 + [pl.BlockSpec((n, LANES), lambda p, i: (0, p)),
                    pl.BlockSpec((n, LANES), lambda p, i: (0, p)),
                    pl.BlockSpec((1,) + bias.shape[1:], lambda p, i: (p, 0, 0, 0))],
        out_specs=pl.BlockSpec((NAT_TILES * TM, LANES), lambda p, i: (i, p)),
        out_shape=jax.ShapeDtypeStruct((nt * TM, w), BF16),
        name="natten",
        compiler_params=_cparams(("arbitrary", "arbitrary")),
    )(*([dq] * NAT_TILES), dk, dv, bias)


def _natten_bias(rpb):
    cols = np.arange(GRID_W)
    start = np.clip(cols - D_WIN_W // 2, 0, GRID_W - D_WIN_W)
    kc = cols[None, :]
    inside = (kc >= start[:, None]) & (kc < start[:, None] + D_WIN_W)
    rel = kc - cols[:, None] + (D_WIN_W - 1)
    onehot = (rel[:, :, None] == np.arange(2 * D_WIN_W - 1)).astype(np.float32)
    tab = jnp.einsum("har,ckr->hack", rpb, onehot, precision=lax.Precision.HIGHEST)
    tab = jnp.where(inside[None, None], tab * LOG2E, NEG)
    na = tab.shape[1]
    t = jnp.transpose(tab.reshape(D_HEADS // 2, 2, na, GRID_W, GRID_W), (0, 2, 1, 3, 4))
    t = t.reshape(D_HEADS // 2, na, 2 * GRID_W, GRID_W)
    return jnp.concatenate([t[:, :-1], t[:, 1:]], axis=-1)


def _out_cd_kernel(t_ref, m_ref, g_ref, of_ref, ob_ref, hg_ref, nat_ref, gn_ref, w_ref, w1_ref, w3_ref, w2_ref, o_ref):
    m = m_ref[0]
    half = nat_ref.shape[1]
    o = of_ref[...] + ob_ref[...]
    gate = hg_ref[...]
    gate = gate * jax.nn.sigmoid(gate)
    parts = []
    for s in range(half // LANES):
        sl = slice(s * LANES, (s + 1) * LANES)
        parts.append((_half_norm(o[:, sl], gn_ref[...], C_DV) * gate[:, sl]).astype(BF16))
    y = _dot(jnp.concatenate(parts, axis=1), w_ref[:half, :]) + _dot(nat_ref[...], w_ref[half:, :])
    x = t_ref[...] + m[5:6] * y
    o_ref[...] = _ffn_body(x, m, g_ref[2:3], w1_ref, w3_ref, w2_ref, 6)


def _out_cd(t, mod, g, o_f, o_b, hg, nat, gn, w, w1, w3, w2, layer):
    n, d = t.shape
    nt = n // TM - 1
    wd = nat.shape[1]
    return pl.pallas_call(
        _out_cd_kernel,
        grid=(nt,),
        in_specs=[_tile_spec(d, 1), _mod_spec(d, latent_only=True), _const_spec(g.shape),
                  _tile_spec(wd, 1), _tile_spec(wd, 1), _tile_spec(wd, 1), _tile_spec(wd),
                  _const_spec(gn.shape), _const_spec(w.shape)] + _ffn_specs(w1, w2, layer, 1),
        out_specs=_tile_spec(d),
        out_shape=jax.ShapeDtypeStruct((nt * TM, d), F32),
        name="out_cd_ffn",
        compiler_params=_cparams(("arbitrary",)),
    )(t, mod, g, o_f, o_b, hg, nat, gn, w, w1, w3, w2)


def _rope_tables(n_lat, dim, lane0):
    rows = n_lat // GRID_W
    dh = dim // 2
    q = dim // 4
    sign = jnp.where(jnp.arange(dh) < q, -1.0, 1.0)

    def one(n_pos):
        inv = ROPE_THETA ** (-jnp.arange(0, dh, 2, dtype=F32) / dh)
        ang = jnp.arange(n_pos, dtype=jnp.int32).astype(F32)[:, None] * inv[None, :]
        ang = jnp.concatenate([ang, ang], axis=-1)
        return jnp.cos(ang), jnp.sin(ang) * sign

    (cos_r, sin_r), (cos_c, sin_c) = one(rows), one(GRID_W)
    grid = lambda r, c: jnp.concatenate([jnp.broadcast_to(r[:, None], (rows, GRID_W, dh)),
                                         jnp.broadcast_to(c[None], (rows, GRID_W, dh))], axis=-1).reshape(n_lat, dim)
    cos, sin = grid(cos_r, cos_c), grid(sin_r, sin_c)
    if dim == HEAD_DIM:
        cos, sin = jnp.concatenate([cos, cos], -1), jnp.concatenate([sin, sin], -1)
    else:
        pad = ((0, 0), (lane0, LANES - lane0 - dim))
        cos = jnp.pad(cos, pad, constant_values=1.0)
        sin = jnp.pad(sin, pad)
    cos = jnp.concatenate([jnp.ones((TM, LANES), F32), cos], axis=0)
    sin = jnp.concatenate([jnp.zeros((TM, LANES), F32), sin], axis=0)
    return cos, sin


def _pad_lanes(x, lo, width=LANES):
    return jnp.pad(x, [(0, 0)] * (x.ndim - 1) + [(lo, width - lo - x.shape[-1])])


def _ab_weights(w_in, a_qn, a_kn, b_qn, b_kn, b_kvn, wukv):
    d = w_in.shape[0]
    grp = A_HEADS // A_KV_HEADS
    o = 0
    aq = w_in[:, o:o + A_HEADS * HEAD_DIM].reshape(d, A_HEADS, HEAD_DIM); o += A_HEADS * HEAD_DIM
    ak = w_in[:, o:o + A_KV_HEADS * HEAD_DIM]; o += A_KV_HEADS * HEAD_DIM
    av = w_in[:, o:o + A_KV_HEADS * HEAD_DIM]; o += A_KV_HEADS * HEAD_DIM
    bq = w_in[:, o:o + B_HEADS * B_QK].reshape(d, B_HEADS, B_QK); o += B_HEADS * B_QK
    bkv = w_in[:, o:o + B_KV_RANK]; o += B_KV_RANK
    bkr = w_in[:, o:o + B_ROPE]
    aq_slabs = jnp.concatenate([_pad_lanes(aq[:, h], (h // grp) * HEAD_DIM) for h in range(A_HEADS)], axis=1)
    bq_slabs = _pad_lanes(bq, 0).reshape(d, B_HEADS * LANES)
    half = B_ROPE // 4
    partner = np.where(np.arange(B_ROPE) % (2 * half) < half, np.arange(B_ROPE) + half, np.arange(B_ROPE) - half)
    wk = jnp.concatenate([ak, bkv, _pad_lanes(bkr, B_NOPE), _pad_lanes(bkr[:, partner], B_NOPE)], axis=1).astype(BF16)
    wqv = jnp.concatenate([aq_slabs, bq_slabs, av, bkv], axis=1).T.astype(BF16)
    kvw = wukv.reshape(B_KV_RANK, B_HEADS, B_NOPE + B_VDIM)
    wuk = _pad_lanes(kvw[:, :, :B_NOPE], 0).reshape(B_KV_RANK, B_HEADS * LANES).astype(BF16)
    wuv = kvw[:, :, B_NOPE:].reshape(B_KV_RANK, B_HEADS * B_VDIM).T.astype(BF16)
    gq_a = jnp.concatenate([_pad_lanes(a_qn, (h // grp) * HEAD_DIM) for h in range(A_HEADS)])
    gq_b = jnp.tile(_pad_lanes(b_qn, 0), B_HEADS)
    gq = jnp.concatenate([gq_a * (HEAD_DIM ** -0.5 * LOG2E), gq_b * (B_QK ** -0.5 * LOG2E)])
    gq = jnp.broadcast_to(gq[:, None], (gq.shape[0], TM))
    gk_a = jnp.concatenate([a_kn, a_kn])[None]
    gk_b = jnp.stack([_pad_lanes(b_kn, 0), _pad_lanes(b_kn[B_NOPE:][partner], B_NOPE)])
    gkvt = jnp.broadcast_to(b_kvn[:, None], (B_KV_RANK, TM))
    return [wk, wqv, wuk, wuv, gq, gk_a, gk_b, b_kvn[None], gkvt]


def kernel(x, c, ctx, c_ctx, norm_g, ada_w, ada_b, ffn_w1, ffn_w3, ffn_w2, ab_w_in, ab_a_qn, ab_a_kn, ab_b_qn,
           ab_b_kn, ab_b_kvn, ab_b_wukv, ab_w_out, cd_w_in, hgrn_lb, cd_c_gn, cd_d_qn, cd_d_kn, cd_d_rpb, cd_w_out):
    batch, seq, d = x.shape
    depth = norm_g.shape[0]
    assert batch == 1 and ctx.shape[1] == TM and seq % TM == 0 and depth == 2

    cc = jnp.zeros((8, d), F32).at[0].set(c_ctx).at[1].set(c[0])
    mods = _ada(cc, ada_w, ada_b)[:, :2].reshape(depth, 2, N_MOD, d)
    bf = lambda a: a.astype(BF16)
    w1, w3, w2 = bf(ffn_w1), bf(ffn_w3), bf(ffn_w2)

    consts = _ab_weights(ab_w_in[0], ab_a_qn[0], ab_a_kn[0], ab_b_qn[0], ab_b_kn[0], ab_b_kvn[0], ab_b_wukv[0])
    tables = _rope_tables(seq, HEAD_DIM, 0) + _rope_tables(seq, B_ROPE, B_NOPE)
    t, qt, k, vta, vtb = _first(ctx[0], x[0], mods[0], norm_g[0], w1, w3, w2, consts, tables,
                                tuple(a.T for a in tables))
    oa = _attn_a(qt, k, vta)
    ob = _attn_b(qt, k, vtb)
    t = _out_ab(t, mods[0], norm_g[0], oa, ob, bf(ab_w_out[0]), w1, w3, w2)

    gq = jnp.concatenate([cd_d_qn[0], cd_d_qn[0]])[None]
    gk = jnp.concatenate([cd_d_kn[0], cd_d_kn[0]])[None]
    gn = jnp.concatenate([cd_c_gn[0], cd_c_gn[0]])[None]
    t, hq, gf, gb, hv, hg, dq, dk, dv = _proj_cd(t, mods[1], norm_g[1], w1, w3, w2, bf(cd_w_in[0]), hgrn_lb, gq, gk, 1)
    o_f, o_b = _hgrn(hq, gf, gb, hv)
    nat = _natten(dq, dk, dv, _natten_bias(cd_d_rpb[0]))
    xl = _out_cd(t, mods[1], norm_g[1], o_f, o_b, hg, nat, gn, bf(cd_w_out[0]), w1, w3, w2, 1)
    return xl[None]
```

```python
import functools
import math

import numpy as np
import jax
import jax.numpy as jnp
from jax import lax
from jax.experimental import pallas as pl
from jax.experimental.pallas import tpu as pltpu

F32 = jnp.float32
BF16 = jnp.bfloat16

GRID_W = 64
HEAD_DIM = 64
EPS = 1e-6
ROPE_THETA = 10000.0
N_MOD = 9
A_HEADS, A_KV_HEADS = 8, 2
B_HEADS, B_NOPE, B_ROPE, B_VDIM, B_KV_RANK = 8, 64, 32, 64, 256
B_QK = B_NOPE + B_ROPE
C_HEADS, C_DK, C_DV, C_CHUNK = 8, 64, 64, 64
D_HEADS, D_WIN_H, D_WIN_W = 8, 8, 16

LANES = 128
TM = 256
VMEM_LIMIT = 56 * 1024 * 1024
NEG = -1e30
LOG2E = math.log2(math.e)


def _cparams(sem):
    return pltpu.CompilerParams(dimension_semantics=sem, vmem_limit_bytes=VMEM_LIMIT)


def _const_spec(shape):
    nd = len(shape)
    return pl.BlockSpec(shape, lambda *_: (0,) * nd, pipeline_mode=pl.Buffered(1))


def _lane(shape):
    return lax.broadcasted_iota(jnp.int32, shape, len(shape) - 1)


def _rms_mod(x, g, shift, scale):
    y = x * lax.rsqrt(jnp.mean(x * x, axis=-1, keepdims=True) + EPS) * g
    return y * (1.0 + scale) + shift


def _half_norm(x, gain, n):
    left = _lane(x.shape) < n
    x2 = x * x
    sl = jnp.sum(jnp.where(left, x2, 0.0), axis=-1, keepdims=True)
    sr = jnp.sum(jnp.where(left, 0.0, x2), axis=-1, keepdims=True)
    r = jnp.where(left, lax.rsqrt(sl / n + EPS), lax.rsqrt(sr / n + EPS))
    return x * r * gain


def _rope(x, cos, sin_signed, half):
    first = (_lane(x.shape) % (2 * half)) < half
    w = x.shape[-1]
    partner = jnp.where(first, pltpu.roll(x, w - half, 1), pltpu.roll(x, half, 1))
    return x * cos + partner * sin_signed


def _dot(a, b):
    return jnp.dot(a, b, preferred_element_type=F32)


def _dot_nt(a, b):
    return lax.dot_general(a, b, (((1,), (1,)), ((), ())), preferred_element_type=F32)


def _dot_tn(a, b):
    return lax.dot_general(a, b, (((0,), (0,)), ((), ())), preferred_element_type=F32)


ADA_K_CHUNKS = 4


def _ada_kernel(c_ref, w_ref, b_ref, o_ref):
    c = c_ref[...]
    y = jnp.dot(c * jax.nn.sigmoid(c), w_ref[0], preferred_element_type=F32, precision=lax.Precision.HIGHEST)

    @pl.when(pl.program_id(1) == 0)
    def _():
        o_ref[0] = y + b_ref[0]

    @pl.when(pl.program_id(1) > 0)
    def _():
        o_ref[0] += y


def _ada(cc, ada_w, ada_b):
    depth, d, n = ada_w.shape
    dk = d // ADA_K_CHUNKS
    return pl.pallas_call(
        _ada_kernel,
        grid=(depth, ADA_K_CHUNKS),
        in_specs=[pl.BlockSpec((8, dk), lambda l, j: (0, j)),
                  pl.BlockSpec((1, dk, n), lambda l, j: (l, j, 0)),
                  pl.BlockSpec((1, 1, n), lambda l, j: (l, 0, 0))],
        out_specs=pl.BlockSpec((1, 8, n), lambda l, j: (l, 0, 0)),
        out_shape=jax.ShapeDtypeStruct((depth, 8, n), F32),
        name="ada_mod",
        compiler_params=_cparams(("arbitrary", "arbitrary")),
    )(cc, ada_w, ada_b.reshape(depth, 1, n))


def _ffn_body(x, m, g, w1_ref, w3_ref, w2_ref, base):
    h = _rms_mod(x, g, m[base:base + 1], m[base + 1:base + 2]).astype(BF16)
    a = _dot(h, w1_ref[...])
    b = _dot(h, w3_ref[...])
    u = (a * jax.nn.sigmoid(a) * b).astype(BF16)
    return x + 0.5 * m[base + 2:base + 3] * _dot(u, w2_ref[...])


def _ffn_specs(w1, w2, l, j):
    sel = lambda a: pl.BlockSpec((None, None) + a.shape[2:], lambda *_: (l, j, 0, 0), pipeline_mode=pl.Buffered(1))
    return [sel(w1), sel(w1), sel(w2)]


def _tile_spec(width, first_tile=0):
    return pl.BlockSpec((TM, width), lambda i: (i + first_tile, 0))


def _mod_spec(d, latent_only=False):
    return pl.BlockSpec((1, N_MOD, d), (lambda i: (1, 0, 0)) if latent_only else (lambda i: (jnp.minimum(i, 1), 0, 0)))


BF16_SUBLANES = 16
VROWS = HEAD_DIM + BF16_SUBLANES


def _store_values_t(ref, n_heads, xt):
    for e in range(n_heads):
        r0 = e * VROWS
        ref[0, r0:r0 + HEAD_DIM, :] = xt[e * HEAD_DIM:(e + 1) * HEAD_DIM].astype(BF16)
        ref[0, r0 + HEAD_DIM:r0 + VROWS, :] = jnp.ones((VROWS - HEAD_DIM, xt.shape[1]), BF16)


def _rope_t(x, cos, sin_signed, half, lo, hi):
    blocks = [x[:lo]] if lo else []
    for r in range(lo, hi, 2 * half):
        blocks += [x[r + half:r + 2 * half], x[r:r + half]]
    if hi < x.shape[0]:
        blocks.append(x[hi:])
    return x * cos + jnp.concatenate(blocks, axis=0) * sin_signed

def _proj_ab_body(t, m, g, wk_ref, wqv_ref, wuk_ref, wuv_ref, gq_ref, gk_a_ref, gk_b_ref, gkv_ref, gkvt_ref,
                  cos_a_ref, sin_a_ref, cos_b_ref, sin_b_ref, cost_a_ref, sint_a_ref, cost_b_ref, sint_b_ref,
                  qt_ref, k_ref, vta_ref, vtb_ref):
    hf = _rms_mod(t, g, m[3:4], m[4:5])
    h = hf.astype(BF16)
    ht = hf.T.astype(BF16)
    pk = _dot(h, wk_ref[...])
    c = pk[:, LANES:LANES + B_KV_RANK]
    cn = (c * lax.rsqrt(jnp.mean(c * c, axis=-1, keepdims=True) + EPS) * gkv_ref[...]).astype(BF16)
    kn = _dot(cn, wuk_ref[...])
    pt = _dot(wqv_ref[...], ht)
    nq = A_HEADS + B_HEADS
    cos_a, sin_a = cos_a_ref[...], sin_a_ref[...]
    x = _half_norm(pk[:, :LANES], gk_a_ref[...], HEAD_DIM)
    k_ref[:, B_HEADS * LANES:(B_HEADS + 1) * LANES] = _rope(x, cos_a, sin_a, HEAD_DIM // 4).astype(BF16)
    kr = pk[:, LANES + B_KV_RANK:2 * LANES + B_KV_RANK]
    gcos = gk_b_ref[0:1] * cos_b_ref[...]
    rot = pk[:, 2 * LANES + B_KV_RANK:] * gk_b_ref[1:2] * sin_b_ref[...]
    xs = [kn[:, hd * LANES:(hd + 1) * LANES] + kr for hd in range(B_HEADS)]
    rs = [lax.rsqrt(jnp.sum(x * x, axis=-1, keepdims=True) / B_QK + EPS) for x in xs]
    for hd, (x, r) in enumerate(zip(xs, rs)):
        k_ref[:, hd * LANES:(hd + 1) * LANES] = (r * (x * gcos + rot)).astype(BF16)
    xs = [pt[hd * LANES:(hd + 1) * LANES] for hd in range(nq)]
    rs = [lax.rsqrt(jnp.sum(x * x, axis=0, keepdims=True) / (HEAD_DIM if hd < A_HEADS else B_QK) + EPS)
          for hd, x in enumerate(xs)]
    xs = [x * r * gq_ref[hd * LANES:(hd + 1) * LANES, :] for hd, (x, r) in enumerate(zip(xs, rs))]
    for hd, x in enumerate(xs):
        if hd < A_HEADS:
            x = _rope_t(x, cost_a_ref[...], sint_a_ref[...], HEAD_DIM // 4, 0, LANES)
        else:
            x = _rope_t(x, cost_b_ref[...], sint_b_ref[...], B_ROPE // 4, B_NOPE, B_QK)
        qt_ref[0, hd * LANES:(hd + 1) * LANES, :] = x.astype(BF16)
    _store_values_t(vta_ref, A_KV_HEADS, pt[nq * LANES:(nq + 1) * LANES])
    ct = pt[(nq + 1) * LANES:(nq + 1) * LANES + B_KV_RANK]
    cnt = (ct * lax.rsqrt(jnp.mean(ct * ct, axis=0, keepdims=True) + EPS) * gkvt_ref[...]).astype(BF16)
    _store_values_t(vtb_ref, B_HEADS, _dot(wuv_ref[...], cnt))


def _first_kernel(c_ref, x_ref, m_ref, g_ref, w1_ref, w3_ref, w2_ref, *refs):
    t_ref = refs[-5]
    x = jnp.where(pl.program_id(0) == 0, c_ref[...], x_ref[...])
    m, g = m_ref[0], g_ref[...]
    t = _ffn_body(x, m, g[0:1], w1_ref, w3_ref, w2_ref, 0)
    t_ref[...] = t
    _proj_ab_body(t, m, g[1:2], *refs[:-5], *refs[-4:])


def _first(ctx, x, mod, g, w1, w3, w2, consts, tables, tables_t):
    seq, d = x.shape
    nt = seq // TM + 1
    n = nt * TM
    nq = (A_HEADS + B_HEADS) * LANES
    return pl.pallas_call(
        _first_kernel,
        grid=(nt,),
        in_specs=[pl.BlockSpec((TM, d), lambda i: (0, 0)),
                  pl.BlockSpec((TM, d), lambda i: (jnp.maximum(i - 1, 0), 0)),
                  _mod_spec(d), _const_spec(g.shape)] + _ffn_specs(w1, w2, 0, 0)
                 + [_const_spec(a.shape) for a in consts]
                 + [_tile_spec(LANES)] * len(tables)
                 + [pl.BlockSpec((LANES, TM), lambda i: (0, i))] * len(tables_t),
        out_specs=[_tile_spec(d),
                   pl.BlockSpec((1, nq, TM), lambda i: (i, 0, 0)),
                   _tile_spec((B_HEADS + 1) * LANES),
                   pl.BlockSpec((1, A_KV_HEADS * VROWS, TM), lambda i: (i, 0, 0)),
                   pl.BlockSpec((1, B_HEADS * VROWS, TM), lambda i: (i, 0, 0))],
        out_shape=[jax.ShapeDtypeStruct((n, d), F32),
                   jax.ShapeDtypeStruct((nt, nq, TM), BF16),
                   jax.ShapeDtypeStruct((n, (B_HEADS + 1) * LANES), BF16),
                   jax.ShapeDtypeStruct((nt, A_KV_HEADS * VROWS, TM), BF16),
                   jax.ShapeDtypeStruct((nt, B_HEADS * VROWS, TM), BF16)],
        name="ffn_proj_ab",
        compiler_params=_cparams(("arbitrary",)),
    )(ctx, x, mod, g, w1, w3, w2, *consts, *tables, *tables_t)


def _attn_kernel(qt_ref, k_ref, vt_ref, o_ref, s_ref, mx_ref, *, n_heads, k_slab, v_rows, n_tiles):
    heads = range(n_heads)
    is_ctx = pl.program_id(1) == 0
    n_pairs = jnp.where(is_ctx, 0, (n_tiles - 1) // 2)
    last = jnp.where(is_ctx, 0, n_tiles - 1)

    def scores(kt, slot):
        off = pl.multiple_of(kt * TM, TM)
        for hl in heads:
            r = _dot(k_ref[pl.ds(off, TM), k_slab[hl] * LANES:(k_slab[hl] + 1) * LANES],
                     qt_ref[0, hl * LANES:(hl + 1) * LANES, :])
            s_ref[slot, hl] = r
            mx_ref[slot, hl] = jnp.max(r, axis=0, keepdims=True)

    def consume(kt, slot, carry):
        m, acc = carry
        m_new = tuple(jnp.maximum(m[hl], mx_ref[slot, hl]) for hl in heads)
        alpha = tuple(jnp.exp2(m[hl] - m_new[hl]) for hl in heads)
        new_acc = []
        for hl in heads:
            pm = jnp.exp2(s_ref[slot, hl] - m_new[hl]).astype(BF16)
            new_acc.append(alpha[hl] * acc[hl] + _dot(vt_ref[kt, v_rows[hl] * VROWS:(v_rows[hl] + 1) * VROWS, :], pm))
        return m_new, tuple(new_acc)

    pairs_per_trip = max(u for u in (16, 8, 4, 2, 1) if ((n_tiles - 1) // 2) % u == 0)

    def body(j, carry):
        for u in range(pairs_per_trip):
            t0 = 2 * (j * pairs_per_trip + u)
            scores(t0 + 1, 1)
            carry = consume(t0, 0, carry)
            scores(t0 + 2, 0)
            carry = consume(t0 + 1, 1, carry)
        return carry

    rep = lambda v: (v,) * n_heads
    init = (rep(jnp.full((1, TM), -jnp.inf, F32)), rep(jnp.zeros((VROWS, TM), F32)))
    scores(0, 0)
    _, acc = consume(last, 0, lax.fori_loop(0, n_pairs // pairs_per_trip, body, init))
    for pr in range(n_heads // 2):
        outs = [acc[2 * pr + e][:HEAD_DIM] / acc[2 * pr + e][HEAD_DIM:HEAD_DIM + 1] for e in range(2)]
        o_ref[:, pr * LANES:(pr + 1) * LANES] = jnp.concatenate(outs, axis=0).T.astype(BF16)


def _attn_scratch(n_heads):
    return [pltpu.VMEM((2, n_heads, TM, TM), F32), pltpu.VMEM((2, n_heads, 1, TM), F32)]


def _attn_a(qt, k, vta):
    nt = qt.shape[0]
    assert nt % 2 == 1
    n = nt * TM
    gh = A_HEADS // A_KV_HEADS
    kern = functools.partial(_attn_kernel, n_heads=gh, k_slab=(0,) * gh, v_rows=(0,) * gh, n_tiles=nt)
    return pl.pallas_call(
        kern,
        grid=(A_KV_HEADS, nt),
        in_specs=[pl.BlockSpec((1, gh * LANES, TM), lambda g, i: (i, g, 0)),
                  pl.BlockSpec((n, LANES), lambda g, i: (0, B_HEADS)),
                  pl.BlockSpec((nt, VROWS, TM), lambda g, i: (0, g, 0))],
        out_specs=pl.BlockSpec((TM, gh * HEAD_DIM), lambda g, i: (i, g)),
        out_shape=jax.ShapeDtypeStruct((n, A_HEADS * HEAD_DIM), BF16),
        scratch_shapes=_attn_scratch(gh),
        name="attn_gqa",
        compiler_params=_cparams(("arbitrary", "arbitrary")),
    )(qt, k, vta)


def _attn_b(qt, k, vtb):
    nt = qt.shape[0]
    n = nt * TM
    gh = 4
    hs = tuple(range(gh))
    kern = functools.partial(_attn_kernel, n_heads=gh, k_slab=hs, v_rows=hs, n_tiles=nt)
    return pl.pallas_call(
        kern,
        grid=(B_HEADS // gh, nt),
        in_specs=[pl.BlockSpec((1, gh * LANES, TM), lambda j, i: (i, A_HEADS // gh + j, 0)),
                  pl.BlockSpec((n, gh * LANES), lambda j, i: (0, j), pipeline_mode=pl.Buffered(1)),
                  pl.BlockSpec((nt, gh * VROWS, TM), lambda j, i: (0, j, 0), pipeline_mode=pl.Buffered(1))],
        out_specs=pl.BlockSpec((TM, gh * B_VDIM), lambda j, i: (i, j)),
        out_shape=jax.ShapeDtypeStruct((n, B_HEADS * B_VDIM), BF16),
        scratch_shapes=_attn_scratch(gh),
        name="attn_mla",
        compiler_params=_cparams(("arbitrary", "arbitrary")),
    )(qt, k, vtb)


def _out_ab_kernel(t_ref, m_ref, g_ref, oa_ref, ob_ref, w_ref, w1_ref, w3_ref, w2_ref, o_ref):
    m = m_ref[0]
    half = oa_ref.shape[1]
    y = _dot(oa_ref[...], w_ref[:half, :]) + _dot(ob_ref[...], w_ref[half:, :])
    x = t_ref[...] + m[5:6] * y
    o_ref[...] = _ffn_body(x, m, g_ref[2:3], w1_ref, w3_ref, w2_ref, 6)


def _out_ab(t, mod, g, oa, ob, w, w1, w3, w2):
    n, d = t.shape
    return pl.pallas_call(
        _out_ab_kernel,
        grid=(n // TM,),
        in_specs=[_tile_spec(d), _mod_spec(d), _const_spec(g.shape), _tile_spec(oa.shape[1]),
                  _tile_spec(ob.shape[1]), _const_spec(w.shape)] + _ffn_specs(w1, w2, 0, 1),
        out_specs=_tile_spec(d),
        out_shape=jax.ShapeDtypeStruct((n, d), F32),
        name="out_ab_ffn",
        compiler_params=_cparams(("arbitrary",)),
    )(t, mod, g, oa, ob, w, w1, w3, w2)


def _proj_cd_kernel(t_ref, m_ref, g_ref, w1_ref, w3_ref, w2_ref, w_ref, lbp_ref, gq_ref, gk_ref,
                    t_out_ref, hq_ref, gf_ref, gb_ref, hv_ref, hg_ref, dq_ref, dk_ref, dv_ref, *, layer):
    m = m_ref[0]
    t = _ffn_body(t_ref[...], m, g_ref[0:1], w1_ref, w3_ref, w2_ref, 0)
    t_out_ref[...] = t
    h = _rms_mod(t, g_ref[1:2], m[3:4], m[4:5]).astype(BF16)
    p = _dot(h, w_ref[...])
    w = C_HEADS * C_DK
    x = p[:, 0:w]
    hq_ref[...] = x * jax.nn.sigmoid(x)
    lbp = lbp_ref[...]
    e = jnp.exp(lbp - jnp.max(lbp, axis=0, keepdims=True))
    sm = e / jnp.sum(e, axis=0, keepdims=True)
    lb = jnp.sum(sm[1:layer + 1], axis=0)
    for dr, ref in ((0, gf_ref), (1, gb_ref)):
        z = p[:, (1 + dr) * w:(2 + dr) * w]
        lbd = lb[dr:dr + 1]
        ref[...] = jnp.log(lbd + (1.0 - lbd) * jax.nn.sigmoid(z))
    hv_ref[...] = p[:, 3 * w:4 * w].astype(BF16)
    hg_ref[...] = p[:, 4 * w:5 * w]
    for s in range(w // LANES):
        sl = slice(s * LANES, (s + 1) * LANES)
        q = _half_norm(p[:, 5 * w + s * LANES:5 * w + (s + 1) * LANES], gq_ref[...], HEAD_DIM)
        dq_ref[:, sl] = (q * (HEAD_DIM ** -0.5 * LOG2E)).astype(BF16)
        dk_ref[:, sl] = _half_norm(p[:, 6 * w + s * LANES:6 * w + (s + 1) * LANES], gk_ref[...], HEAD_DIM).astype(BF16)
    dv_ref[...] = p[:, 7 * w:8 * w].astype(BF16)


def _proj_cd(t, mod, g, w1, w3, w2, w_in, hgrn_lb, gq, gk, layer):
    n, d = t.shape
    w = C_HEADS * C_DK
    shp = lambda dt: jax.ShapeDtypeStruct((n, w), dt)
    return pl.pallas_call(
        functools.partial(_proj_cd_kernel, layer=layer),
        grid=(n // TM,),
        in_specs=[_tile_spec(d), _mod_spec(d), _const_spec(g.shape)] + _ffn_specs(w1, w2, layer, 0)
                 + [_const_spec(w_in.shape), _const_spec(hgrn_lb.shape), _const_spec(gq.shape), _const_spec(gk.shape)],
        out_specs=[_tile_spec(d)] + [_tile_spec(w)] * 8,
        out_shape=[jax.ShapeDtypeStruct((n, d), F32),
                   shp(F32), shp(F32), shp(F32), shp(BF16), shp(F32), shp(BF16), shp(BF16), shp(BF16)],
        name="ffn_proj_cd",
        compiler_params=_cparams(("arbitrary",)),
    )(t, mod, g, w1, w3, w2, w_in, hgrn_lb, gq, gk)


N_LEVELS = int(math.log2(C_CHUNK))


def _hgrn_tables():
    t = np.arange(C_CHUNK)
    return (t[None, :] <= t[:, None]).astype(np.float32)


def _level_boundary(gi, h, reverse):
    c = gi.shape[0]
    if h >= 4:
        blocks = []
        for s in range(0, c, 2 * h):
            b = s + (h if reverse else h - 1)
            blocks.append(jnp.broadcast_to(gi[b:b + 1], (2 * h, gi.shape[1])))
        return jnp.concatenate(blocks, axis=0)
    pos = lax.broadcasted_iota(jnp.int32, gi.shape, 0) % (2 * h)
    gb = gi
    for p in range(2 * h):
        d = (h if reverse else h - 1) - p
        if d != 0:
            gb = jnp.where(pos == p, pltpu.roll(gi, (c - d) % c, 0), gb)
    return gb


def _level_masks(reverse):
    c = C_CHUNK
    ti = lax.broadcasted_iota(jnp.int32, (2 * c, c), 0) % c
    si = lax.broadcasted_iota(jnp.int32, (2 * c, c), 1)
    tq, sk = (si, ti) if reverse else (ti, si)
    masks = []
    for lv in range(N_LEVELS):
        h = c >> (lv + 1)
        sh = N_LEVELS - lv
        masks.append(((ti >> sh) == (si >> sh)) & ((tq & h) != 0) & ((sk & h) == 0))
    masks.append(ti == si)
    return masks


def _hgrn_chunks(jobs, st_ref):
    c = C_CHUNK
    gis = []
    for q, g, v, tri, masks, d, reverse in jobs:
        ghi = g.astype(BF16)
        glo = (g - ghi.astype(F32)).astype(BF16)
        gis.append(_dot(tri, ghi) + _dot(tri, glo))
    ops = []
    for (q, g, v, tri, masks, d, reverse), gi in zip(jobs, gis):
        f = jnp.exp(g)
        kk = 1.0 - f
        glast = gi[(c - 1):c] if not reverse else gi[0:1]
        qs = (q * jnp.exp(gi)).astype(BF16)
        ks = (kk * jnp.exp(glast - gi)).astype(BF16)
        last_visited = (lax.broadcasted_iota(jnp.int32, g.shape, 0) & 1) == (0 if reverse else 1)
        eqs, eks = [], []
        for lv in range(N_LEVELS):
            if lv == N_LEVELS - 1:
                e = jnp.where(last_visited, f, 1.0)
            else:
                e = jnp.exp(-jnp.abs(gi - _level_boundary(gi, c >> (lv + 1), reverse)))
            eqs.append((e * q).astype(BF16))
            eks.append((e * kk).astype(BF16))
        eqs.append(q.astype(BF16))
        eks.append(kk.astype(BF16))
        ops.append((eqs, eks, qs, ks, jnp.exp(glast)))
    left = _lane((c, LANES)) < C_DK
    blockdiag = (lax.broadcasted_iota(jnp.int32, (LANES, LANES), 0) < C_DV) == (_lane((LANES, LANES)) < C_DK)
    outs = [[] for _ in jobs]
    for pr in range(C_HEADS // 2):
        sl = slice(pr * LANES, (pr + 1) * LANES)
        a2s = [jnp.zeros((2 * c, c), F32) for _ in jobs]
        for lv in range(N_LEVELS + 1):
            for n, job in enumerate(jobs):
                eq = ops[n][0][lv][:, sl]
                eq2 = jnp.concatenate([jnp.where(left, eq, 0), jnp.where(left, 0, eq)], axis=0)
                a = _dot_nt(eq2, ops[n][1][lv][:, sl])
                a2s[n] = jnp.where(job[4][lv], a, a2s[n])
        r2s = [_dot(a2s[n].astype(BF16), job[2][:, sl]) for n, job in enumerate(jobs)]
        upds = [_dot_tn(job[2][:, sl], ops[n][3][:, sl]) for n, job in enumerate(jobs)]
        for n, job in enumerate(jobs):
            d = job[5]
            st = st_ref[d, pr]
            outs[n].append(jnp.where(left, r2s[n][:c], r2s[n][c:]) + _dot_nt(ops[n][2][:, sl], st.astype(BF16)))
            st_ref[d, pr] = st * ops[n][4][:, sl] + jnp.where(blockdiag, upds[n], 0.0)
    return [jnp.concatenate(o, axis=1) for o in outs]


HGRN_CHUNKS_PER_PASS = 4


def _hgrn_kernel(qf_ref, gf_ref, vf_ref, qb_ref, gb_ref, vb_ref, wf_ref, wb_ref, of_ref, ob_ref, st_ref):
    @pl.when(pl.program_id(0) == 0)
    def _():
        st_ref[...] = jnp.zeros(st_ref.shape, F32)

    c = C_CHUNK
    mf, mb = _level_masks(False), _level_masks(True)
    tf, tb = wf_ref[...], wb_ref[...]
    for j0 in range(0, TM // c, HGRN_CHUNKS_PER_PASS):
        js = range(j0, j0 + HGRN_CHUNKS_PER_PASS)
        fs = [slice(j * c, (j + 1) * c) for j in js]
        bs = [slice(TM - (j + 1) * c, TM - j * c) for j in js]
        jobs = ([(qf_ref[s, :], gf_ref[s, :], vf_ref[s, :], tf, mf, 0, False) for s in fs]
                + [(qb_ref[s, :], gb_ref[s, :], vb_ref[s, :], tb, mb, 1, True) for s in bs])
        outs = _hgrn_chunks(jobs, st_ref)
        for s, o in zip(fs, outs[:len(fs)]):
            of_ref[s, :] = o
        for s, o in zip(bs, outs[len(fs):]):
            ob_ref[s, :] = o


def _hgrn(hq, gf, gb, hv):
    n, w = hq.shape
    nt = n // TM
    wf = _hgrn_tables()
    wb = wf.reshape(-1, C_CHUNK, C_CHUNK)[:, ::-1, ::-1].reshape(wf.shape)
    fwd = pl.BlockSpec((TM, w), lambda i: (i, 0))
    bwd = pl.BlockSpec((TM, w), lambda i: (jnp.where(i == 0, 0, nt - i), 0))
    return pl.pallas_call(
        _hgrn_kernel,
        grid=(nt,),
        in_specs=[fwd, fwd, fwd, bwd, bwd, bwd, _const_spec(wf.shape), _const_spec(wb.shape)],
        out_specs=[fwd, bwd],
        out_shape=[jax.ShapeDtypeStruct((n, w), F32)] * 2,
        scratch_shapes=[pltpu.VMEM((2, C_HEADS // 2, LANES, LANES), F32)],
        name="hgrn2",
        compiler_params=_cparams(("arbitrary",)),
    )(hq, gf, hv, hq, gb, hv, jnp.asarray(wf, BF16), jnp.asarray(wb, BF16))


NAT_TILES = 4
NAT_ROWS = NAT_TILES * TM // GRID_W


def _natten_kernel(*refs, rows):
    q_refs, (k_ref, v_ref, bias_ref, o_ref) = refs[:NAT_TILES], refs[NAT_TILES:]
    rows_per_tile = TM // GRID_W
    i = pl.program_id(1)
    kc = k_ref[0:TM, :]
    vc = v_ref[0:TM, :]
    left_q = _lane((GRID_W, LANES)) < HEAD_DIM
    q2s, offs, sws = [], [], []
    for j in range(NAT_ROWS):
        r = i * NAT_ROWS + j
        rs = jnp.clip(r - D_WIN_H // 2, 0, rows - D_WIN_H)
        offs.append(pl.multiple_of(TM + rs * GRID_W, GRID_W))
        jt = j % rows_per_tile
        q = q_refs[j // rows_per_tile][jt * GRID_W:(jt + 1) * GRID_W, :]
        q2s.append(jnp.concatenate([jnp.where(left_q, q, 0), jnp.where(left_q, 0, q)], axis=0))
        kw = k_ref[pl.ds(offs[j], D_WIN_H * GRID_W), :]
        a0 = rs - r + (D_WIN_H - 1)
        bias = jnp.concatenate([bias_ref[0, a0 + 2 * u] for u in range(D_WIN_H // 2)], axis=1)
        sws.append(_dot_nt(q2s[j], kw) + bias)
    sc = _dot_nt(jnp.concatenate(q2s, axis=0), kc)
    pws, pcs, ls = [], [], []
    for j in range(NAT_ROWS):
        scj = sc[j * 2 * GRID_W:(j + 1) * 2 * GRID_W]
        m = jnp.maximum(jnp.max(sws[j], axis=-1, keepdims=True), jnp.max(scj, axis=-1, keepdims=True))
        pw = jnp.exp2(sws[j] - m)
        pc = jnp.exp2(scj - m)
        ls.append(jnp.sum(pw, axis=-1, keepdims=True) + jnp.sum(pc, axis=-1, keepdims=True))
        pws.append(pw.astype(BF16))
        pcs.append(pc.astype(BF16))
    oc = _dot(jnp.concatenate(pcs, axis=0), vc)
    for j in range(NAT_ROWS):
        vw = v_ref[pl.ds(offs[j], D_WIN_H * GRID_W), :]
        o2 = (_dot(pws[j], vw) + oc[j * 2 * GRID_W:(j + 1) * 2 * GRID_W]) / ls[j]
        o_ref[j * GRID_W:(j + 1) * GRID_W, :] = jnp.where(left_q, o2[:GRID_W], o2[GRID_W:]).astype(BF16)


def _natten(dq, dk, dv, bias):
    n, w = dq.shape
    nt = n // TM - 1
    assert nt % NAT_TILES == 0
    q_spec = lambda u: pl.BlockSpec((TM, LANES), lambda p, i: (NAT_TILES * i + u + 1, p))
    return pl.pallas_call(
        functools.partial(_natten_kernel, rows=nt * TM // GRID_W),
        grid=(w // LANES, nt // NAT_TILES),
        in_specs=[q_spec(u) for u in range(NAT_TILES)]
                 + [pl.BlockSpec((n, LANES), lambda p, i: (0, p)),
                    pl.BlockSpec((n, LANES), lambda p, i: (0, p)),
                    pl.BlockSpec((1,) + bias.shape[1:], lambda p, i: (p, 0, 0, 0))],
        out_specs=pl.BlockSpec((NAT_TILES * TM, LANES), lambda p, i: (i, p)),
        out_shape=jax.ShapeDtypeStruct((nt * TM, w), BF16),
        name="natten",
        compiler_params=_cparams(("arbitrary", "arbitrary")),
    )(*([dq] * NAT_TILES), dk, dv, bias)


def _natten_bias(rpb):
    cols = np.arange(GRID_W)
    start = np.clip(cols - D_WIN_W // 2, 0, GRID_W - D_WIN_W)
    kc = cols[None, :]
    inside = (kc >= start[:, None]) & (kc < start[:, None] + D_WIN_W)
    rel = kc - cols[:, None] + (D_WIN_W - 1)
    onehot = (rel[:, :, None] == np.arange(2 * D_WIN_W - 1)).astype(np.float32)
    tab = jnp.einsum("har,ckr->hack", rpb, onehot, precision=lax.Precision.HIGHEST)
    tab = jnp.where(inside[None, None], tab * LOG2E, NEG)
    na = tab.shape[1]
    t = jnp.transpose(tab.reshape(D_HEADS // 2, 2, na, GRID_W, GRID_W), (0, 2, 1, 3, 4))
    t = t.reshape(D_HEADS // 2, na, 2 * GRID_W, GRID_W)
    return jnp.concatenate([t[:, :-1], t[:, 1:]], axis=-1)


def _out_cd_kernel(t_ref, m_ref, g_ref, of_ref, ob_ref, hg_ref, nat_ref, gn_ref, w_ref, w1_ref, w3_ref, w2_ref, o_ref):
    m = m_ref[0]
    half = nat_ref.shape[1]
    o = of_ref[...] + ob_ref[...]
    gate = hg_ref[...]
    gate = gate * jax.nn.sigmoid(gate)
    parts = []
    for s in range(half // LANES):
        sl = slice(s * LANES, (s + 1) * LANES)
        parts.append((_half_norm(o[:, sl], gn_ref[...], C_DV) * gate[:, sl]).astype(BF16))
    y = _dot(jnp.concatenate(parts, axis=1), w_ref[:half, :]) + _dot(nat_ref[...], w_ref[half:, :])
    x = t_ref[...] + m[5:6] * y
    o_ref[...] = _ffn_body(x, m, g_ref[2:3], w1_ref, w3_ref, w2_ref, 6)


def _out_cd(t, mod, g, o_f, o_b, hg, nat, gn, w, w1, w3, w2, layer):
    n, d = t.shape
    nt = n // TM - 1
    wd = nat.shape[1]
    return pl.pallas_call(
        _out_cd_kernel,
        grid=(nt,),
        in_specs=[_tile_spec(d, 1), _mod_spec(d, latent_only=True), _const_spec(g.shape),
                  _tile_spec(wd, 1), _tile_spec(wd, 1), _tile_spec(wd, 1), _tile_spec(wd),
                  _const_spec(gn.shape), _const_spec(w.shape)] + _ffn_specs(w1, w2, layer, 1),
        out_specs=_tile_spec(d),
        out_shape=jax.ShapeDtypeStruct((nt * TM, d), F32),
        name="out_cd_ffn",
        compiler_params=_cparams(("arbitrary",)),
    )(t, mod, g, o_f, o_b, hg, nat, gn, w, w1, w3, w2)


def _rope_tables(n_lat, dim, lane0):
    rows = n_lat // GRID_W
    dh = dim // 2
    q = dim // 4
    sign = jnp.where(jnp.arange(dh) < q, -1.0, 1.0)

    def one(n_pos):
        inv = ROPE_THETA ** (-jnp.arange(0, dh, 2, dtype=F32) / dh)
        ang = jnp.arange(n_pos, dtype=jnp.int32).astype(F32)[:, None] * inv[None, :]
        ang = jnp.concatenate([ang, ang], axis=-1)
        return jnp.cos(ang), jnp.sin(ang) * sign

    (cos_r, sin_r), (cos_c, sin_c) = one(rows), one(GRID_W)
    grid = lambda r, c: jnp.concatenate([jnp.broadcast_to(r[:, None], (rows, GRID_W, dh)),
                                         jnp.broadcast_to(c[None], (rows, GRID_W, dh))], axis=-1).reshape(n_lat, dim)
    cos, sin = grid(cos_r, cos_c), grid(sin_r, sin_c)
    if dim == HEAD_DIM:
        cos, sin = jnp.concatenate([cos, cos], -1), jnp.concatenate([sin, sin], -1)
    else:
        pad = ((0, 0), (lane0, LANES - lane0 - dim))
        cos = jnp.pad(cos, pad, constant_values=1.0)
        sin = jnp.pad(sin, pad)
    cos = jnp.concatenate([jnp.ones((TM, LANES), F32), cos], axis=0)
    sin = jnp.concatenate([jnp.zeros((TM, LANES), F32), sin], axis=0)
    return cos, sin


def _pad_lanes(x, lo, width=LANES):
    return jnp.pad(x, [(0, 0)] * (x.ndim - 1) + [(lo, width - lo - x.shape[-1])])


def _ab_weights(w_in, a_qn, a_kn, b_qn, b_kn, b_kvn, wukv):
    d = w_in.shape[0]
    grp = A_HEADS // A_KV_HEADS
    o = 0
    aq = w_in[:, o:o + A_HEADS * HEAD_DIM].reshape(d, A_HEADS, HEAD_DIM); o += A_HEADS * HEAD_DIM
    ak = w_in[:, o:o + A_KV_HEADS * HEAD_DIM]; o += A_KV_HEADS * HEAD_DIM
    av = w_in[:, o:o + A_KV_HEADS * HEAD_DIM]; o += A_KV_HEADS * HEAD_DIM
    bq = w_in[:, o:o + B_HEADS * B_QK].reshape(d, B_HEADS, B_QK); o += B_HEADS * B_QK
    bkv = w_in[:, o:o + B_KV_RANK]; o += B_KV_RANK
    bkr = w_in[:, o:o + B_ROPE]
    aq_slabs = jnp.concatenate([_pad_lanes(aq[:, h], (h // grp) * HEAD_DIM) for h in range(A_HEADS)], axis=1)
    bq_slabs = _pad_lanes(bq, 0).reshape(d, B_HEADS * LANES)
    half = B_ROPE // 4
    partner = np.where(np.arange(B_ROPE) % (2 * half) < half, np.arange(B_ROPE) + half, np.arange(B_ROPE) - half)
    wk = jnp.concatenate([ak, bkv, _pad_lanes(bkr, B_NOPE), _pad_lanes(bkr[:, partner], B_NOPE)], axis=1).astype(BF16)
    wqv = jnp.concatenate([aq_slabs, bq_slabs, av, bkv], axis=1).T.astype(BF16)
    kvw = wukv.reshape(B_KV_RANK, B_HEADS, B_NOPE + B_VDIM)
    wuk = _pad_lanes(kvw[:, :, :B_NOPE], 0).reshape(B_KV_RANK, B_HEADS * LANES).astype(BF16)
    wuv = kvw[:, :, B_NOPE:].reshape(B_KV_RANK, B_HEADS * B_VDIM).T.astype(BF16)
    gq_a = jnp.concatenate([_pad_lanes(a_qn, (h // grp) * HEAD_DIM) for h in range(A_HEADS)])
    gq_b = jnp.tile(_pad_lanes(b_qn, 0), B_HEADS)
    gq = jnp.concatenate([gq_a * (HEAD_DIM ** -0.5 * LOG2E), gq_b * (B_QK ** -0.5 * LOG2E)])
    gq = jnp.broadcast_to(gq[:, None], (gq.shape[0], TM))
    gk_a = jnp.concatenate([a_kn, a_kn])[None]
    gk_b = jnp.stack([_pad_lanes(b_kn, 0), _pad_lanes(b_kn[B_NOPE:][partner], B_NOPE)])
    gkvt = jnp.broadcast_to(b_kvn[:, None], (B_KV_RANK, TM))
    return [wk, wqv, wuk, wuv, gq, gk_a, gk_b, b_kvn[None], gkvt]


def kernel(x, c, ctx, c_ctx, norm_g, ada_w, ada_b, ffn_w1, ffn_w3, ffn_w2, ab_w_in, ab_a_qn, ab_a_kn, ab_b_qn,
           ab_b_kn, ab_b_kvn, ab_b_wukv, ab_w_out, cd_w_in, hgrn_lb, cd_c_gn, cd_d_qn, cd_d_kn, cd_d_rpb, cd_w_out):
    batch, seq, d = x.shape
    depth = norm_g.shape[0]
    assert batch == 1 and ctx.shape[1] == TM and seq % TM == 0 and depth == 2

    cc = jnp.zeros((8, d), F32).at[0].set(c_ctx).at[1].set(c[0])
    mods = _ada(cc, ada_w, ada_b)[:, :2].reshape(depth, 2, N_MOD, d)
    bf = lambda a: a.astype(BF16)
    w1, w3, w2 = bf(ffn_w1), bf(ffn_w3), bf(ffn_w2)

    consts = _ab_weights(ab_w_in[0], ab_a_qn[0], ab_a_kn[0], ab_b_qn[0], ab_b_kn[0], ab_b_kvn[0], ab_b_wukv[0])
    tables = _rope_tables(seq, HEAD_DIM, 0) + _rope_tables(seq, B_ROPE, B_NOPE)
    t, qt, k, vta, vtb = _first(ctx[0], x[0], mods[0], norm_g[0], w1, w3, w2, consts, tables,
                                tuple(a.T for a in tables))
    oa = _attn_a(qt, k, vta)
    ob = _attn_b(qt, k, vtb)
    t = _out_ab(t, mods[0], norm_g[0], oa, ob, bf(ab_w_out[0]), w1, w3, w2)

    gq = jnp.concatenate([cd_d_qn[0], cd_d_qn[0]])[None]
    gk = jnp.concatenate([cd_d_kn[0], cd_d_kn[0]])[None]
    gn = jnp.concatenate([cd_c_gn[0], cd_c_gn[0]])[None]
    t, hq, gf, gb, hv, hg, dq, dk, dv = _proj_cd(t, mods[1], norm_g[1], w1, w3, w2, bf(cd_w_in[0]), hgrn_lb, gq, gk, 1)
    o_f, o_b = _hgrn(hq, gf, gb, hv)
    nat = _natten(dq, dk, dv, _natten_bias(cd_d_rpb[0]))
    xl = _out_cd(t, mods[1], norm_g[1], o_f, o_b, hg, nat, gn, bf(cd_w_out[0]), w1, w3, w2, 1)
    return xl[None]
```

```python
import functools
import math

import numpy as np
import jax
import jax.numpy as jnp
from jax import lax
from jax.experimental import pallas as pl
from jax.experimental.pallas import tpu as pltpu

F32 = jnp.float32
BF16 = jnp.bfloat16

GRID_W = 64
HEAD_DIM = 64
EPS = 1e-6
ROPE_THETA = 10000.0
N_MOD = 9
A_HEADS, A_KV_HEADS = 8, 2
B_HEADS, B_NOPE, B_ROPE, B_VDIM, B_KV_RANK = 8, 64, 32, 64, 256
B_QK = B_NOPE + B_ROPE
C_HEADS, C_DK, C_DV, C_CHUNK = 8, 64, 64, 64
D_HEADS, D_WIN_H, D_WIN_W = 8, 8, 16

LANES = 128
TM = 256
VMEM_LIMIT = 56 * 1024 * 1024
NEG = -1e30
LOG2E = math.log2(math.e)


def _cparams(sem):
    return pltpu.CompilerParams(dimension_semantics=sem, vmem_limit_bytes=VMEM_LIMIT)


def _const_spec(shape):
    nd = len(shape)
    return pl.BlockSpec(shape, lambda *_: (0,) * nd, pipeline_mode=pl.Buffered(1))


def _lane(shape):
    return lax.broadcasted_iota(jnp.int32, shape, len(shape) - 1)


def _rms_mod(x, g, shift, scale):
    y = x * lax.rsqrt(jnp.mean(x * x, axis=-1, keepdims=True) + EPS) * g
    return y * (1.0 + scale) + shift


def _half_norm(x, gain, n):
    left = _lane(x.shape) < n
    x2 = x * x
    sl = jnp.sum(jnp.where(left, x2, 0.0), axis=-1, keepdims=True)
    sr = jnp.sum(jnp.where(left, 0.0, x2), axis=-1, keepdims=True)
    r = jnp.where(left, lax.rsqrt(sl / n + EPS), lax.rsqrt(sr / n + EPS))
    return x * r * gain


def _rope(x, cos, sin_signed, half):
    first = (_lane(x.shape) % (2 * half)) < half
    w = x.shape[-1]
    partner = jnp.where(first, pltpu.roll(x, w - half, 1), pltpu.roll(x, half, 1))
    return x * cos + partner * sin_signed


def _dot(a, b):
    return jnp.dot(a, b, preferred_element_type=F32)


def _dot_nt(a, b):
    return lax.dot_general(a, b, (((1,), (1,)), ((), ())), preferred_element_type=F32)


def _dot_tn(a, b):
    return lax.dot_general(a, b, (((0,), (0,)), ((), ())), preferred_element_type=F32)


ADA_K_CHUNKS = 4


def _ada_kernel(c_ref, w_ref, b_ref, o_ref):
    c = c_ref[...]
    y = jnp.dot(c * jax.nn.sigmoid(c), w_ref[0], preferred_element_type=F32, precision=lax.Precision.HIGHEST)

    @pl.when(pl.program_id(1) == 0)
    def _():
        o_ref[0] = y + b_ref[0]

    @pl.when(pl.program_id(1) > 0)
    def _():
        o_ref[0] += y


def _ada(cc, ada_w, ada_b):
    depth, d, n = ada_w.shape
    dk = d // ADA_K_CHUNKS
    return pl.pallas_call(
        _ada_kernel,
        grid=(depth, ADA_K_CHUNKS),
        in_specs=[pl.BlockSpec((8, dk), lambda l, j: (0, j)),
                  pl.BlockSpec((1, dk, n), lambda l, j: (l, j, 0)),
                  pl.BlockSpec((1, 1, n), lambda l, j: (l, 0, 0))],
        out_specs=pl.BlockSpec((1, 8, n), lambda l, j: (l, 0, 0)),
        out_shape=jax.ShapeDtypeStruct((depth, 8, n), F32),
        name="ada_mod",
        compiler_params=_cparams(("arbitrary", "arbitrary")),
    )(cc, ada_w, ada_b.reshape(depth, 1, n))


def _ffn_body(x, m, g, w1_ref, w3_ref, w2_ref, base):
    h = _rms_mod(x, g, m[base:base + 1], m[base + 1:base + 2]).astype(BF16)
    a = _dot(h, w1_ref[...])
    b = _dot(h, w3_ref[...])
    u = (a * jax.nn.sigmoid(a) * b).astype(BF16)
    return x + 0.5 * m[base + 2:base + 3] * _dot(u, w2_ref[...])


def _ffn_specs(w1, w2, l, j):
    sel = lambda a: pl.BlockSpec((None, None) + a.shape[2:], lambda *_: (l, j, 0, 0), pipeline_mode=pl.Buffered(1))
    return [sel(w1), sel(w1), sel(w2)]


def _tile_spec(width, first_tile=0):
    return pl.BlockSpec((TM, width), lambda i: (i + first_tile, 0))


def _mod_spec(d, latent_only=False):
    return pl.BlockSpec((1, N_MOD, d), (lambda i: (1, 0, 0)) if latent_only else (lambda i: (jnp.minimum(i, 1), 0, 0)))


BF16_SUBLANES = 16
VROWS = HEAD_DIM + BF16_SUBLANES


def _store_values_t(ref, n_heads, xt):
    for e in range(n_heads):
        r0 = e * VROWS
        ref[0, r0:r0 + HEAD_DIM, :] = xt[e * HEAD_DIM:(e + 1) * HEAD_DIM].astype(BF16)
        ref[0, r0 + HEAD_DIM:r0 + VROWS, :] = jnp.ones((VROWS - HEAD_DIM, xt.shape[1]), BF16)


def _rope_t(x, cos, sin_signed, half, lo, hi):
    blocks = [x[:lo]] if lo else []
    for r in range(lo, hi, 2 * half):
        blocks += [x[r + half:r + 2 * half], x[r:r + half]]
    if hi < x.shape[0]:
        blocks.append(x[hi:])
    return x * cos + jnp.concatenate(blocks, axis=0) * sin_signed

def _proj_ab_body(t, m, g, wk_ref, wqv_ref, wuk_ref, wuv_ref, gq_ref, gk_a_ref, gk_b_ref, gkv_ref, gkvt_ref,
                  cos_a_ref, sin_a_ref, cos_b_ref, sin_b_ref, cost_a_ref, sint_a_ref, cost_b_ref, sint_b_ref,
                  qt_ref, k_ref, vta_ref, vtb_ref):
    hf = _rms_mod(t, g, m[3:4], m[4:5])
    h = hf.astype(BF16)
    ht = hf.T.astype(BF16)
    pk = _dot(h, wk_ref[...])
    c = pk[:, LANES:LANES + B_KV_RANK]
    cn = (c * lax.rsqrt(jnp.mean(c * c, axis=-1, keepdims=True) + EPS) * gkv_ref[...]).astype(BF16)
    kn = _dot(cn, wuk_ref[...])
    nq = A_HEADS + B_HEADS
    pt_v = _dot(wqv_ref[nq * LANES:, :], ht)
    pt_q = [_dot(wqv_ref[lo * LANES:hi * LANES, :], ht) for lo, hi in ((0, A_HEADS), (A_HEADS, nq))]
    cos_a, sin_a = cos_a_ref[...], sin_a_ref[...]
    x = _half_norm(pk[:, :LANES], gk_a_ref[...], HEAD_DIM)
    k_ref[:, B_HEADS * LANES:(B_HEADS + 1) * LANES] = _rope(x, cos_a, sin_a, HEAD_DIM // 4).astype(BF16)
    kr = pk[:, LANES + B_KV_RANK:2 * LANES + B_KV_RANK]
    gcos = gk_b_ref[0:1] * cos_b_ref[...]
    rot = pk[:, 2 * LANES + B_KV_RANK:] * gk_b_ref[1:2] * sin_b_ref[...]
    xs = [kn[:, hd * LANES:(hd + 1) * LANES] + kr for hd in range(B_HEADS)]
    rs = [lax.rsqrt(jnp.sum(x * x, axis=-1, keepdims=True) / B_QK + EPS) for x in xs]
    for hd, (x, r) in enumerate(zip(xs, rs)):
        k_ref[:, hd * LANES:(hd + 1) * LANES] = (r * (x * gcos + rot)).astype(BF16)
    _store_values_t(vta_ref, A_KV_HEADS, pt_v[:LANES])
    ct = pt_v[LANES:]
    cnt = (ct * lax.rsqrt(jnp.mean(ct * ct, axis=0, keepdims=True) + EPS) * gkvt_ref[...]).astype(BF16)
    _store_values_t(vtb_ref, B_HEADS, _dot(wuv_ref[...], cnt))
    for grp, (hd0, n_real) in enumerate(((0, HEAD_DIM), (A_HEADS, B_QK))):
        xs = [pt_q[grp][j * LANES:(j + 1) * LANES] for j in range(pt_q[grp].shape[0] // LANES)]
        rs = [lax.rsqrt(jnp.sum(x * x, axis=0, keepdims=True) / n_real + EPS) for x in xs]
        xs = [x * r * gq_ref[(hd0 + j) * LANES:(hd0 + j + 1) * LANES, :] for j, (x, r) in enumerate(zip(xs, rs))]
        for j, x in enumerate(xs):
            if grp == 0:
                x = _rope_t(x, cost_a_ref[...], sint_a_ref[...], HEAD_DIM // 4, 0, LANES)
            else:
                x = _rope_t(x, cost_b_ref[...], sint_b_ref[...], B_ROPE // 4, B_NOPE, B_QK)
            qt_ref[0, (hd0 + j) * LANES:(hd0 + j + 1) * LANES, :] = x.astype(BF16)


def _first_kernel(c_ref, x_ref, m_ref, g_ref, w1_ref, w3_ref, w2_ref, *refs):
    t_ref = refs[-5]
    x = jnp.where(pl.program_id(0) == 0, c_ref[...], x_ref[...])
    m, g = m_ref[0], g_ref[...]
    t = _ffn_body(x, m, g[0:1], w1_ref, w3_ref, w2_ref, 0)
    t_ref[...] = t
    _proj_ab_body(t, m, g[1:2], *refs[:-5], *refs[-4:])


def _first(ctx, x, mod, g, w1, w3, w2, consts, tables, tables_t):
    seq, d = x.shape
    nt = seq // TM + 1
    n = nt * TM
    nq = (A_HEADS + B_HEADS) * LANES
    return pl.pallas_call(
        _first_kernel,
        grid=(nt,),
        in_specs=[pl.BlockSpec((TM, d), lambda i: (0, 0)),
                  pl.BlockSpec((TM, d), lambda i: (jnp.maximum(i - 1, 0), 0)),
                  _mod_spec(d), _const_spec(g.shape)] + _ffn_specs(w1, w2, 0, 0)
                 + [_const_spec(a.shape) for a in consts]
                 + [_tile_spec(LANES)] * len(tables)
                 + [pl.BlockSpec((LANES, TM), lambda i: (0, i))] * len(tables_t),
        out_specs=[_tile_spec(d),
                   pl.BlockSpec((1, nq, TM), lambda i: (i, 0, 0)),
                   _tile_spec((B_HEADS + 1) * LANES),
                   pl.BlockSpec((1, A_KV_HEADS * VROWS, TM), lambda i: (i, 0, 0)),
                   pl.BlockSpec((1, B_HEADS * VROWS, TM), lambda i: (i, 0, 0))],
        out_shape=[jax.ShapeDtypeStruct((n, d), F32),
                   jax.ShapeDtypeStruct((nt, nq, TM), BF16),
                   jax.ShapeDtypeStruct((n, (B_HEADS + 1) * LANES), BF16),
                   jax.ShapeDtypeStruct((nt, A_KV_HEADS * VROWS, TM), BF16),
                   jax.ShapeDtypeStruct((nt, B_HEADS * VROWS, TM), BF16)],
        name="ffn_proj_ab",
        compiler_params=_cparams(("arbitrary",)),
    )(ctx, x, mod, g, w1, w3, w2, *consts, *tables, *tables_t)


def _attn_kernel(qt_ref, k_ref, vt_ref, o_ref, s_ref, mx_ref, *, n_heads, k_slab, v_rows, n_tiles):
    heads = range(n_heads)
    is_ctx = pl.program_id(1) == 0
    n_pairs = jnp.where(is_ctx, 0, (n_tiles - 1) // 2)
    last = jnp.where(is_ctx, 0, n_tiles - 1)

    def scores(kt, slot):
        off = pl.multiple_of(kt * TM, TM)
        for hl in heads:
            r = _dot(k_ref[pl.ds(off, TM), k_slab[hl] * LANES:(k_slab[hl] + 1) * LANES],
                     qt_ref[0, hl * LANES:(hl + 1) * LANES, :])
            s_ref[slot, hl] = r
            mx_ref[slot, hl] = jnp.max(r, axis=0, keepdims=True)

    def consume(kt, slot, carry):
        m, acc = carry
        m_new = tuple(jnp.maximum(m[hl], mx_ref[slot, hl]) for hl in heads)
        alpha = tuple(jnp.exp2(m[hl] - m_new[hl]) for hl in heads)
        new_acc = []
        for hl in heads:
            pm = jnp.exp2(s_ref[slot, hl] - m_new[hl]).astype(BF16)
            new_acc.append(alpha[hl] * acc[hl] + _dot(vt_ref[kt, v_rows[hl] * VROWS:(v_rows[hl] + 1) * VROWS, :], pm))
        return m_new, tuple(new_acc)

    pairs_per_trip = max(u for u in (16, 8, 4, 2, 1) if ((n_tiles - 1) // 2) % u == 0)

    def body(j, carry):
        for u in range(pairs_per_trip):
            t0 = 2 * (j * pairs_per_trip + u)
            scores(t0 + 1, 1)
            carry = consume(t0, 0, carry)
            scores(t0 + 2, 0)
            carry = consume(t0 + 1, 1, carry)
        return carry

    rep = lambda v: (v,) * n_heads
    init = (rep(jnp.full((1, TM), -jnp.inf, F32)), rep(jnp.zeros((VROWS, TM), F32)))
    scores(0, 0)
    _, acc = consume(last, 0, lax.fori_loop(0, n_pairs // pairs_per_trip, body, init))
    for pr in range(n_heads // 2):
        outs = [acc[2 * pr + e][:HEAD_DIM] / acc[2 * pr + e][HEAD_DIM:HEAD_DIM + 1] for e in range(2)]
        o_ref[:, pr * LANES:(pr + 1) * LANES] = jnp.concatenate(outs, axis=0).T.astype(BF16)


def _attn_scratch(n_heads):
    return [pltpu.VMEM((2, n_heads, TM, TM), F32), pltpu.VMEM((2, n_heads, 1, TM), F32)]


def _attn_a(qt, k, vta):
    nt = qt.shape[0]
    assert nt % 2 == 1
    n = nt * TM
    gh = A_HEADS // A_KV_HEADS
    kern = functools.partial(_attn_kernel, n_heads=gh, k_slab=(0,) * gh, v_rows=(0,) * gh, n_tiles=nt)
    return pl.pallas_call(
        kern,
        grid=(A_KV_HEADS, nt),
        in_specs=[pl.BlockSpec((1, gh * LANES, TM), lambda g, i: (i, g, 0)),
                  pl.BlockSpec((n, LANES), lambda g, i: (0, B_HEADS)),
                  pl.BlockSpec((nt, VROWS, TM), lambda g, i: (0, g, 0))],
        out_specs=pl.BlockSpec((TM, gh * HEAD_DIM), lambda g, i: (i, g)),
        out_shape=jax.ShapeDtypeStruct((n, A_HEADS * HEAD_DIM), BF16),
        scratch_shapes=_attn_scratch(gh),
        name="attn_gqa",
        compiler_params=_cparams(("arbitrary", "arbitrary")),
    )(qt, k, vta)


def _attn_b(qt, k, vtb):
    nt = qt.shape[0]
    n = nt * TM
    gh = 4
    hs = tuple(range(gh))
    kern = functools.partial(_attn_kernel, n_heads=gh, k_slab=hs, v_rows=hs, n_tiles=nt)
    return pl.pallas_call(
        kern,
        grid=(B_HEADS // gh, nt),
        in_specs=[pl.BlockSpec((1, gh * LANES, TM), lambda j, i: (i, A_HEADS // gh + j, 0)),
                  pl.BlockSpec((n, gh * LANES), lambda j, i: (0, j), pipeline_mode=pl.Buffered(1)),
                  pl.BlockSpec((nt, gh * VROWS, TM), lambda j, i: (0, j, 0), pipeline_mode=pl.Buffered(1))],
        out_specs=pl.BlockSpec((TM, gh * B_VDIM), lambda j, i: (i, j)),
        out_shape=jax.ShapeDtypeStruct((n, B_HEADS * B_VDIM), BF16),
        scratch_shapes=_attn_scratch(gh),
        name="attn_mla",
        compiler_params=_cparams(("arbitrary", "arbitrary")),
    )(qt, k, vtb)


def _out_ab_kernel(t_ref, m_ref, g_ref, oa_ref, ob_ref, w_ref, w1_ref, w3_ref, w2_ref, o_ref):
    m = m_ref[0]
    half = oa_ref.shape[1]
    y = _dot(oa_ref[...], w_ref[:half, :]) + _dot(ob_ref[...], w_ref[half:, :])
    x = t_ref[...] + m[5:6] * y
    o_ref[...] = _ffn_body(x, m, g_ref[2:3], w1_ref, w3_ref, w2_ref, 6)


def _out_ab(t, mod, g, oa, ob, w, w1, w3, w2):
    n, d = t.shape
    return pl.pallas_call(
        _out_ab_kernel,
        grid=(n // TM,),
        in_specs=[_tile_spec(d), _mod_spec(d), _const_spec(g.shape), _tile_spec(oa.shape[1]),
                  _tile_spec(ob.shape[1]), _const_spec(w.shape)] + _ffn_specs(w1, w2, 0, 1),
        out_specs=_tile_spec(d),
        out_shape=jax.ShapeDtypeStruct((n, d), F32),
        name="out_ab_ffn",
        compiler_params=_cparams(("arbitrary",)),
    )(t, mod, g, oa, ob, w, w1, w3, w2)


def _proj_cd_kernel(t_ref, m_ref, g_ref, w1_ref, w3_ref, w2_ref, w_ref, lbp_ref, gq_ref, gk_ref,
                    t_out_ref, hq_ref, gf_ref, gb_ref, hv_ref, hg_ref, dq_ref, dk_ref, dv_ref, *, layer):
    m = m_ref[0]
    t = _ffn_body(t_ref[...], m, g_ref[0:1], w1_ref, w3_ref, w2_ref, 0)
    t_out_ref[...] = t
    h = _rms_mod(t, g_ref[1:2], m[3:4], m[4:5]).astype(BF16)
    p = _dot(h, w_ref[...])
    w = C_HEADS * C_DK
    x = p[:, 0:w]
    hq_ref[...] = x * jax.nn.sigmoid(x)
    lbp = lbp_ref[...]
    e = jnp.exp(lbp - jnp.max(lbp, axis=0, keepdims=True))
    sm = e / jnp.sum(e, axis=0, keepdims=True)
    lb = jnp.sum(sm[1:layer + 1], axis=0)
    for dr, ref in ((0, gf_ref), (1, gb_ref)):
        z = p[:, (1 + dr) * w:(2 + dr) * w]
        lbd = lb[dr:dr + 1]
        ref[...] = jnp.log2(lbd + (1.0 - lbd) * jax.nn.sigmoid(z))
    hv_ref[...] = p[:, 3 * w:4 * w].astype(BF16)
    hg_ref[...] = p[:, 4 * w:5 * w]
    for s in range(w // LANES):
        sl = slice(s * LANES, (s + 1) * LANES)
        q = _half_norm(p[:, 5 * w + s * LANES:5 * w + (s + 1) * LANES], gq_ref[...], HEAD_DIM)
        dq_ref[:, sl] = (q * (HEAD_DIM ** -0.5 * LOG2E)).astype(BF16)
        dk_ref[:, sl] = _half_norm(p[:, 6 * w + s * LANES:6 * w + (s + 1) * LANES], gk_ref[...], HEAD_DIM).astype(BF16)
    dv_ref[...] = p[:, 7 * w:8 * w].astype(BF16)


def _proj_cd(t, mod, g, w1, w3, w2, w_in, hgrn_lb, gq, gk, layer):
    n, d = t.shape
    w = C_HEADS * C_DK
    shp = lambda dt: jax.ShapeDtypeStruct((n, w), dt)
    return pl.pallas_call(
        functools.partial(_proj_cd_kernel, layer=layer),
        grid=(n // TM,),
        in_specs=[_tile_spec(d), _mod_spec(d), _const_spec(g.shape)] + _ffn_specs(w1, w2, layer, 0)
                 + [_const_spec(w_in.shape), _const_spec(hgrn_lb.shape), _const_spec(gq.shape), _const_spec(gk.shape)],
        out_specs=[_tile_spec(d)] + [_tile_spec(w)] * 8,
        out_shape=[jax.ShapeDtypeStruct((n, d), F32),
                   shp(F32), shp(F32), shp(F32), shp(BF16), shp(F32), shp(BF16), shp(BF16), shp(BF16)],
        name="ffn_proj_cd",
        compiler_params=_cparams(("arbitrary",)),
    )(t, mod, g, w1, w3, w2, w_in, hgrn_lb, gq, gk)


N_LEVELS = int(math.log2(C_CHUNK))


def _hgrn_tables():
    t = np.arange(C_CHUNK)
    return (t[None, :] <= t[:, None]).astype(np.float32)


def _level_boundary(gi, h, reverse):
    c = gi.shape[0]
    if h >= 4:
        blocks = []
        for s in range(0, c, 2 * h):
            b = s + (h if reverse else h - 1)
            blocks.append(jnp.broadcast_to(gi[b:b + 1], (2 * h, gi.shape[1])))
        return jnp.concatenate(blocks, axis=0)
    pos = lax.broadcasted_iota(jnp.int32, gi.shape, 0) % (2 * h)
    gb = gi
    for p in range(2 * h):
        d = (h if reverse else h - 1) - p
        if d != 0:
            gb = jnp.where(pos == p, pltpu.roll(gi, (c - d) % c, 0), gb)
    return gb


def _level_masks(reverse):
    c = C_CHUNK
    ti = lax.broadcasted_iota(jnp.int32, (2 * c, c), 0) % c
    si = lax.broadcasted_iota(jnp.int32, (2 * c, c), 1)
    tq, sk = (si, ti) if reverse else (ti, si)
    masks = []
    for lv in range(N_LEVELS):
        h = c >> (lv + 1)
        sh = N_LEVELS - lv
        masks.append(((ti >> sh) == (si >> sh)) & ((tq & h) != 0) & ((sk & h) == 0))
    masks.append(ti == si)
    return masks


def _hgrn_chunks(jobs, st_ref):
    c = C_CHUNK
    gis = []
    for q, g, v, tri, masks, d, reverse in jobs:
        ghi = g.astype(BF16)
        glo = (g - ghi.astype(F32)).astype(BF16)
        gis.append(_dot(tri, ghi) + _dot(tri, glo))
    ops = []
    for (q, g, v, tri, masks, d, reverse), gi in zip(jobs, gis):
        f = jnp.exp2(g)
        kk = 1.0 - f
        glast = gi[(c - 1):c] if not reverse else gi[0:1]
        qs = (q * jnp.exp2(gi)).astype(BF16)
        ks = (kk * jnp.exp2(glast - gi)).astype(BF16)
        last_visited = (lax.broadcasted_iota(jnp.int32, g.shape, 0) & 1) == (0 if reverse else 1)
        eqs, eks = [], []
        for lv in range(N_LEVELS):
            if lv == N_LEVELS - 1:
                e = jnp.where(last_visited, f, 1.0)
            else:
                e = jnp.exp2(-jnp.abs(gi - _level_boundary(gi, c >> (lv + 1), reverse)))
            eqs.append((e * q).astype(BF16))
            eks.append((e * kk).astype(BF16))
        eqs.append(q.astype(BF16))
        eks.append(kk.astype(BF16))
        ops.append((eqs, eks, qs, ks, jnp.exp2(glast)))
    left = _lane((c, LANES)) < C_DK
    blockdiag = (lax.broadcasted_iota(jnp.int32, (LANES, LANES), 0) < C_DV) == (_lane((LANES, LANES)) < C_DK)
    outs = [[] for _ in jobs]
    for pr in range(C_HEADS // 2):
        sl = slice(pr * LANES, (pr + 1) * LANES)
        a2s = [jnp.zeros((2 * c, c), F32) for _ in jobs]
        for lv in range(N_LEVELS + 1):
            for n, job in enumerate(jobs):
                eq = ops[n][0][lv][:, sl]
                eq2 = jnp.concatenate([jnp.where(left, eq, 0), jnp.where(left, 0, eq)], axis=0)
                a = _dot_nt(eq2, ops[n][1][lv][:, sl])
                a2s[n] = jnp.where(job[4][lv], a, a2s[n])
        r2s = [_dot(a2s[n].astype(BF16), job[2][:, sl]) for n, job in enumerate(jobs)]
        upds = [_dot_tn(job[2][:, sl], ops[n][3][:, sl]) for n, job in enumerate(jobs)]
        for n, job in enumerate(jobs):
            d = job[5]
            st = st_ref[d, pr]
            outs[n].append(jnp.where(left, r2s[n][:c], r2s[n][c:]) + _dot_nt(ops[n][2][:, sl], st.astype(BF16)))
            st_ref[d, pr] = st * ops[n][4][:, sl] + jnp.where(blockdiag, upds[n], 0.0)
    return [jnp.concatenate(o, axis=1) for o in outs]


HGRN_CHUNKS_PER_PASS = 4


def _hgrn_kernel(qf_ref, gf_ref, vf_ref, qb_ref, gb_ref, vb_ref, wf_ref, wb_ref, of_ref, ob_ref, st_ref):
    @pl.when(pl.program_id(0) == 0)
    def _():
        st_ref[...] = jnp.zeros(st_ref.shape, F32)

    c = C_CHUNK
    mf, mb = _level_masks(False), _level_masks(True)
    tf, tb = wf_ref[...], wb_ref[...]
    for j0 in range(0, TM // c, HGRN_CHUNKS_PER_PASS):
        js = range(j0, j0 + HGRN_CHUNKS_PER_PASS)
        fs = [slice(j * c, (j + 1) * c) for j in js]
        bs = [slice(TM - (j + 1) * c, TM - j * c) for j in js]
        jobs = ([(qf_ref[s, :], gf_ref[s, :], vf_ref[s, :], tf, mf, 0, False) for s in fs]
                + [(qb_ref[s, :], gb_ref[s, :], vb_ref[s, :], tb, mb, 1, True) for s in bs])
        outs = _hgrn_chunks(jobs, st_ref)
        for s, o in zip(fs, outs[:len(fs)]):
            of_ref[s, :] = o
        for s, o in zip(bs, outs[len(fs):]):
            ob_ref[s, :] = o


def _hgrn(hq, gf, gb, hv):
    n, w = hq.shape
    nt = n // TM
    wf = _hgrn_tables()
    wb = wf.reshape(-1, C_CHUNK, C_CHUNK)[:, ::-1, ::-1].reshape(wf.shape)
    fwd = pl.BlockSpec((TM, w), lambda i: (i, 0))
    bwd = pl.BlockSpec((TM, w), lambda i: (jnp.where(i == 0, 0, nt - i), 0))
    return pl.pallas_call(
        _hgrn_kernel,
        grid=(nt,),
        in_specs=[fwd, fwd, fwd, bwd, bwd, bwd, _const_spec(wf.shape), _const_spec(wb.shape)],
        out_specs=[fwd, bwd],
        out_shape=[jax.ShapeDtypeStruct((n, w), F32)] * 2,
        scratch_shapes=[pltpu.VMEM((2, C_HEADS // 2, LANES, LANES), F32)],
        name="hgrn2",
        compiler_params=_cparams(("arbitrary",)),
    )(hq, gf, hv, hq, gb, hv, jnp.asarray(wf, BF16), jnp.asarray(wb, BF16))


NAT_TILES = 4
NAT_ROWS = NAT_TILES * TM // GRID_W


def _natten_kernel(*refs, rows):
    q_refs, (k_ref, v_ref, bias_ref, o_ref) = refs[:NAT_TILES], refs[NAT_TILES:]
    rows_per_tile = TM // GRID_W
    i = pl.program_id(1)
    kc = k_ref[0:TM, :]
    vc = v_ref[0:TM, :]
    left_q = _lane((GRID_W, LANES)) < HEAD_DIM
    q2s, offs, sws = [], [], []
    for j in range(NAT_ROWS):
        r = i * NAT_ROWS + j
        rs = jnp.clip(r - D_WIN_H // 2, 0, rows - D_WIN_H)
        offs.append(pl.multiple_of(TM + rs * GRID_W, GRID_W))
        jt = j % rows_per_tile
        q = q_refs[j // rows_per_tile][jt * GRID_W:(jt + 1) * GRID_W, :]
        q2s.append(jnp.concatenate([jnp.where(left_q, q, 0), jnp.where(left_q, 0, q)], axis=0))
        kw = k_ref[pl.ds(offs[j], D_WIN_H * GRID_W), :]
        a0 = rs - r + (D_WIN_H - 1)
        bias = jnp.concatenate([bias_ref[0, a0 + 2 * u] for u in range(D_WIN_H // 2)], axis=1)
        sws.append(_dot_nt(q2s[j], kw) + bias)
    sc = _dot_nt(jnp.concatenate(q2s, axis=0), kc)
    pws, pcs, ls = [], [], []
    for j in range(NAT_ROWS):
        scj = sc[j * 2 * GRID_W:(j + 1) * 2 * GRID_W]
        m = jnp.maximum(jnp.max(sws[j], axis=-1, keepdims=True), jnp.max(scj, axis=-1, keepdims=True))
        pw = jnp.exp2(sws[j] - m)
        pc = jnp.exp2(scj - m)
        ls.append(jnp.sum(pw, axis=-1, keepdims=True) + jnp.sum(pc, axis=-1, keepdims=True))
        pws.append(pw.astype(BF16))
        pcs.append(pc.astype(BF16))
    oc = _dot(jnp.concatenate(pcs, axis=0), vc)
    for j in range(NAT_ROWS):
        vw = v_ref[pl.ds(offs[j], D_WIN_H * GRID_W), :]
        o2 = (_dot(pws[j], vw) + oc[j * 2 * GRID_W:(j + 1) * 2 * GRID_W]) / ls[j]
        o_ref[j * GRID_W:(j + 1) * GRID_W, :] = jnp.where(left_q, o2[:GRID_W], o2[GRID_W:]).astype(BF16)


def _natten(dq, dk, dv, bias):
    n, w = dq.shape
    nt = n // TM - 1
    assert nt % NAT_TILES == 0
    q_spec = lambda u: pl.BlockSpec((TM, LANES), lambda p, i: (NAT_TILES * i + u + 1, p))
    return pl.pallas_call(
        functools.partial(_natten_kernel, rows=nt * TM // GRID_W),
        grid=(w // LANES, nt // NAT_TILES),
        in_specs=[q_spec(u) for u in range(NAT_TILES)]
                 + [pl.BlockSpec((n, LANES), lambda p, i: (0, p)),
                    pl.BlockSpec((n, LANES), lambda p, i: (0, p)),
                    pl.BlockSpec((1,) + bias.shape[1:], lambda p, i: (p, 0, 0, 0))],
        out_specs=pl.BlockSpec((NAT_TILES * TM, LANES), lambda p, i: (i, p)),
        out_shape=jax.ShapeDtypeStruct((nt * TM, w), BF16),
        name="natten",
        compiler_params=_cparams(("arbitrary", "arbitrary")),
    )(*([dq] * NAT_TILES), dk, dv, bias)


def _natten_bias(rpb):
    cols = np.arange(GRID_W)
    start = np.clip(cols - D_WIN_W // 2, 0, GRID_W - D_WIN_W)
    kc = cols[None, :]
    inside = (kc >= start[:, None]) & (kc < start[:, None] + D_WIN_W)
    rel = kc - cols[:, None] + (D_WIN_W - 1)
    onehot = (rel[:, :, None] == np.arange(2 * D_WIN_W - 1)).astype(np.float32)
    tab = jnp.einsum("har,ckr->hack", rpb, onehot, precision=lax.Precision.HIGHEST)
    tab = jnp.where(inside[None, None], tab * LOG2E, NEG)
    na = tab.shape[1]
    t = jnp.transpose(tab.reshape(D_HEADS // 2, 2, na, GRID_W, GRID_W), (0, 2, 1, 3, 4))
    t = t.reshape(D_HEADS // 2, na, 2 * GRID_W, GRID_W)
    return jnp.concatenate([t[:, :-1], t[:, 1:]], axis=-1)


def _out_cd_kernel(t_ref, m_ref, g_ref, of_ref, ob_ref, hg_ref, nat_ref, gn_ref, w_ref, w1_ref, w3_ref, w2_ref, o_ref):
    m = m_ref[0]
    half = nat_ref.shape[1]
    o = of_ref[...] + ob_ref[...]
    gate = hg_ref[...]
    gate = gate * jax.nn.sigmoid(gate)
    parts = []
    for s in range(half // LANES):
        sl = slice(s * LANES, (s + 1) * LANES)
        parts.append((_half_norm(o[:, sl], gn_ref[...], C_DV) * gate[:, sl]).astype(BF16))
    y = _dot(jnp.concatenate(parts, axis=1), w_ref[:half, :]) + _dot(nat_ref[...], w_ref[half:, :])
    x = t_ref[...] + m[5:6] * y
    o_ref[...] = _ffn_body(x, m, g_ref[2:3], w1_ref, w3_ref, w2_ref, 6)


def _out_cd(t, mod, g, o_f, o_b, hg, nat, gn, w, w1, w3, w2, layer):
    n, d = t.shape
    nt = n // TM - 1
    wd = nat.shape[1]
    return pl.pallas_call(
        _out_cd_kernel,
        grid=(nt,),
        in_specs=[_tile_spec(d, 1), _mod_spec(d, latent_only=True), _const_spec(g.shape),
                  _tile_spec(wd, 1), _tile_spec(wd, 1), _tile_spec(wd, 1), _tile_spec(wd),
                  _const_spec(gn.shape), _const_spec(w.shape)] + _ffn_specs(w1, w2, layer, 1),
        out_specs=_tile_spec(d),
        out_shape=jax.ShapeDtypeStruct((nt * TM, d), F32),
        name="out_cd_ffn",
        compiler_params=_cparams(("arbitrary",)),
    )(t, mod, g, o_f, o_b, hg, nat, gn, w, w1, w3, w2)


def _rope_tables(n_lat, dim, lane0):
    rows = n_lat // GRID_W
    dh = dim // 2
    q = dim // 4
    sign = jnp.where(jnp.arange(dh) < q, -1.0, 1.0)

    def one(n_pos):
        inv = ROPE_THETA ** (-jnp.arange(0, dh, 2, dtype=F32) / dh)
        ang = jnp.arange(n_pos, dtype=jnp.int32).astype(F32)[:, None] * inv[None, :]
        ang = jnp.concatenate([ang, ang], axis=-1)
        return jnp.cos(ang), jnp.sin(ang) * sign

    (cos_r, sin_r), (cos_c, sin_c) = one(rows), one(GRID_W)
    grid = lambda r, c: jnp.concatenate([jnp.broadcast_to(r[:, None], (rows, GRID_W, dh)),
                                         jnp.broadcast_to(c[None], (rows, GRID_W, dh))], axis=-1).reshape(n_lat, dim)
    cos, sin = grid(cos_r, cos_c), grid(sin_r, sin_c)
    if dim == HEAD_DIM:
        cos, sin = jnp.concatenate([cos, cos], -1), jnp.concatenate([sin, sin], -1)
    else:
        pad = ((0, 0), (lane0, LANES - lane0 - dim))
        cos = jnp.pad(cos, pad, constant_values=1.0)
        sin = jnp.pad(sin, pad)
    cos = jnp.concatenate([jnp.ones((TM, LANES), F32), cos], axis=0)
    sin = jnp.concatenate([jnp.zeros((TM, LANES), F32), sin], axis=0)
    return cos, sin


def _pad_lanes(x, lo, width=LANES):
    return jnp.pad(x, [(0, 0)] * (x.ndim - 1) + [(lo, width - lo - x.shape[-1])])


def _ab_weights(w_in, a_qn, a_kn, b_qn, b_kn, b_kvn, wukv):
    d = w_in.shape[0]
    grp = A_HEADS // A_KV_HEADS
    o = 0
    aq = w_in[:, o:o + A_HEADS * HEAD_DIM].reshape(d, A_HEADS, HEAD_DIM); o += A_HEADS * HEAD_DIM
    ak = w_in[:, o:o + A_KV_HEADS * HEAD_DIM]; o += A_KV_HEADS * HEAD_DIM
    av = w_in[:, o:o + A_KV_HEADS * HEAD_DIM]; o += A_KV_HEADS * HEAD_DIM
    bq = w_in[:, o:o + B_HEADS * B_QK].reshape(d, B_HEADS, B_QK); o += B_HEADS * B_QK
    bkv = w_in[:, o:o + B_KV_RANK]; o += B_KV_RANK
    bkr = w_in[:, o:o + B_ROPE]
    aq_slabs = jnp.concatenate([_pad_lanes(aq[:, h], (h // grp) * HEAD_DIM) for h in range(A_HEADS)], axis=1)
    bq_slabs = _pad_lanes(bq, 0).reshape(d, B_HEADS * LANES)
    half = B_ROPE // 4
    partner = np.where(np.arange(B_ROPE) % (2 * half) < half, np.arange(B_ROPE) + half, np.arange(B_ROPE) - half)
    wk = jnp.concatenate([ak, bkv, _pad_lanes(bkr, B_NOPE), _pad_lanes(bkr[:, partner], B_NOPE)], axis=1).astype(BF16)
    wqv = jnp.concatenate([aq_slabs, bq_slabs, av, bkv], axis=1).T.astype(BF16)
    kvw = wukv.reshape(B_KV_RANK, B_HEADS, B_NOPE + B_VDIM)
    wuk = _pad_lanes(kvw[:, :, :B_NOPE], 0).reshape(B_KV_RANK, B_HEADS * LANES).astype(BF16)
    wuv = kvw[:, :, B_NOPE:].reshape(B_KV_RANK, B_HEADS * B_VDIM).T.astype(BF16)
    gq_a = jnp.concatenate([_pad_lanes(a_qn, (h // grp) * HEAD_DIM) for h in range(A_HEADS)])
    gq_b = jnp.tile(_pad_lanes(b_qn, 0), B_HEADS)
    gq = jnp.concatenate([gq_a * (HEAD_DIM ** -0.5 * LOG2E), gq_b * (B_QK ** -0.5 * LOG2E)])
    gq = jnp.broadcast_to(gq[:, None], (gq.shape[0], TM))
    gk_a = jnp.concatenate([a_kn, a_kn])[None]
    gk_b = jnp.stack([_pad_lanes(b_kn, 0), _pad_lanes(b_kn[B_NOPE:][partner], B_NOPE)])
    gkvt = jnp.broadcast_to(b_kvn[:, None], (B_KV_RANK, TM))
    return [wk, wqv, wuk, wuv, gq, gk_a, gk_b, b_kvn[None], gkvt]


def kernel(x, c, ctx, c_ctx, norm_g, ada_w, ada_b, ffn_w1, ffn_w3, ffn_w2, ab_w_in, ab_a_qn, ab_a_kn, ab_b_qn,
           ab_b_kn, ab_b_kvn, ab_b_wukv, ab_w_out, cd_w_in, hgrn_lb, cd_c_gn, cd_d_qn, cd_d_kn, cd_d_rpb, cd_w_out):
    batch, seq, d = x.shape
    depth = norm_g.shape[0]
    assert batch == 1 and ctx.shape[1] == TM and seq % TM == 0 and depth == 2

    cc = jnp.zeros((8, d), F32).at[0].set(c_ctx).at[1].set(c[0])
    mods = _ada(cc, ada_w, ada_b)[:, :2].reshape(depth, 2, N_MOD, d)
    bf = lambda a: a.astype(BF16)
    w1, w3, w2 = bf(ffn_w1), bf(ffn_w3), bf(ffn_w2)

    consts = _ab_weights(ab_w_in[0], ab_a_qn[0], ab_a_kn[0], ab_b_qn[0], ab_b_kn[0], ab_b_kvn[0], ab_b_wukv[0])
    tables = _rope_tables(seq, HEAD_DIM, 0) + _rope_tables(seq, B_ROPE, B_NOPE)
    t, qt, k, vta, vtb = _first(ctx[0], x[0], mods[0], norm_g[0], w1, w3, w2, consts, tables,
                                tuple(a.T for a in tables))
    oa = _attn_a(qt, k, vta)
    ob = _attn_b(qt, k, vtb)
    t = _out_ab(t, mods[0], norm_g[0], oa, ob, bf(ab_w_out[0]), w1, w3, w2)

    gq = jnp.concatenate([cd_d_qn[0], cd_d_qn[0]])[None]
    gk = jnp.concatenate([cd_d_kn[0], cd_d_kn[0]])[None]
    gn = jnp.concatenate([cd_c_gn[0], cd_c_gn[0]])[None]
    t, hq, gf, gb, hv, hg, dq, dk, dv = _proj_cd(t, mods[1], norm_g[1], w1, w3, w2, bf(cd_w_in[0]), hgrn_lb, gq, gk, 1)
    o_f, o_b = _hgrn(hq, gf, gb, hv)
    nat = _natten(dq, dk, dv, _natten_bias(cd_d_rpb[0]))
    xl = _out_cd(t, mods[1], norm_g[1], o_f, o_b, hg, nat, gn, bf(cd_w_out[0]), w1, w3, w2, 1)
    return xl[None]
```

```python
import functools
import math

import numpy as np
import jax
import jax.numpy as jnp
from jax import lax
from jax.experimental import pallas as pl
from jax.experimental.pallas import tpu as pltpu

F32 = jnp.float32
BF16 = jnp.bfloat16

GRID_W = 64
HEAD_DIM = 64
EPS = 1e-6
ROPE_THETA = 10000.0
N_MOD = 9
A_HEADS, A_KV_HEADS = 8, 2
B_HEADS, B_NOPE, B_ROPE, B_VDIM, B_KV_RANK = 8, 64, 32, 64, 256
B_QK = B_NOPE + B_ROPE
C_HEADS, C_DK, C_DV, C_CHUNK = 8, 64, 64, 64
D_HEADS, D_WIN_H, D_WIN_W = 8, 8, 16

LANES = 128
TM = 256
VMEM_LIMIT = 56 * 1024 * 1024
NEG = -1e30
LOG2E = math.log2(math.e)


def _cparams(sem):
    return pltpu.CompilerParams(dimension_semantics=sem, vmem_limit_bytes=VMEM_LIMIT)


def _const_spec(shape):
    nd = len(shape)
    return pl.BlockSpec(shape, lambda *_: (0,) * nd, pipeline_mode=pl.Buffered(1))


def _lane(shape):
    return lax.broadcasted_iota(jnp.int32, shape, len(shape) - 1)


def _rms_mod(x, g, shift, scale):
    y = x * lax.rsqrt(jnp.mean(x * x, axis=-1, keepdims=True) + EPS) * g
    return y * (1.0 + scale) + shift


def _half_norm(x, gain, n):
    left = _lane(x.shape) < n
    x2 = x * x
    sl = jnp.sum(jnp.where(left, x2, 0.0), axis=-1, keepdims=True)
    sr = jnp.sum(jnp.where(left, 0.0, x2), axis=-1, keepdims=True)
    r = jnp.where(left, lax.rsqrt(sl / n + EPS), lax.rsqrt(sr / n + EPS))
    return x * r * gain


def _rope(x, cos, sin_signed, half):
    first = (_lane(x.shape) % (2 * half)) < half
    w = x.shape[-1]
    partner = jnp.where(first, pltpu.roll(x, w - half, 1), pltpu.roll(x, half, 1))
    return x * cos + partner * sin_signed


def _dot(a, b):
    return jnp.dot(a, b, preferred_element_type=F32)


def _dot_nt(a, b):
    return lax.dot_general(a, b, (((1,), (1,)), ((), ())), preferred_element_type=F32)


def _dot_tn(a, b):
    return lax.dot_general(a, b, (((0,), (0,)), ((), ())), preferred_element_type=F32)


ADA_K_CHUNKS = 4


def _ada_kernel(c_ref, w_ref, b_ref, o_ref):
    c = c_ref[...]
    y = jnp.dot(c * jax.nn.sigmoid(c), w_ref[0], preferred_element_type=F32, precision=lax.Precision.HIGHEST)

    @pl.when(pl.program_id(1) == 0)
    def _():
        o_ref[0] = y + b_ref[0]

    @pl.when(pl.program_id(1) > 0)
    def _():
        o_ref[0] += y


def _ada(cc, ada_w, ada_b):
    depth, d, n = ada_w.shape
    dk = d // ADA_K_CHUNKS
    return pl.pallas_call(
        _ada_kernel,
        grid=(depth, ADA_K_CHUNKS),
        in_specs=[pl.BlockSpec((8, dk), lambda l, j: (0, j)),
                  pl.BlockSpec((1, dk, n), lambda l, j: (l, j, 0)),
                  pl.BlockSpec((1, 1, n), lambda l, j: (l, 0, 0))],
        out_specs=pl.BlockSpec((1, 8, n), lambda l, j: (l, 0, 0)),
        out_shape=jax.ShapeDtypeStruct((depth, 8, n), F32),
        name="ada_mod",
        compiler_params=_cparams(("arbitrary", "arbitrary")),
    )(cc, ada_w, ada_b.reshape(depth, 1, n))


def _ffn_body(x, m, g, w1_ref, w3_ref, w2_ref, base):
    h = _rms_mod(x, g, m[base:base + 1], m[base + 1:base + 2]).astype(BF16)
    a = _dot(h, w1_ref[...])
    b = _dot(h, w3_ref[...])
    u = (a * jax.nn.sigmoid(a) * b).astype(BF16)
    return x + 0.5 * m[base + 2:base + 3] * _dot(u, w2_ref[...])


def _ffn_specs(w1, w2, l, j):
    sel = lambda a: pl.BlockSpec((None, None) + a.shape[2:], lambda *_: (l, j, 0, 0), pipeline_mode=pl.Buffered(1))
    return [sel(w1), sel(w1), sel(w2)]


def _tile_spec(width, first_tile=0):
    return pl.BlockSpec((TM, width), lambda i: (i + first_tile, 0))


def _mod_spec(d, latent_only=False):
    return pl.BlockSpec((1, N_MOD, d), (lambda i: (1, 0, 0)) if latent_only else (lambda i: (jnp.minimum(i, 1), 0, 0)))


BF16_SUBLANES = 16
VROWS = HEAD_DIM + BF16_SUBLANES


def _store_values_t(ref, n_heads, xt):
    for e in range(n_heads):
        r0 = e * VROWS
        ref[0, r0:r0 + HEAD_DIM, :] = xt[e * HEAD_DIM:(e + 1) * HEAD_DIM].astype(BF16)
        ref[0, r0 + HEAD_DIM:r0 + VROWS, :] = jnp.ones((VROWS - HEAD_DIM, xt.shape[1]), BF16)


def _rope_t(x, cos, sin_signed, half, lo, hi):
    blocks = [x[:lo]] if lo else []
    for r in range(lo, hi, 2 * half):
        blocks += [x[r + half:r + 2 * half], x[r:r + half]]
    if hi < x.shape[0]:
        blocks.append(x[hi:])
    return x * cos + jnp.concatenate(blocks, axis=0) * sin_signed

def _proj_ab_body(t, m, g, wk_ref, wqv_ref, wuk_ref, wuv_ref, gq_ref, gk_a_ref, gk_b_ref, gkv_ref, gkvt_ref,
                  cos_a_ref, sin_a_ref, cos_b_ref, sin_b_ref, cost_a_ref, sint_a_ref, cost_b_ref, sint_b_ref,
                  qt_ref, k_ref, vta_ref, vtb_ref):
    hf = _rms_mod(t, g, m[3:4], m[4:5])
    h = hf.astype(BF16)
    ht = hf.T.astype(BF16)
    pk = _dot(h, wk_ref[...])
    c = pk[:, LANES:LANES + B_KV_RANK]
    cn = (c * lax.rsqrt(jnp.mean(c * c, axis=-1, keepdims=True) + EPS) * gkv_ref[...]).astype(BF16)
    kn = _dot(cn, wuk_ref[...])
    nq = A_HEADS + B_HEADS
    pt_v = _dot(wqv_ref[nq * LANES:, :], ht)
    pt_q = [_dot(wqv_ref[lo * LANES:hi * LANES, :], ht) for lo, hi in ((0, A_HEADS), (A_HEADS, nq))]
    cos_a, sin_a = cos_a_ref[...], sin_a_ref[...]
    x = _half_norm(pk[:, :LANES], gk_a_ref[...], HEAD_DIM)
    k_ref[:, B_HEADS * LANES:(B_HEADS + 1) * LANES] = _rope(x, cos_a, sin_a, HEAD_DIM // 4).astype(BF16)
    kr = pk[:, LANES + B_KV_RANK:2 * LANES + B_KV_RANK]
    gcos = gk_b_ref[0:1] * cos_b_ref[...]
    rot = pk[:, 2 * LANES + B_KV_RANK:] * gk_b_ref[1:2] * sin_b_ref[...]
    xs = [kn[:, hd * LANES:(hd + 1) * LANES] + kr for hd in range(B_HEADS)]
    rs = [lax.rsqrt(jnp.sum(x * x, axis=-1, keepdims=True) / B_QK + EPS) for x in xs]
    for hd, (x, r) in enumerate(zip(xs, rs)):
        k_ref[:, hd * LANES:(hd + 1) * LANES] = (r * (x * gcos + rot)).astype(BF16)
    _store_values_t(vta_ref, A_KV_HEADS, pt_v[:LANES])
    ct = pt_v[LANES:]
    cnt = (ct * lax.rsqrt(jnp.mean(ct * ct, axis=0, keepdims=True) + EPS) * gkvt_ref[...]).astype(BF16)
    _store_values_t(vtb_ref, B_HEADS, _dot(wuv_ref[...], cnt))
    for grp, (hd0, n_real) in enumerate(((0, HEAD_DIM), (A_HEADS, B_QK))):
        xs = [pt_q[grp][j * LANES:(j + 1) * LANES] for j in range(pt_q[grp].shape[0] // LANES)]
        rs = [lax.rsqrt(jnp.sum(x * x, axis=0, keepdims=True) / n_real + EPS) for x in xs]
        xs = [x * r * gq_ref[(hd0 + j) * LANES:(hd0 + j + 1) * LANES, :] for j, (x, r) in enumerate(zip(xs, rs))]
        for j, x in enumerate(xs):
            if grp == 0:
                x = _rope_t(x, cost_a_ref[...], sint_a_ref[...], HEAD_DIM // 4, 0, LANES)
            else:
                x = _rope_t(x, cost_b_ref[...], sint_b_ref[...], B_ROPE // 4, B_NOPE, B_QK)
            qt_ref[0, (hd0 + j) * LANES:(hd0 + j + 1) * LANES, :] = x.astype(BF16)


def _first_kernel(c_ref, x_ref, m_ref, g_ref, w1_ref, w3_ref, w2_ref, *refs):
    t_ref = refs[-5]
    x = jnp.where(pl.program_id(0) == 0, c_ref[...], x_ref[...])
    m, g = m_ref[0], g_ref[...]
    t = _ffn_body(x, m, g[0:1], w1_ref, w3_ref, w2_ref, 0)
    t_ref[...] = t
    _proj_ab_body(t, m, g[1:2], *refs[:-5], *refs[-4:])


def _first(ctx, x, mod, g, w1, w3, w2, consts, tables, tables_t):
    seq, d = x.shape
    nt = seq // TM + 1
    n = nt * TM
    nq = (A_HEADS + B_HEADS) * LANES
    return pl.pallas_call(
        _first_kernel,
        grid=(nt,),
        in_specs=[pl.BlockSpec((TM, d), lambda i: (0, 0)),
                  pl.BlockSpec((TM, d), lambda i: (jnp.maximum(i - 1, 0), 0)),
                  _mod_spec(d), _const_spec(g.shape)] + _ffn_specs(w1, w2, 0, 0)
                 + [_const_spec(a.shape) for a in consts]
                 + [_tile_spec(LANES)] * len(tables)
                 + [pl.BlockSpec((LANES, TM), lambda i: (0, i))] * len(tables_t),
        out_specs=[_tile_spec(d),
                   pl.BlockSpec((1, nq, TM), lambda i: (i, 0, 0)),
                   _tile_spec((B_HEADS + 1) * LANES),
                   pl.BlockSpec((1, A_KV_HEADS * VROWS, TM), lambda i: (i, 0, 0)),
                   pl.BlockSpec((1, B_HEADS * VROWS, TM), lambda i: (i, 0, 0))],
        out_shape=[jax.ShapeDtypeStruct((n, d), F32),
                   jax.ShapeDtypeStruct((nt, nq, TM), BF16),
                   jax.ShapeDtypeStruct((n, (B_HEADS + 1) * LANES), BF16),
                   jax.ShapeDtypeStruct((nt, A_KV_HEADS * VROWS, TM), BF16),
                   jax.ShapeDtypeStruct((nt, B_HEADS * VROWS, TM), BF16)],
        name="ffn_proj_ab",
        compiler_params=_cparams(("arbitrary",)),
    )(ctx, x, mod, g, w1, w3, w2, *consts, *tables, *tables_t)


def _attn_kernel(qt_ref, k_ref, vt_ref, o_ref, s_ref, mx_ref, *, n_heads, k_slab, v_rows, n_tiles):
    heads = range(n_heads)
    is_ctx = pl.program_id(1) == 0
    n_pairs = jnp.where(is_ctx, 0, (n_tiles - 1) // 2)
    last = jnp.where(is_ctx, 0, n_tiles - 1)

    def scores(kt, slot):
        off = pl.multiple_of(kt * TM, TM)
        for hl in heads:
            r = _dot(k_ref[pl.ds(off, TM), k_slab[hl] * LANES:(k_slab[hl] + 1) * LANES],
                     qt_ref[0, hl * LANES:(hl + 1) * LANES, :])
            s_ref[slot, hl] = r
            mx_ref[slot, hl] = jnp.max(r, axis=0, keepdims=True)

    def consume(kt, slot, carry):
        m, acc = carry
        m_new = tuple(jnp.maximum(m[hl], mx_ref[slot, hl]) for hl in heads)
        alpha = tuple(jnp.exp2(m[hl] - m_new[hl]) for hl in heads)
        new_acc = []
        for hl in heads:
            pm = jnp.exp2(s_ref[slot, hl] - m_new[hl]).astype(BF16)
            new_acc.append(alpha[hl] * acc[hl] + _dot(vt_ref[kt, v_rows[hl] * VROWS:(v_rows[hl] + 1) * VROWS, :], pm))
        return m_new, tuple(new_acc)

    pairs_per_trip = max(u for u in (32, 16, 8, 4, 2, 1) if ((n_tiles - 1) // 2) % u == 0)

    def body(j, carry):
        for u in range(pairs_per_trip):
            t0 = 2 * (j * pairs_per_trip + u)
            scores(t0 + 1, 1)
            carry = consume(t0, 0, carry)
            scores(t0 + 2, 0)
            carry = consume(t0 + 1, 1, carry)
        return carry

    rep = lambda v: (v,) * n_heads
    init = (rep(jnp.full((1, TM), -jnp.inf, F32)), rep(jnp.zeros((VROWS, TM), F32)))
    scores(0, 0)
    _, acc = consume(last, 0, lax.fori_loop(0, n_pairs // pairs_per_trip, body, init))
    for pr in range(n_heads // 2):
        outs = [acc[2 * pr + e][:HEAD_DIM] / acc[2 * pr + e][HEAD_DIM:HEAD_DIM + 1] for e in range(2)]
        o_ref[:, pr * LANES:(pr + 1) * LANES] = jnp.concatenate(outs, axis=0).T.astype(BF16)


def _attn_scratch(n_heads):
    return [pltpu.VMEM((2, n_heads, TM, TM), F32), pltpu.VMEM((2, n_heads, 1, TM), F32)]


def _attn_a(qt, k, vta):
    nt = qt.shape[0]
    assert nt % 2 == 1
    n = nt * TM
    gh = A_HEADS // A_KV_HEADS
    kern = functools.partial(_attn_kernel, n_heads=gh, k_slab=(0,) * gh, v_rows=(0,) * gh, n_tiles=nt)
    return pl.pallas_call(
        kern,
        grid=(A_KV_HEADS, nt),
        in_specs=[pl.BlockSpec((1, gh * LANES, TM), lambda g, i: (i, g, 0)),
                  pl.BlockSpec((n, LANES), lambda g, i: (0, B_HEADS)),
                  pl.BlockSpec((nt, VROWS, TM), lambda g, i: (0, g, 0))],
        out_specs=pl.BlockSpec((TM, gh * HEAD_DIM), lambda g, i: (i, g)),
        out_shape=jax.ShapeDtypeStruct((n, A_HEADS * HEAD_DIM), BF16),
        scratch_shapes=_attn_scratch(gh),
        name="attn_gqa",
        compiler_params=_cparams(("arbitrary", "arbitrary")),
    )(qt, k, vta)


def _attn_b(qt, k, vtb):
    nt = qt.shape[0]
    n = nt * TM
    gh = 4
    hs = tuple(range(gh))
    kern = functools.partial(_attn_kernel, n_heads=gh, k_slab=hs, v_rows=hs, n_tiles=nt)
    return pl.pallas_call(
        kern,
        grid=(B_HEADS // gh, nt),
        in_specs=[pl.BlockSpec((1, gh * LANES, TM), lambda j, i: (i, A_HEADS // gh + j, 0)),
                  pl.BlockSpec((n, gh * LANES), lambda j, i: (0, j), pipeline_mode=pl.Buffered(1)),
                  pl.BlockSpec((nt, gh * VROWS, TM), lambda j, i: (0, j, 0), pipeline_mode=pl.Buffered(1))],
        out_specs=pl.BlockSpec((TM, gh * B_VDIM), lambda j, i: (i, j)),
        out_shape=jax.ShapeDtypeStruct((n, B_HEADS * B_VDIM), BF16),
        scratch_shapes=_attn_scratch(gh),
        name="attn_mla",
        compiler_params=_cparams(("arbitrary", "arbitrary")),
    )(qt, k, vtb)


def _out_ab_kernel(t_ref, m_ref, g_ref, oa_ref, ob_ref, w_ref, w1_ref, w3_ref, w2_ref, o_ref):
    m = m_ref[0]
    half = oa_ref.shape[1]
    y = _dot(oa_ref[...], w_ref[:half, :]) + _dot(ob_ref[...], w_ref[half:, :])
    x = t_ref[...] + m[5:6] * y
    o_ref[...] = _ffn_body(x, m, g_ref[2:3], w1_ref, w3_ref, w2_ref, 6)


def _out_ab(t, mod, g, oa, ob, w, w1, w3, w2):
    n, d = t.shape
    return pl.pallas_call(
        _out_ab_kernel,
        grid=(n // TM,),
        in_specs=[_tile_spec(d), _mod_spec(d), _const_spec(g.shape), _tile_spec(oa.shape[1]),
                  _tile_spec(ob.shape[1]), _const_spec(w.shape)] + _ffn_specs(w1, w2, 0, 1),
        out_specs=_tile_spec(d),
        out_shape=jax.ShapeDtypeStruct((n, d), F32),
        name="out_ab_ffn",
        compiler_params=_cparams(("arbitrary",)),
    )(t, mod, g, oa, ob, w, w1, w3, w2)


def _proj_cd_kernel(t_ref, m_ref, g_ref, w1_ref, w3_ref, w2_ref, w_ref, lbp_ref, gq_ref, gk_ref,
                    t_out_ref, hq_ref, gf_ref, gb_ref, hv_ref, hg_ref, dq_ref, dk_ref, dv_ref, *, layer):
    m = m_ref[0]
    t = _ffn_body(t_ref[...], m, g_ref[0:1], w1_ref, w3_ref, w2_ref, 0)
    t_out_ref[...] = t
    h = _rms_mod(t, g_ref[1:2], m[3:4], m[4:5]).astype(BF16)
    p = _dot(h, w_ref[...])
    w = C_HEADS * C_DK
    x = p[:, 0:w]
    hq_ref[...] = x * jax.nn.sigmoid(x)
    lbp = lbp_ref[...]
    e = jnp.exp(lbp - jnp.max(lbp, axis=0, keepdims=True))
    sm = e / jnp.sum(e, axis=0, keepdims=True)
    lb = jnp.sum(sm[1:layer + 1], axis=0)
    for dr, ref in ((0, gf_ref), (1, gb_ref)):
        z = p[:, (1 + dr) * w:(2 + dr) * w]
        lbd = lb[dr:dr + 1]
        ref[...] = jnp.log2(lbd + (1.0 - lbd) * jax.nn.sigmoid(z))
    hv_ref[...] = p[:, 3 * w:4 * w].astype(BF16)
    hg_ref[...] = p[:, 4 * w:5 * w]
    for s in range(w // LANES):
        sl = slice(s * LANES, (s + 1) * LANES)
        q = _half_norm(p[:, 5 * w + s * LANES:5 * w + (s + 1) * LANES], gq_ref[...], HEAD_DIM)
        dq_ref[:, sl] = (q * (HEAD_DIM ** -0.5 * LOG2E)).astype(BF16)
        dk_ref[:, sl] = _half_norm(p[:, 6 * w + s * LANES:6 * w + (s + 1) * LANES], gk_ref[...], HEAD_DIM).astype(BF16)
    dv_ref[...] = p[:, 7 * w:8 * w].astype(BF16)


def _proj_cd(t, mod, g, w1, w3, w2, w_in, hgrn_lb, gq, gk, layer):
    n, d = t.shape
    w = C_HEADS * C_DK
    shp = lambda dt: jax.ShapeDtypeStruct((n, w), dt)
    return pl.pallas_call(
        functools.partial(_proj_cd_kernel, layer=layer),
        grid=(n // TM,),
        in_specs=[_tile_spec(d), _mod_spec(d), _const_spec(g.shape)] + _ffn_specs(w1, w2, layer, 0)
                 + [_const_spec(w_in.shape), _const_spec(hgrn_lb.shape), _const_spec(gq.shape), _const_spec(gk.shape)],
        out_specs=[_tile_spec(d)] + [_tile_spec(w)] * 8,
        out_shape=[jax.ShapeDtypeStruct((n, d), F32),
                   shp(F32), shp(F32), shp(F32), shp(BF16), shp(F32), shp(BF16), shp(BF16), shp(BF16)],
        name="ffn_proj_cd",
        compiler_params=_cparams(("arbitrary",)),
    )(t, mod, g, w1, w3, w2, w_in, hgrn_lb, gq, gk)


N_LEVELS = int(math.log2(C_CHUNK))


def _hgrn_tables():
    t = np.arange(C_CHUNK)
    return (t[None, :] <= t[:, None]).astype(np.float32)


def _level_boundary(gi, h, reverse):
    c = gi.shape[0]
    if h >= 4:
        blocks = []
        for s in range(0, c, 2 * h):
            b = s + (h if reverse else h - 1)
            blocks.append(jnp.broadcast_to(gi[b:b + 1], (2 * h, gi.shape[1])))
        return jnp.concatenate(blocks, axis=0)
    pos = lax.broadcasted_iota(jnp.int32, gi.shape, 0) % (2 * h)
    gb = gi
    for p in range(2 * h):
        d = (h if reverse else h - 1) - p
        if d != 0:
            gb = jnp.where(pos == p, pltpu.roll(gi, (c - d) % c, 0), gb)
    return gb


def _level_masks(reverse):
    c = C_CHUNK
    ti = lax.broadcasted_iota(jnp.int32, (2 * c, c), 0) % c
    si = lax.broadcasted_iota(jnp.int32, (2 * c, c), 1)
    tq, sk = (si, ti) if reverse else (ti, si)
    masks = []
    for lv in range(N_LEVELS):
        h = c >> (lv + 1)
        sh = N_LEVELS - lv
        masks.append(((ti >> sh) == (si >> sh)) & ((tq & h) != 0) & ((sk & h) == 0))
    masks.append(ti == si)
    return masks


def _hgrn_chunks(jobs, st_ref):
    c = C_CHUNK
    gis = []
    for q, g, v, tri, masks, d, reverse in jobs:
        ghi = g.astype(BF16)
        glo = (g - ghi.astype(F32)).astype(BF16)
        gis.append(_dot(tri, ghi) + _dot(tri, glo))
    ops = []
    for (q, g, v, tri, masks, d, reverse), gi in zip(jobs, gis):
        f = jnp.exp2(g)
        kk = 1.0 - f
        glast = gi[(c - 1):c] if not reverse else gi[0:1]
        qs = (q * jnp.exp2(gi)).astype(BF16)
        ks = (kk * jnp.exp2(glast - gi)).astype(BF16)
        last_visited = (lax.broadcasted_iota(jnp.int32, g.shape, 0) & 1) == (0 if reverse else 1)
        eqs, eks = [], []
        for lv in range(N_LEVELS):
            if lv == N_LEVELS - 1:
                e = jnp.where(last_visited, f, 1.0)
            else:
                e = jnp.exp2(-jnp.abs(gi - _level_boundary(gi, c >> (lv + 1), reverse)))
            eqs.append((e * q).astype(BF16))
            eks.append((e * kk).astype(BF16))
        eqs.append(q.astype(BF16))
        eks.append(kk.astype(BF16))
        ops.append((eqs, eks, qs, ks, jnp.exp2(glast)))
    left = _lane((c, LANES)) < C_DK
    blockdiag = (lax.broadcasted_iota(jnp.int32, (LANES, LANES), 0) < C_DV) == (_lane((LANES, LANES)) < C_DK)
    outs = [[] for _ in jobs]
    for pr in range(C_HEADS // 2):
        sl = slice(pr * LANES, (pr + 1) * LANES)
        a2s = [jnp.zeros((2 * c, c), F32) for _ in jobs]
        for lv in range(N_LEVELS + 1):
            for n, job in enumerate(jobs):
                eq = ops[n][0][lv][:, sl]
                eq2 = jnp.concatenate([jnp.where(left, eq, 0), jnp.where(left, 0, eq)], axis=0)
                a = _dot_nt(eq2, ops[n][1][lv][:, sl])
                a2s[n] = jnp.where(job[4][lv], a, a2s[n])
        r2s = [_dot(a2s[n].astype(BF16), job[2][:, sl]) for n, job in enumerate(jobs)]
        upds = [_dot_tn(job[2][:, sl], ops[n][3][:, sl]) for n, job in enumerate(jobs)]
        for n, job in enumerate(jobs):
            d = job[5]
            st = st_ref[d, pr]
            outs[n].append(jnp.where(left, r2s[n][:c], r2s[n][c:]) + _dot_nt(ops[n][2][:, sl], st.astype(BF16)))
            st_ref[d, pr] = st * ops[n][4][:, sl] + jnp.where(blockdiag, upds[n], 0.0)
    return [jnp.concatenate(o, axis=1) for o in outs]


HGRN_CHUNKS_PER_PASS = 4


def _hgrn_kernel(qf_ref, gf_ref, vf_ref, qb_ref, gb_ref, vb_ref, wf_ref, wb_ref, of_ref, ob_ref, st_ref):
    @pl.when(pl.program_id(0) == 0)
    def _():
        st_ref[...] = jnp.zeros(st_ref.shape, F32)

    c = C_CHUNK
    mf, mb = _level_masks(False), _level_masks(True)
    tf, tb = wf_ref[...], wb_ref[...]
    for j0 in range(0, TM // c, HGRN_CHUNKS_PER_PASS):
        js = range(j0, j0 + HGRN_CHUNKS_PER_PASS)
        fs = [slice(j * c, (j + 1) * c) for j in js]
        bs = [slice(TM - (j + 1) * c, TM - j * c) for j in js]
        jobs = ([(qf_ref[s, :], gf_ref[s, :], vf_ref[s, :], tf, mf, 0, False) for s in fs]
                + [(qb_ref[s, :], gb_ref[s, :], vb_ref[s, :], tb, mb, 1, True) for s in bs])
        outs = _hgrn_chunks(jobs, st_ref)
        for s, o in zip(fs, outs[:len(fs)]):
            of_ref[s, :] = o
        for s, o in zip(bs, outs[len(fs):]):
            ob_ref[s, :] = o


def _hgrn(hq, gf, gb, hv):
    n, w = hq.shape
    nt = n // TM
    wf = _hgrn_tables()
    wb = wf.reshape(-1, C_CHUNK, C_CHUNK)[:, ::-1, ::-1].reshape(wf.shape)
    fwd = pl.BlockSpec((TM, w), lambda i: (i, 0))
    bwd = pl.BlockSpec((TM, w), lambda i: (jnp.where(i == 0, 0, nt - i), 0))
    return pl.pallas_call(
        _hgrn_kernel,
        grid=(nt,),
        in_specs=[fwd, fwd, fwd, bwd, bwd, bwd, _const_spec(wf.shape), _const_spec(wb.shape)],
        out_specs=[fwd, bwd],
        out_shape=[jax.ShapeDtypeStruct((n, w), F32)] * 2,
        scratch_shapes=[pltpu.VMEM((2, C_HEADS // 2, LANES, LANES), F32)],
        name="hgrn2",
        compiler_params=_cparams(("arbitrary",)),
    )(hq, gf, hv, hq, gb, hv, jnp.asarray(wf, BF16), jnp.asarray(wb, BF16))


NAT_TILES = 8
NAT_ROWS = NAT_TILES * TM // GRID_W


def _natten_kernel(*refs, rows):
    q_refs, (k_ref, v_ref, bias_ref, o_ref) = refs[:NAT_TILES], refs[NAT_TILES:]
    rows_per_tile = TM // GRID_W
    i = pl.program_id(1)
    kc = k_ref[0:TM, :]
    vc = v_ref[0:TM, :]
    left_q = _lane((GRID_W, LANES)) < HEAD_DIM
    q2s, offs, sws = [], [], []
    for j in range(NAT_ROWS):
        r = i * NAT_ROWS + j
        rs = jnp.clip(r - D_WIN_H // 2, 0, rows - D_WIN_H)
        offs.append(pl.multiple_of(TM + rs * GRID_W, GRID_W))
        jt = j % rows_per_tile
        q = q_refs[j // rows_per_tile][jt * GRID_W:(jt + 1) * GRID_W, :]
        q2s.append(jnp.concatenate([jnp.where(left_q, q, 0), jnp.where(left_q, 0, q)], axis=0))
        kw = k_ref[pl.ds(offs[j], D_WIN_H * GRID_W), :]
        a0 = rs - r + (D_WIN_H - 1)
        bias = jnp.concatenate([bias_ref[0, a0 + 2 * u] for u in range(D_WIN_H // 2)], axis=1)
        sws.append(_dot_nt(q2s[j], kw) + bias)
    sc = _dot_nt(jnp.concatenate(q2s, axis=0), kc)
    pws, pcs, ls = [], [], []
    for j in range(NAT_ROWS):
        scj = sc[j * 2 * GRID_W:(j + 1) * 2 * GRID_W]
        m = jnp.maximum(jnp.max(sws[j], axis=-1, keepdims=True), jnp.max(scj, axis=-1, keepdims=True))
        pw = jnp.exp2(sws[j] - m)
        pc = jnp.exp2(scj - m)
        ls.append(jnp.sum(pw, axis=-1, keepdims=True) + jnp.sum(pc, axis=-1, keepdims=True))
        pws.append(pw.astype(BF16))
        pcs.append(pc.astype(BF16))
    oc = _dot(jnp.concatenate(pcs, axis=0), vc)
    for j in range(NAT_ROWS):
        vw = v_ref[pl.ds(offs[j], D_WIN_H * GRID_W), :]
        o2 = (_dot(pws[j], vw) + oc[j * 2 * GRID_W:(j + 1) * 2 * GRID_W]) / ls[j]
        o_ref[j * GRID_W:(j + 1) * GRID_W, :] = jnp.where(left_q, o2[:GRID_W], o2[GRID_W:]).astype(BF16)


def _natten(dq, dk, dv, bias):
    n, w = dq.shape
    nt = n // TM - 1
    assert nt % NAT_TILES == 0
    q_spec = lambda u: pl.BlockSpec((TM, LANES), lambda p, i: (NAT_TILES * i + u + 1, p))
    return pl.pallas_call(
        functools.partial(_natten_kernel, rows=nt * TM // GRID_W),
        grid=(w // LANES, nt // NAT_TILES),
        in_specs=[q_spec(u) for u in range(NAT_TILES)]
                 + [pl.BlockSpec((n, LANES), lambda p, i: (0, p)),
                    pl.BlockSpec((n, LANES), lambda p, i: (0, p)),
                    pl.BlockSpec((1,) + bias.shape[1:], lambda p, i: (p, 0, 0, 0))],
        out_specs=pl.BlockSpec((NAT_TILES * TM, LANES), lambda p, i: (i, p)),
        out_shape=jax.ShapeDtypeStruct((nt * TM, w), BF16),
        name="natten",
        compiler_params=_cparams(("arbitrary", "arbitrary")),
    )(*([dq] * NAT_TILES), dk, dv, bias)


def _natten_bias(rpb):
    cols = np.arange(GRID_W)
    start = np.clip(cols - D_WIN_W // 2, 0, GRID_W - D_WIN_W)
    kc = cols[None, :]
    inside = (kc >= start[:, None]) & (kc < start[:, None] + D_WIN_W)
    rel = kc - cols[:, None] + (D_WIN_W - 1)
    onehot = (rel[:, :, None] == np.arange(2 * D_WIN_W - 1)).astype(np.float32)
    tab = jnp.einsum("har,ckr->hack", rpb, onehot, precision=lax.Precision.HIGHEST)
    tab = jnp.where(inside[None, None], tab * LOG2E, NEG)
    na = tab.shape[1]
    t = jnp.transpose(tab.reshape(D_HEADS // 2, 2, na, GRID_W, GRID_W), (0, 2, 1, 3, 4))
    t = t.reshape(D_HEADS // 2, na, 2 * GRID_W, GRID_W)
    return jnp.concatenate([t[:, :-1], t[:, 1:]], axis=-1)


def _out_cd_kernel(t_ref, m_ref, g_ref, of_ref, ob_ref, hg_ref, nat_ref, gn_ref, w_ref, w1_ref, w3_ref, w2_ref, o_ref):
    m = m_ref[0]
    half = nat_ref.shape[1]
    o = of_ref[...] + ob_ref[...]
    gate = hg_ref[...]
    gate = gate * jax.nn.sigmoid(gate)
    parts = []
    for s in range(half // LANES):
        sl = slice(s * LANES, (s + 1) * LANES)
        parts.append((_half_norm(o[:, sl], gn_ref[...], C_DV) * gate[:, sl]).astype(BF16))
    y = _dot(jnp.concatenate(parts, axis=1), w_ref[:half, :]) + _dot(nat_ref[...], w_ref[half:, :])
    x = t_ref[...] + m[5:6] * y
    o_ref[...] = _ffn_body(x, m, g_ref[2:3], w1_ref, w3_ref, w2_ref, 6)


def _out_cd(t, mod, g, o_f, o_b, hg, nat, gn, w, w1, w3, w2, layer):
    n, d = t.shape
    nt = n // TM - 1
    wd = nat.shape[1]
    return pl.pallas_call(
        _out_cd_kernel,
        grid=(nt,),
        in_specs=[_tile_spec(d, 1), _mod_spec(d, latent_only=True), _const_spec(g.shape),
                  _tile_spec(wd, 1), _tile_spec(wd, 1), _tile_spec(wd, 1), _tile_spec(wd),
                  _const_spec(gn.shape), _const_spec(w.shape)] + _ffn_specs(w1, w2, layer, 1),
        out_specs=_tile_spec(d),
        out_shape=jax.ShapeDtypeStruct((nt * TM, d), F32),
        name="out_cd_ffn",
        compiler_params=_cparams(("arbitrary",)),
    )(t, mod, g, o_f, o_b, hg, nat, gn, w, w1, w3, w2)


def _rope_tables(n_lat, dim, lane0):
    rows = n_lat // GRID_W
    dh = dim // 2
    q = dim // 4
    sign = jnp.where(jnp.arange(dh) < q, -1.0, 1.0)

    def one(n_pos):
        inv = ROPE_THETA ** (-jnp.arange(0, dh, 2, dtype=F32) / dh)
        ang = jnp.arange(n_pos, dtype=jnp.int32).astype(F32)[:, None] * inv[None, :]
        ang = jnp.concatenate([ang, ang], axis=-1)
        return jnp.cos(ang), jnp.sin(ang) * sign

    (cos_r, sin_r), (cos_c, sin_c) = one(rows), one(GRID_W)
    grid = lambda r, c: jnp.concatenate([jnp.broadcast_to(r[:, None], (rows, GRID_W, dh)),
                                         jnp.broadcast_to(c[None], (rows, GRID_W, dh))], axis=-1).reshape(n_lat, dim)
    cos, sin = grid(cos_r, cos_c), grid(sin_r, sin_c)
    if dim == HEAD_DIM:
        cos, sin = jnp.concatenate([cos, cos], -1), jnp.concatenate([sin, sin], -1)
    else:
        pad = ((0, 0), (lane0, LANES - lane0 - dim))
        cos = jnp.pad(cos, pad, constant_values=1.0)
        sin = jnp.pad(sin, pad)
    cos = jnp.concatenate([jnp.ones((TM, LANES), F32), cos], axis=0)
    sin = jnp.concatenate([jnp.zeros((TM, LANES), F32), sin], axis=0)
    return cos, sin


def _pad_lanes(x, lo, width=LANES):
    return jnp.pad(x, [(0, 0)] * (x.ndim - 1) + [(lo, width - lo - x.shape[-1])])


def _ab_weights(w_in, a_qn, a_kn, b_qn, b_kn, b_kvn, wukv):
    d = w_in.shape[0]
    grp = A_HEADS // A_KV_HEADS
    o = 0
    aq = w_in[:, o:o + A_HEADS * HEAD_DIM].reshape(d, A_HEADS, HEAD_DIM); o += A_HEADS * HEAD_DIM
    ak = w_in[:, o:o + A_KV_HEADS * HEAD_DIM]; o += A_KV_HEADS * HEAD_DIM
    av = w_in[:, o:o + A_KV_HEADS * HEAD_DIM]; o += A_KV_HEADS * HEAD_DIM
    bq = w_in[:, o:o + B_HEADS * B_QK].reshape(d, B_HEADS, B_QK); o += B_HEADS * B_QK
    bkv = w_in[:, o:o + B_KV_RANK]; o += B_KV_RANK
    bkr = w_in[:, o:o + B_ROPE]
    aq_slabs = jnp.concatenate([_pad_lanes(aq[:, h], (h // grp) * HEAD_DIM) for h in range(A_HEADS)], axis=1)
    bq_slabs = _pad_lanes(bq, 0).reshape(d, B_HEADS * LANES)
    half = B_ROPE // 4
    partner = np.where(np.arange(B_ROPE) % (2 * half) < half, np.arange(B_ROPE) + half, np.arange(B_ROPE) - half)
    wk = jnp.concatenate([ak, bkv, _pad_lanes(bkr, B_NOPE), _pad_lanes(bkr[:, partner], B_NOPE)], axis=1).astype(BF16)
    wqv = jnp.concatenate([aq_slabs, bq_slabs, av, bkv], axis=1).T.astype(BF16)
    kvw = wukv.reshape(B_KV_RANK, B_HEADS, B_NOPE + B_VDIM)
    wuk = _pad_lanes(kvw[:, :, :B_NOPE], 0).reshape(B_KV_RANK, B_HEADS * LANES).astype(BF16)
    wuv = kvw[:, :, B_NOPE:].reshape(B_KV_RANK, B_HEADS * B_VDIM).T.astype(BF16)
    gq_a = jnp.concatenate([_pad_lanes(a_qn, (h // grp) * HEAD_DIM) for h in range(A_HEADS)])
    gq_b = jnp.tile(_pad_lanes(b_qn, 0), B_HEADS)
    gq = jnp.concatenate([gq_a * (HEAD_DIM ** -0.5 * LOG2E), gq_b * (B_QK ** -0.5 * LOG2E)])
    gq = jnp.broadcast_to(gq[:, None], (gq.shape[0], TM))
    gk_a = jnp.concatenate([a_kn, a_kn])[None]
    gk_b = jnp.stack([_pad_lanes(b_kn, 0), _pad_lanes(b_kn[B_NOPE:][partner], B_NOPE)])
    gkvt = jnp.broadcast_to(b_kvn[:, None], (B_KV_RANK, TM))
    return [wk, wqv, wuk, wuv, gq, gk_a, gk_b, b_kvn[None], gkvt]


def kernel(x, c, ctx, c_ctx, norm_g, ada_w, ada_b, ffn_w1, ffn_w3, ffn_w2, ab_w_in, ab_a_qn, ab_a_kn, ab_b_qn,
           ab_b_kn, ab_b_kvn, ab_b_wukv, ab_w_out, cd_w_in, hgrn_lb, cd_c_gn, cd_d_qn, cd_d_kn, cd_d_rpb, cd_w_out):
    batch, seq, d = x.shape
    depth = norm_g.shape[0]
    assert batch == 1 and ctx.shape[1] == TM and seq % TM == 0 and depth == 2

    cc = jnp.zeros((8, d), F32).at[0].set(c_ctx).at[1].set(c[0])
    mods = _ada(cc, ada_w, ada_b)[:, :2].reshape(depth, 2, N_MOD, d)
    bf = lambda a: a.astype(BF16)
    w1, w3, w2 = bf(ffn_w1), bf(ffn_w3), bf(ffn_w2)

    consts = _ab_weights(ab_w_in[0], ab_a_qn[0], ab_a_kn[0], ab_b_qn[0], ab_b_kn[0], ab_b_kvn[0], ab_b_wukv[0])
    tables = _rope_tables(seq, HEAD_DIM, 0) + _rope_tables(seq, B_ROPE, B_NOPE)
    t, qt, k, vta, vtb = _first(ctx[0], x[0], mods[0], norm_g[0], w1, w3, w2, consts, tables,
                                tuple(a.T for a in tables))
    oa = _attn_a(qt, k, vta)
    ob = _attn_b(qt, k, vtb)
    t = _out_ab(t, mods[0], norm_g[0], oa, ob, bf(ab_w_out[0]), w1, w3, w2)

    gq = jnp.concatenate([cd_d_qn[0], cd_d_qn[0]])[None]
    gk = jnp.concatenate([cd_d_kn[0], cd_d_kn[0]])[None]
    gn = jnp.concatenate([cd_c_gn[0], cd_c_gn[0]])[None]
    t, hq, gf, gb, hv, hg, dq, dk, dv = _proj_cd(t, mods[1], norm_g[1], w1, w3, w2, bf(cd_w_in[0]), hgrn_lb, gq, gk, 1)
    o_f, o_b = _hgrn(hq, gf, gb, hv)
    nat = _natten(dq, dk, dv, _natten_bias(cd_d_rpb[0]))
    xl = _out_cd(t, mods[1], norm_g[1], o_f, o_b, hg, nat, gn, bf(cd_w_out[0]), w1, w3, w2, 1)
    return xl[None]
```

```python
import functools
import math

import numpy as np
import jax
import jax.numpy as jnp
from jax import lax
from jax.experimental import pallas as pl
from jax.experimental.pallas import tpu as pltpu

F32 = jnp.float32
BF16 = jnp.bfloat16

GRID_W = 64
HEAD_DIM = 64
EPS = 1e-6
ROPE_THETA = 10000.0
N_MOD = 9
A_HEADS, A_KV_HEADS = 8, 2
B_HEADS, B_NOPE, B_ROPE, B_VDIM, B_KV_RANK = 8, 64, 32, 64, 256
B_QK = B_NOPE + B_ROPE
C_HEADS, C_DK, C_DV, C_CHUNK = 8, 64, 64, 64
D_HEADS, D_WIN_H, D_WIN_W = 8, 8, 16

LANES = 128
TM = 256
VMEM_LIMIT = 56 * 1024 * 1024
NEG = -1e30
LOG2E = math.log2(math.e)


def _cparams(sem):
    return pltpu.CompilerParams(dimension_semantics=sem, vmem_limit_bytes=VMEM_LIMIT)


def _const_spec(shape):
    nd = len(shape)
    return pl.BlockSpec(shape, lambda *_: (0,) * nd, pipeline_mode=pl.Buffered(1))


def _lane(shape):
    return lax.broadcasted_iota(jnp.int32, shape, len(shape) - 1)


def _rms_mod(x, g, shift, scale):
    y = x * lax.rsqrt(jnp.mean(x * x, axis=-1, keepdims=True) + EPS) * g
    return y * (1.0 + scale) + shift


def _half_norm(x, gain, n):
    left = _lane(x.shape) < n
    x2 = x * x
    sl = jnp.sum(jnp.where(left, x2, 0.0), axis=-1, keepdims=True)
    sr = jnp.sum(jnp.where(left, 0.0, x2), axis=-1, keepdims=True)
    r = jnp.where(left, lax.rsqrt(sl / n + EPS), lax.rsqrt(sr / n + EPS))
    return x * r * gain


def _rope(x, cos, sin_signed, half):
    first = (_lane(x.shape) % (2 * half)) < half
    w = x.shape[-1]
    partner = jnp.where(first, pltpu.roll(x, w - half, 1), pltpu.roll(x, half, 1))
    return x * cos + partner * sin_signed


def _dot(a, b):
    return jnp.dot(a, b, preferred_element_type=F32)


def _dot_nt(a, b):
    return lax.dot_general(a, b, (((1,), (1,)), ((), ())), preferred_element_type=F32)


def _dot_tn(a, b):
    return lax.dot_general(a, b, (((0,), (0,)), ((), ())), preferred_element_type=F32)


ADA_N_CHUNKS = 8
ADA_BUFFERS = 3


def _ada_kernel(c_ref, w_ref, b_ref, o_ref):
    c = c_ref[...]
    o_ref[0] = jnp.dot(c * jax.nn.sigmoid(c), w_ref[0], preferred_element_type=F32,
                       precision=lax.Precision.HIGHEST) + b_ref[0]


def _ada(cc, ada_w, ada_b):
    depth, d, n = ada_w.shape
    bn = n // ADA_N_CHUNKS

    def outer(c_hbm, w_hbm, b_hbm, o_hbm):
        pltpu.emit_pipeline(
            _ada_kernel,
            grid=(depth, ADA_N_CHUNKS),
            in_specs=[pl.BlockSpec((8, d), lambda l, j: (0, 0)),
                      pl.BlockSpec((1, d, bn), lambda l, j: (l, 0, j), pipeline_mode=pl.Buffered(ADA_BUFFERS)),
                      pl.BlockSpec((1, 1, bn), lambda l, j: (l, 0, j))],
            out_specs=[pl.BlockSpec((1, 8, bn), lambda l, j: (l, 0, j))],
        )(c_hbm, w_hbm, b_hbm, o_hbm)

    return pl.pallas_call(
        outer,
        in_specs=[pl.BlockSpec(memory_space=pl.ANY)] * 3,
        out_specs=pl.BlockSpec(memory_space=pl.ANY),
        out_shape=jax.ShapeDtypeStruct((depth, 8, n), F32),
        name="ada_mod",
        compiler_params=pltpu.CompilerParams(vmem_limit_bytes=VMEM_LIMIT),
    )(cc, ada_w, ada_b.reshape(depth, 1, n))


def _ffn_body(x, m, g, w1_ref, w3_ref, w2_ref, base):
    h = _rms_mod(x, g, m[base:base + 1], m[base + 1:base + 2]).astype(BF16)
    a = _dot(h, w1_ref[...])
    b = _dot(h, w3_ref[...])
    u = (a * jax.nn.sigmoid(a) * b).astype(BF16)
    return x + 0.5 * m[base + 2:base + 3] * _dot(u, w2_ref[...])


def _ffn_specs(w1, w2, l, j):
    sel = lambda a: pl.BlockSpec((None, None) + a.shape[2:], lambda *_: (l, j, 0, 0), pipeline_mode=pl.Buffered(1))
    return [sel(w1), sel(w1), sel(w2)]


def _tile_spec(width, first_tile=0):
    return pl.BlockSpec((TM, width), lambda i: (i + first_tile, 0))


def _mod_spec(d, latent_only=False):
    return pl.BlockSpec((1, N_MOD, d), (lambda i: (1, 0, 0)) if latent_only else (lambda i: (jnp.minimum(i, 1), 0, 0)))


BF16_SUBLANES = 16
VROWS = HEAD_DIM + BF16_SUBLANES


def _store_values_t(ref, n_heads, xt):
    for e in range(n_heads):
        r0 = e * VROWS
        ref[0, r0:r0 + HEAD_DIM, :] = xt[e * HEAD_DIM:(e + 1) * HEAD_DIM].astype(BF16)
        ref[0, r0 + HEAD_DIM:r0 + VROWS, :] = jnp.ones((VROWS - HEAD_DIM, xt.shape[1]), BF16)


def _rope_t(x, cos, sin_signed, half, lo, hi):
    blocks = [x[:lo]] if lo else []
    for r in range(lo, hi, 2 * half):
        blocks += [x[r + half:r + 2 * half], x[r:r + half]]
    if hi < x.shape[0]:
        blocks.append(x[hi:])
    return x * cos + jnp.concatenate(blocks, axis=0) * sin_signed

def _proj_ab_body(t, m, g, wk_ref, wqv_ref, wuk_ref, wuv_ref, gq_ref, gk_a_ref, gk_b_ref, gkv_ref, gkvt_ref,
                  cos_a_ref, sin_a_ref, cos_b_ref, sin_b_ref, cost_a_ref, sint_a_ref, cost_b_ref, sint_b_ref,
                  qt_ref, k_ref, vta_ref, vtb_ref):
    hf = _rms_mod(t, g, m[3:4], m[4:5])
    h = hf.astype(BF16)
    ht = hf.T.astype(BF16)
    pk = _dot(h, wk_ref[...])
    c = pk[:, LANES:LANES + B_KV_RANK]
    cn = (c * lax.rsqrt(jnp.mean(c * c, axis=-1, keepdims=True) + EPS) * gkv_ref[...]).astype(BF16)
    kn = _dot(cn, wuk_ref[...])
    nq = A_HEADS + B_HEADS
    pt_v = _dot(wqv_ref[nq * LANES:, :], ht)
    pt_q = [_dot(wqv_ref[lo * LANES:hi * LANES, :], ht) for lo, hi in ((0, A_HEADS), (A_HEADS, nq))]
    cos_a, sin_a = cos_a_ref[...], sin_a_ref[...]
    x = _half_norm(pk[:, :LANES], gk_a_ref[...], HEAD_DIM)
    k_ref[:, B_HEADS * LANES:(B_HEADS + 1) * LANES] = _rope(x, cos_a, sin_a, HEAD_DIM // 4).astype(BF16)
    kr = pk[:, LANES + B_KV_RANK:2 * LANES + B_KV_RANK]
    gcos = gk_b_ref[0:1] * cos_b_ref[...]
    rot = pk[:, 2 * LANES + B_KV_RANK:] * gk_b_ref[1:2] * sin_b_ref[...]
    xs = [kn[:, hd * LANES:(hd + 1) * LANES] + kr for hd in range(B_HEADS)]
    rs = [lax.rsqrt(jnp.sum(x * x, axis=-1, keepdims=True) / B_QK + EPS) for x in xs]
    for hd, (x, r) in enumerate(zip(xs, rs)):
        k_ref[:, hd * LANES:(hd + 1) * LANES] = (r * (x * gcos + rot)).astype(BF16)
    _store_values_t(vta_ref, A_KV_HEADS, pt_v[:LANES])
    ct = pt_v[LANES:]
    cnt = (ct * lax.rsqrt(jnp.mean(ct * ct, axis=0, keepdims=True) + EPS) * gkvt_ref[...]).astype(BF16)
    _store_values_t(vtb_ref, B_HEADS, _dot(wuv_ref[...], cnt))
    for grp, (hd0, n_real) in enumerate(((0, HEAD_DIM), (A_HEADS, B_QK))):
        xs = [pt_q[grp][j * LANES:(j + 1) * LANES] for j in range(pt_q[grp].shape[0] // LANES)]
        rs = [lax.rsqrt(jnp.sum(x * x, axis=0, keepdims=True) / n_real + EPS) for x in xs]
        xs = [x * r * gq_ref[(hd0 + j) * LANES:(hd0 + j + 1) * LANES, :] for j, (x, r) in enumerate(zip(xs, rs))]
        for j, x in enumerate(xs):
            if grp == 0:
                x = _rope_t(x, cost_a_ref[...], sint_a_ref[...], HEAD_DIM // 4, 0, LANES)
            else:
                x = _rope_t(x, cost_b_ref[...], sint_b_ref[...], B_ROPE // 4, B_NOPE, B_QK)
            qt_ref[0, (hd0 + j) * LANES:(hd0 + j + 1) * LANES, :] = x.astype(BF16)


def _first_kernel(c_ref, x_ref, m_ref, g_ref, w1_ref, w3_ref, w2_ref, *refs):
    t_ref = refs[-5]
    x = jnp.where(pl.program_id(0) == 0, c_ref[...], x_ref[...])
    m, g = m_ref[0], g_ref[...]
    t = _ffn_body(x, m, g[0:1], w1_ref, w3_ref, w2_ref, 0)
    t_ref[...] = t
    _proj_ab_body(t, m, g[1:2], *refs[:-5], *refs[-4:])


def _first(ctx, x, mod, g, w1, w3, w2, consts, tables, tables_t):
    seq, d = x.shape
    nt = seq // TM + 1
    n = nt * TM
    nq = (A_HEADS + B_HEADS) * LANES
    return pl.pallas_call(
        _first_kernel,
        grid=(nt,),
        in_specs=[pl.BlockSpec((TM, d), lambda i: (0, 0)),
                  pl.BlockSpec((TM, d), lambda i: (jnp.maximum(i - 1, 0), 0)),
                  _mod_spec(d), _const_spec(g.shape)] + _ffn_specs(w1, w2, 0, 0)
                 + [_const_spec(a.shape) for a in consts]
                 + [_tile_spec(LANES)] * len(tables)
                 + [pl.BlockSpec((LANES, TM), lambda i: (0, i))] * len(tables_t),
        out_specs=[_tile_spec(d),
                   pl.BlockSpec((1, nq, TM), lambda i: (i, 0, 0)),
                   _tile_spec((B_HEADS + 1) * LANES),
                   pl.BlockSpec((1, A_KV_HEADS * VROWS, TM), lambda i: (i, 0, 0)),
                   pl.BlockSpec((1, B_HEADS * VROWS, TM), lambda i: (i, 0, 0))],
        out_shape=[jax.ShapeDtypeStruct((n, d), F32),
                   jax.ShapeDtypeStruct((nt, nq, TM), BF16),
                   jax.ShapeDtypeStruct((n, (B_HEADS + 1) * LANES), BF16),
                   jax.ShapeDtypeStruct((nt, A_KV_HEADS * VROWS, TM), BF16),
                   jax.ShapeDtypeStruct((nt, B_HEADS * VROWS, TM), BF16)],
        name="ffn_proj_ab",
        compiler_params=_cparams(("arbitrary",)),
    )(ctx, x, mod, g, w1, w3, w2, *consts, *tables, *tables_t)


def _attn_kernel(qt_ref, k_ref, vt_ref, o_ref, s_ref, mx_ref, *, n_heads, k_slab, v_rows, n_tiles):
    heads = range(n_heads)
    is_ctx = pl.program_id(1) == 0
    n_pairs = jnp.where(is_ctx, 0, (n_tiles - 1) // 2)
    last = jnp.where(is_ctx, 0, n_tiles - 1)

    def scores(kt, slot):
        off = pl.multiple_of(kt * TM, TM)
        for hl in heads:
            r = _dot(k_ref[pl.ds(off, TM), k_slab[hl] * LANES:(k_slab[hl] + 1) * LANES],
                     qt_ref[0, hl * LANES:(hl + 1) * LANES, :])
            s_ref[slot, hl] = r
            mx_ref[slot, hl] = jnp.max(r, axis=0, keepdims=True)

    def consume(kt, slot, carry):
        m, acc = carry
        m_new = tuple(jnp.maximum(m[hl], mx_ref[slot, hl]) for hl in heads)
        alpha = tuple(jnp.exp2(m[hl] - m_new[hl]) for hl in heads)
        new_acc = []
        for hl in heads:
            pm = jnp.exp2(s_ref[slot, hl] - m_new[hl]).astype(BF16)
            new_acc.append(alpha[hl] * acc[hl] + _dot(vt_ref[kt, v_rows[hl] * VROWS:(v_rows[hl] + 1) * VROWS, :], pm))
        return m_new, tuple(new_acc)

    pairs_per_trip = max(u for u in (32, 16, 8, 4, 2, 1) if ((n_tiles - 1) // 2) % u == 0)

    def body(j, carry):
        for u in range(pairs_per_trip):
            t0 = 2 * (j * pairs_per_trip + u)
            scores(t0 + 1, 1)
            carry = consume(t0, 0, carry)
            scores(t0 + 2, 0)
            carry = consume(t0 + 1, 1, carry)
        return carry

    rep = lambda v: (v,) * n_heads
    init = (rep(jnp.full((1, TM), -jnp.inf, F32)), rep(jnp.zeros((VROWS, TM), F32)))
    scores(0, 0)
    _, acc = consume(last, 0, lax.fori_loop(0, n_pairs // pairs_per_trip, body, init))
    for pr in range(n_heads // 2):
        outs = [acc[2 * pr + e][:HEAD_DIM] / acc[2 * pr + e][HEAD_DIM:HEAD_DIM + 1] for e in range(2)]
        o_ref[:, pr * LANES:(pr + 1) * LANES] = jnp.concatenate(outs, axis=0).T.astype(BF16)


def _attn_scratch(n_heads):
    return [pltpu.VMEM((2, n_heads, TM, TM), F32), pltpu.VMEM((2, n_heads, 1, TM), F32)]


def _attn_a(qt, k, vta):
    nt = qt.shape[0]
    assert nt % 2 == 1
    n = nt * TM
    gh = A_HEADS // A_KV_HEADS
    kern = functools.partial(_attn_kernel, n_heads=gh, k_slab=(0,) * gh, v_rows=(0,) * gh, n_tiles=nt)
    return pl.pallas_call(
        kern,
        grid=(A_KV_HEADS, nt),
        in_specs=[pl.BlockSpec((1, gh * LANES, TM), lambda g, i: (i, g, 0)),
                  pl.BlockSpec((n, LANES), lambda g, i: (0, B_HEADS)),
                  pl.BlockSpec((nt, VROWS, TM), lambda g, i: (0, g, 0))],
        out_specs=pl.BlockSpec((TM, gh * HEAD_DIM), lambda g, i: (i, g)),
        out_shape=jax.ShapeDtypeStruct((n, A_HEADS * HEAD_DIM), BF16),
        scratch_shapes=_attn_scratch(gh),
        name="attn_gqa",
        compiler_params=_cparams(("arbitrary", "arbitrary")),
    )(qt, k, vta)


def _attn_b(qt, k, vtb):
    nt = qt.shape[0]
    n = nt * TM
    gh = 4
    hs = tuple(range(gh))
    kern = functools.partial(_attn_kernel, n_heads=gh, k_slab=hs, v_rows=hs, n_tiles=nt)
    return pl.pallas_call(
        kern,
        grid=(B_HEADS // gh, nt),
        in_specs=[pl.BlockSpec((1, gh * LANES, TM), lambda j, i: (i, A_HEADS // gh + j, 0)),
                  pl.BlockSpec((n, gh * LANES), lambda j, i: (0, j), pipeline_mode=pl.Buffered(1)),
                  pl.BlockSpec((nt, gh * VROWS, TM), lambda j, i: (0, j, 0), pipeline_mode=pl.Buffered(1))],
        out_specs=pl.BlockSpec((TM, gh * B_VDIM), lambda j, i: (i, j)),
        out_shape=jax.ShapeDtypeStruct((n, B_HEADS * B_VDIM), BF16),
        scratch_shapes=_attn_scratch(gh),
        name="attn_mla",
        compiler_params=_cparams(("arbitrary", "arbitrary")),
    )(qt, k, vtb)


def _out_ab_kernel(t_ref, m_ref, g_ref, oa_ref, ob_ref, w_ref, w1_ref, w3_ref, w2_ref, o_ref):
    m = m_ref[0]
    half = oa_ref.shape[1]
    y = _dot(oa_ref[...], w_ref[:half, :]) + _dot(ob_ref[...], w_ref[half:, :])
    x = t_ref[...] + m[5:6] * y
    o_ref[...] = _ffn_body(x, m, g_ref[2:3], w1_ref, w3_ref, w2_ref, 6)


def _out_ab(t, mod, g, oa, ob, w, w1, w3, w2):
    n, d = t.shape
    return pl.pallas_call(
        _out_ab_kernel,
        grid=(n // TM,),
        in_specs=[_tile_spec(d), _mod_spec(d), _const_spec(g.shape), _tile_spec(oa.shape[1]),
                  _tile_spec(ob.shape[1]), _const_spec(w.shape)] + _ffn_specs(w1, w2, 0, 1),
        out_specs=_tile_spec(d),
        out_shape=jax.ShapeDtypeStruct((n, d), F32),
        name="out_ab_ffn",
        compiler_params=_cparams(("arbitrary",)),
    )(t, mod, g, oa, ob, w, w1, w3, w2)


def _proj_cd_kernel(t_ref, m_ref, g_ref, w1_ref, w3_ref, w2_ref, w_ref, lbp_ref, gq_ref, gk_ref,
                    t_out_ref, hq_ref, gf_ref, gb_ref, hv_ref, hg_ref, dq_ref, dk_ref, dv_ref, *, layer):
    m = m_ref[0]
    t = _ffn_body(t_ref[...], m, g_ref[0:1], w1_ref, w3_ref, w2_ref, 0)
    t_out_ref[...] = t
    h = _rms_mod(t, g_ref[1:2], m[3:4], m[4:5]).astype(BF16)
    p = _dot(h, w_ref[...])
    w = C_HEADS * C_DK
    x = p[:, 0:w]
    hq_ref[...] = x * jax.nn.sigmoid(x)
    lbp = lbp_ref[...]
    e = jnp.exp(lbp - jnp.max(lbp, axis=0, keepdims=True))
    sm = e / jnp.sum(e, axis=0, keepdims=True)
    lb = jnp.sum(sm[1:layer + 1], axis=0)
    for dr, ref in ((0, gf_ref), (1, gb_ref)):
        z = p[:, (1 + dr) * w:(2 + dr) * w]
        lbd = lb[dr:dr + 1]
        ref[...] = jnp.log2(lbd + (1.0 - lbd) * jax.nn.sigmoid(z))
    hv_ref[...] = p[:, 3 * w:4 * w].astype(BF16)
    hg_ref[...] = p[:, 4 * w:5 * w]
    for s in range(w // LANES):
        sl = slice(s * LANES, (s + 1) * LANES)
        q = _half_norm(p[:, 5 * w + s * LANES:5 * w + (s + 1) * LANES], gq_ref[...], HEAD_DIM)
        dq_ref[:, sl] = (q * (HEAD_DIM ** -0.5 * LOG2E)).astype(BF16)
        dk_ref[:, sl] = _half_norm(p[:, 6 * w + s * LANES:6 * w + (s + 1) * LANES], gk_ref[...], HEAD_DIM).astype(BF16)
    dv_ref[...] = p[:, 7 * w:8 * w].astype(BF16)


def _proj_cd(t, mod, g, w1, w3, w2, w_in, hgrn_lb, gq, gk, layer):
    n, d = t.shape
    w = C_HEADS * C_DK
    shp = lambda dt: jax.ShapeDtypeStruct((n, w), dt)
    return pl.pallas_call(
        functools.partial(_proj_cd_kernel, layer=layer),
        grid=(n // TM,),
        in_specs=[_tile_spec(d), _mod_spec(d), _const_spec(g.shape)] + _ffn_specs(w1, w2, layer, 0)
                 + [_const_spec(w_in.shape), _const_spec(hgrn_lb.shape), _const_spec(gq.shape), _const_spec(gk.shape)],
        out_specs=[_tile_spec(d)] + [_tile_spec(w)] * 8,
        out_shape=[jax.ShapeDtypeStruct((n, d), F32),
                   shp(F32), shp(F32), shp(F32), shp(BF16), shp(F32), shp(BF16), shp(BF16), shp(BF16)],
        name="ffn_proj_cd",
        compiler_params=_cparams(("arbitrary",)),
    )(t, mod, g, w1, w3, w2, w_in, hgrn_lb, gq, gk)


N_LEVELS = int(math.log2(C_CHUNK))


def _hgrn_tables():
    t = np.arange(C_CHUNK)
    return (t[None, :] <= t[:, None]).astype(np.float32)


def _level_boundary(gi, h, reverse):
    c = gi.shape[0]
    if h >= 4:
        blocks = []
        for s in range(0, c, 2 * h):
            b = s + (h if reverse else h - 1)
            blocks.append(jnp.broadcast_to(gi[b:b + 1], (2 * h, gi.shape[1])))
        return jnp.concatenate(blocks, axis=0)
    pos = lax.broadcasted_iota(jnp.int32, gi.shape, 0) % (2 * h)
    gb = gi
    for p in range(2 * h):
        d = (h if reverse else h - 1) - p
        if d != 0:
            gb = jnp.where(pos == p, pltpu.roll(gi, (c - d) % c, 0), gb)
    return gb


def _level_masks(reverse):
    c = C_CHUNK
    ti = lax.broadcasted_iota(jnp.int32, (2 * c, c), 0) % c
    si = lax.broadcasted_iota(jnp.int32, (2 * c, c), 1)
    tq, sk = (si, ti) if reverse else (ti, si)
    masks = []
    for lv in range(N_LEVELS):
        h = c >> (lv + 1)
        sh = N_LEVELS - lv
        masks.append(((ti >> sh) == (si >> sh)) & ((tq & h) != 0) & ((sk & h) == 0))
    masks.append(ti == si)
    return masks


def _hgrn_chunks(jobs, st_ref):
    c = C_CHUNK
    gis = []
    for q, g, v, tri, masks, d, reverse in jobs:
        ghi = g.astype(BF16)
        glo = (g - ghi.astype(F32)).astype(BF16)
        gis.append(_dot(tri, ghi) + _dot(tri, glo))
    ops = []
    for (q, g, v, tri, masks, d, reverse), gi in zip(jobs, gis):
        f = jnp.exp2(g)
        kk = 1.0 - f
        glast = gi[(c - 1):c] if not reverse else gi[0:1]
        qs = (q * jnp.exp2(gi)).astype(BF16)
        ks = (kk * jnp.exp2(glast - gi)).astype(BF16)
        last_visited = (lax.broadcasted_iota(jnp.int32, g.shape, 0) & 1) == (0 if reverse else 1)
        eqs, eks = [], []
        for lv in range(N_LEVELS):
            if lv == N_LEVELS - 1:
                e = jnp.where(last_visited, f, 1.0)
            else:
                e = jnp.exp2(-jnp.abs(gi - _level_boundary(gi, c >> (lv + 1), reverse)))
            eqs.append((e * q).astype(BF16))
            eks.append((e * kk).astype(BF16))
        eqs.append(q.astype(BF16))
        eks.append(kk.astype(BF16))
        ops.append((eqs, eks, qs, ks, jnp.exp2(glast)))
    left = _lane((c, LANES)) < C_DK
    blockdiag = (lax.broadcasted_iota(jnp.int32, (LANES, LANES), 0) < C_DV) == (_lane((LANES, LANES)) < C_DK)
    outs = [[] for _ in jobs]
    for pr in range(C_HEADS // 2):
        sl = slice(pr * LANES, (pr + 1) * LANES)
        a2s = [jnp.zeros((2 * c, c), F32) for _ in jobs]
        for lv in range(N_LEVELS + 1):
            for n, job in enumerate(jobs):
                eq = ops[n][0][lv][:, sl]
                eq2 = jnp.concatenate([jnp.where(left, eq, 0), jnp.where(left, 0, eq)], axis=0)
                a = _dot_nt(eq2, ops[n][1][lv][:, sl])
                a2s[n] = jnp.where(job[4][lv], a, a2s[n])
        r2s = [_dot(a2s[n].astype(BF16), job[2][:, sl]) for n, job in enumerate(jobs)]
        upds = [_dot_tn(job[2][:, sl], ops[n][3][:, sl]) for n, job in enumerate(jobs)]
        for n, job in enumerate(jobs):
            d = job[5]
            st = st_ref[d, pr]
            outs[n].append(jnp.where(left, r2s[n][:c], r2s[n][c:]) + _dot_nt(ops[n][2][:, sl], st.astype(BF16)))
            st_ref[d, pr] = st * ops[n][4][:, sl] + jnp.where(blockdiag, upds[n], 0.0)
    return [jnp.concatenate(o, axis=1) for o in outs]


HGRN_CHUNKS_PER_PASS = 4


def _hgrn_kernel(qf_ref, gf_ref, vf_ref, qb_ref, gb_ref, vb_ref, wf_ref, wb_ref, of_ref, ob_ref, st_ref):
    @pl.when(pl.program_id(0) == 0)
    def _():
        st_ref[...] = jnp.zeros(st_ref.shape, F32)

    c = C_CHUNK
    mf, mb = _level_masks(False), _level_masks(True)
    tf, tb = wf_ref[...], wb_ref[...]
    for j0 in range(0, TM // c, HGRN_CHUNKS_PER_PASS):
        js = range(j0, j0 + HGRN_CHUNKS_PER_PASS)
        fs = [slice(j * c, (j + 1) * c) for j in js]
        bs = [slice(TM - (j + 1) * c, TM - j * c) for j in js]
        jobs = ([(qf_ref[s, :], gf_ref[s, :], vf_ref[s, :], tf, mf, 0, False) for s in fs]
                + [(qb_ref[s, :], gb_ref[s, :], vb_ref[s, :], tb, mb, 1, True) for s in bs])
        outs = _hgrn_chunks(jobs, st_ref)
        for s, o in zip(fs, outs[:len(fs)]):
            of_ref[s, :] = o
        for s, o in zip(bs, outs[len(fs):]):
            ob_ref[s, :] = o


def _hgrn(hq, gf, gb, hv):
    n, w = hq.shape
    nt = n // TM
    wf = _hgrn_tables()
    wb = wf.reshape(-1, C_CHUNK, C_CHUNK)[:, ::-1, ::-1].reshape(wf.shape)
    fwd = pl.BlockSpec((TM, w), lambda i: (i, 0))
    bwd = pl.BlockSpec((TM, w), lambda i: (jnp.where(i == 0, 0, nt - i), 0))
    return pl.pallas_call(
        _hgrn_kernel,
        grid=(nt,),
        in_specs=[fwd, fwd, fwd, bwd, bwd, bwd, _const_spec(wf.shape), _const_spec(wb.shape)],
        out_specs=[fwd, bwd],
        out_shape=[jax.ShapeDtypeStruct((n, w), F32)] * 2,
        scratch_shapes=[pltpu.VMEM((2, C_HEADS // 2, LANES, LANES), F32)],
        name="hgrn2",
        compiler_params=_cparams(("arbitrary",)),
    )(hq, gf, hv, hq, gb, hv, jnp.asarray(wf, BF16), jnp.asarray(wb, BF16))


NAT_TILES = 8
NAT_ROWS = NAT_TILES * TM // GRID_W


def _natten_kernel(*refs, rows):
    q_refs, (k_ref, v_ref, bias_ref, o_ref) = refs[:NAT_TILES], refs[NAT_TILES:]
    rows_per_tile = TM // GRID_W
    i = pl.program_id(1)
    kc = k_ref[0:TM, :]
    vc = v_ref[0:TM, :]
    left_q = _lane((GRID_W, LANES)) < HEAD_DIM
    q2s, offs, sws = [], [], []
    for j in range(NAT_ROWS):
        r = i * NAT_ROWS + j
        rs = jnp.clip(r - D_WIN_H // 2, 0, rows - D_WIN_H)
        offs.append(pl.multiple_of(TM + rs * GRID_W, GRID_W))
        jt = j % rows_per_tile
        q = q_refs[j // rows_per_tile][jt * GRID_W:(jt + 1) * GRID_W, :]
        q2s.append(jnp.concatenate([jnp.where(left_q, q, 0), jnp.where(left_q, 0, q)], axis=0))
        kw = k_ref[pl.ds(offs[j], D_WIN_H * GRID_W), :]
        a0 = rs - r + (D_WIN_H - 1)
        bias = jnp.concatenate([bias_ref[0, a0 + 2 * u] for u in range(D_WIN_H // 2)], axis=1)
        sws.append(_dot_nt(q2s[j], kw) + bias)
    sc = _dot_nt(jnp.concatenate(q2s, axis=0), kc)
    pws, pcs, ls = [], [], []
    for j in range(NAT_ROWS):
        scj = sc[j * 2 * GRID_W:(j + 1) * 2 * GRID_W]
        m = jnp.maximum(jnp.max(sws[j], axis=-1, keepdims=True), jnp.max(scj, axis=-1, keepdims=True))
        pw = jnp.exp2(sws[j] - m)
        pc = jnp.exp2(scj - m)
        ls.append(jnp.sum(pw, axis=-1, keepdims=True) + jnp.sum(pc, axis=-1, keepdims=True))
        pws.append(pw.astype(BF16))
        pcs.append(pc.astype(BF16))
    oc = _dot(jnp.concatenate(pcs, axis=0), vc)
    for j in range(NAT_ROWS):
        vw = v_ref[pl.ds(offs[j], D_WIN_H * GRID_W), :]
        o2 = (_dot(pws[j], vw) + oc[j * 2 * GRID_W:(j + 1) * 2 * GRID_W]) / ls[j]
        o_ref[j * GRID_W:(j + 1) * GRID_W, :] = jnp.where(left_q, o2[:GRID_W], o2[GRID_W:]).astype(BF16)


def _natten(dq, dk, dv, bias):
    n, w = dq.shape
    nt = n // TM - 1
    assert nt % NAT_TILES == 0
    q_spec = lambda u: pl.BlockSpec((TM, LANES), lambda p, i: (NAT_TILES * i + u + 1, p))
    return pl.pallas_call(
        functools.partial(_natten_kernel, rows=nt * TM // GRID_W),
        grid=(w // LANES, nt // NAT_TILES),
        in_specs=[q_spec(u) for u in range(NAT_TILES)]
                 + [pl.BlockSpec((n, LANES), lambda p, i: (0, p)),
                    pl.BlockSpec((n, LANES), lambda p, i: (0, p)),
                    pl.BlockSpec((1,) + bias.shape[1:], lambda p, i: (p, 0, 0, 0))],
        out_specs=pl.BlockSpec((NAT_TILES * TM, LANES), lambda p, i: (i, p)),
        out_shape=jax.ShapeDtypeStruct((nt * TM, w), BF16),
        name="natten",
        compiler_params=_cparams(("arbitrary", "arbitrary")),
    )(*([dq] * NAT_TILES), dk, dv, bias)


def _natten_bias(rpb):
    cols = np.arange(GRID_W)
    start = np.clip(cols - D_WIN_W // 2, 0, GRID_W - D_WIN_W)
    kc = cols[None, :]
    inside = (kc >= start[:, None]) & (kc < start[:, None] + D_WIN_W)
    rel = kc - cols[:, None] + (D_WIN_W - 1)
    onehot = (rel[:, :, None] == np.arange(2 * D_WIN_W - 1)).astype(np.float32)
    tab = jnp.einsum("har,ckr->hack", rpb, onehot, precision=lax.Precision.HIGHEST)
    tab = jnp.where(inside[None, None], tab * LOG2E, NEG)
    na = tab.shape[1]
    t = jnp.transpose(tab.reshape(D_HEADS // 2, 2, na, GRID_W, GRID_W), (0, 2, 1, 3, 4))
    t = t.reshape(D_HEADS // 2, na, 2 * GRID_W, GRID_W)
    return jnp.concatenate([t[:, :-1], t[:, 1:]], axis=-1)


def _out_cd_kernel(t_ref, m_ref, g_ref, of_ref, ob_ref, hg_ref, nat_ref, gn_ref, w_ref, w1_ref, w3_ref, w2_ref, o_ref):
    m = m_ref[0]
    half = nat_ref.shape[1]
    o = of_ref[...] + ob_ref[...]
    gate = hg_ref[...]
    gate = gate * jax.nn.sigmoid(gate)
    parts = []
    for s in range(half // LANES):
        sl = slice(s * LANES, (s + 1) * LANES)
        parts.append((_half_norm(o[:, sl], gn_ref[...], C_DV) * gate[:, sl]).astype(BF16))
    y = _dot(jnp.concatenate(parts, axis=1), w_ref[:half, :]) + _dot(nat_ref[...], w_ref[half:, :])
    x = t_ref[...] + m[5:6] * y
    o_ref[...] = _ffn_body(x, m, g_ref[2:3], w1_ref, w3_ref, w2_ref, 6)


def _out_cd(t, mod, g, o_f, o_b, hg, nat, gn, w, w1, w3, w2, layer):
    n, d = t.shape
    nt = n // TM - 1
    wd = nat.shape[1]
    return pl.pallas_call(
        _out_cd_kernel,
        grid=(nt,),
        in_specs=[_tile_spec(d, 1), _mod_spec(d, latent_only=True), _const_spec(g.shape),
                  _tile_spec(wd, 1), _tile_spec(wd, 1), _tile_spec(wd, 1), _tile_spec(wd),
                  _const_spec(gn.shape), _const_spec(w.shape)] + _ffn_specs(w1, w2, layer, 1),
        out_specs=_tile_spec(d),
        out_shape=jax.ShapeDtypeStruct((nt * TM, d), F32),
        name="out_cd_ffn",
        compiler_params=_cparams(("arbitrary",)),
    )(t, mod, g, o_f, o_b, hg, nat, gn, w, w1, w3, w2)


def _rope_tables(n_lat, dim, lane0):
    rows = n_lat // GRID_W
    dh = dim // 2
    q = dim // 4
    sign = jnp.where(jnp.arange(dh) < q, -1.0, 1.0)

    def one(n_pos):
        inv = ROPE_THETA ** (-jnp.arange(0, dh, 2, dtype=F32) / dh)
        ang = jnp.arange(n_pos, dtype=jnp.int32).astype(F32)[:, None] * inv[None, :]
        ang = jnp.concatenate([ang, ang], axis=-1)
        return jnp.cos(ang), jnp.sin(ang) * sign

    (cos_r, sin_r), (cos_c, sin_c) = one(rows), one(GRID_W)
    grid = lambda r, c: jnp.concatenate([jnp.broadcast_to(r[:, None], (rows, GRID_W, dh)),
                                         jnp.broadcast_to(c[None], (rows, GRID_W, dh))], axis=-1).reshape(n_lat, dim)
    cos, sin = grid(cos_r, cos_c), grid(sin_r, sin_c)
    if dim == HEAD_DIM:
        cos, sin = jnp.concatenate([cos, cos], -1), jnp.concatenate([sin, sin], -1)
    else:
        pad = ((0, 0), (lane0, LANES - lane0 - dim))
        cos = jnp.pad(cos, pad, constant_values=1.0)
        sin = jnp.pad(sin, pad)
    cos = jnp.concatenate([jnp.ones((TM, LANES), F32), cos], axis=0)
    sin = jnp.concatenate([jnp.zeros((TM, LANES), F32), sin], axis=0)
    return cos, sin


def _pad_lanes(x, lo, width=LANES):
    return jnp.pad(x, [(0, 0)] * (x.ndim - 1) + [(lo, width - lo - x.shape[-1])])


def _ab_weights(w_in, a_qn, a_kn, b_qn, b_kn, b_kvn, wukv):
    d = w_in.shape[0]
    grp = A_HEADS // A_KV_HEADS
    o = 0
    aq = w_in[:, o:o + A_HEADS * HEAD_DIM].reshape(d, A_HEADS, HEAD_DIM); o += A_HEADS * HEAD_DIM
    ak = w_in[:, o:o + A_KV_HEADS * HEAD_DIM]; o += A_KV_HEADS * HEAD_DIM
    av = w_in[:, o:o + A_KV_HEADS * HEAD_DIM]; o += A_KV_HEADS * HEAD_DIM
    bq = w_in[:, o:o + B_HEADS * B_QK].reshape(d, B_HEADS, B_QK); o += B_HEADS * B_QK
    bkv = w_in[:, o:o + B_KV_RANK]; o += B_KV_RANK
    bkr = w_in[:, o:o + B_ROPE]
    aq_slabs = jnp.concatenate([_pad_lanes(aq[:, h], (h // grp) * HEAD_DIM) for h in range(A_HEADS)], axis=1)
    bq_slabs = _pad_lanes(bq, 0).reshape(d, B_HEADS * LANES)
    half = B_ROPE // 4
    partner = np.where(np.arange(B_ROPE) % (2 * half) < half, np.arange(B_ROPE) + half, np.arange(B_ROPE) - half)
    wk = jnp.concatenate([ak, bkv, _pad_lanes(bkr, B_NOPE), _pad_lanes(bkr[:, partner], B_NOPE)], axis=1).astype(BF16)
    wqv = jnp.concatenate([aq_slabs, bq_slabs, av, bkv], axis=1).T.astype(BF16)
    kvw = wukv.reshape(B_KV_RANK, B_HEADS, B_NOPE + B_VDIM)
    wuk = _pad_lanes(kvw[:, :, :B_NOPE], 0).reshape(B_KV_RANK, B_HEADS * LANES).astype(BF16)
    wuv = kvw[:, :, B_NOPE:].reshape(B_KV_RANK, B_HEADS * B_VDIM).T.astype(BF16)
    gq_a = jnp.concatenate([_pad_lanes(a_qn, (h // grp) * HEAD_DIM) for h in range(A_HEADS)])
    gq_b = jnp.tile(_pad_lanes(b_qn, 0), B_HEADS)
    gq = jnp.concatenate([gq_a * (HEAD_DIM ** -0.5 * LOG2E), gq_b * (B_QK ** -0.5 * LOG2E)])
    gq = jnp.broadcast_to(gq[:, None], (gq.shape[0], TM))
    gk_a = jnp.concatenate([a_kn, a_kn])[None]
    gk_b = jnp.stack([_pad_lanes(b_kn, 0), _pad_lanes(b_kn[B_NOPE:][partner], B_NOPE)])
    gkvt = jnp.broadcast_to(b_kvn[:, None], (B_KV_RANK, TM))
    return [wk, wqv, wuk, wuv, gq, gk_a, gk_b, b_kvn[None], gkvt]


def kernel(x, c, ctx, c_ctx, norm_g, ada_w, ada_b, ffn_w1, ffn_w3, ffn_w2, ab_w_in, ab_a_qn, ab_a_kn, ab_b_qn,
           ab_b_kn, ab_b_kvn, ab_b_wukv, ab_w_out, cd_w_in, hgrn_lb, cd_c_gn, cd_d_qn, cd_d_kn, cd_d_rpb, cd_w_out):
    batch, seq, d = x.shape
    depth = norm_g.shape[0]
    assert batch == 1 and ctx.shape[1] == TM and seq % TM == 0 and depth == 2

    cc = jnp.zeros((8, d), F32).at[0].set(c_ctx).at[1].set(c[0])
    mods = _ada(cc, ada_w, ada_b)[:, :2].reshape(depth, 2, N_MOD, d)
    bf = lambda a: a.astype(BF16)
    w1, w3, w2 = bf(ffn_w1), bf(ffn_w3), bf(ffn_w2)

    consts = _ab_weights(ab_w_in[0], ab_a_qn[0], ab_a_kn[0], ab_b_qn[0], ab_b_kn[0], ab_b_kvn[0], ab_b_wukv[0])
    tables = _rope_tables(seq, HEAD_DIM, 0) + _rope_tables(seq, B_ROPE, B_NOPE)
    t, qt, k, vta, vtb = _first(ctx[0], x[0], mods[0], norm_g[0], w1, w3, w2, consts, tables,
                                tuple(a.T for a in tables))
    oa = _attn_a(qt, k, vta)
    ob = _attn_b(qt, k, vtb)
    t = _out_ab(t, mods[0], norm_g[0], oa, ob, bf(ab_w_out[0]), w1, w3, w2)

    gq = jnp.concatenate([cd_d_qn[0], cd_d_qn[0]])[None]
    gk = jnp.concatenate([cd_d_kn[0], cd_d_kn[0]])[None]
    gn = jnp.concatenate([cd_c_gn[0], cd_c_gn[0]])[None]
    t, hq, gf, gb, hv, hg, dq, dk, dv = _proj_cd(t, mods[1], norm_g[1], w1, w3, w2, bf(cd_w_in[0]), hgrn_lb, gq, gk, 1)
    o_f, o_b = _hgrn(hq, gf, gb, hv)
    nat = _natten(dq, dk, dv, _natten_bias(cd_d_rpb[0]))
    xl = _out_cd(t, mods[1], norm_g[1], o_f, o_b, hg, nat, gn, bf(cd_w_out[0]), w1, w3, w2, 1)
    return xl[None]
```
